```python
import jax, jax.numpy as jnp
from jax import lax
import numpy as np

D_MODEL = 1024
BATCH = 8
SEQ = 8192
DEPTH = 1
DEC_BATCH = 8
DEC_SEQ = 16
PAST_LEN = 1024

CHUNK = 64
SGU_LEN = 128
SGU_GROUPS = 8
SGU_WIDTH = 512
SGU_GDIM = SGU_WIDTH // SGU_GROUPS
N_HEADS = 8
N_KV_HEADS = 4
HEAD_DIM = 64
GROUP = N_HEADS // N_KV_HEADS
ATT_WIDTH = N_HEADS * HEAD_DIM
KV_WIDTH = N_KV_HEADS * HEAD_DIM
N_IDX_HEADS = 8
IDX_DIM = 64
TOPK_MAX = 256
Q_BLOCK = 128
RMS_EPS = 1e-6
LN_EPS = 1e-5
SPLIT_SIZES = (SGU_WIDTH, SGU_WIDTH, SGU_WIDTH,
               ATT_WIDTH, KV_WIDTH, KV_WIDTH, ATT_WIDTH,
               N_IDX_HEADS * IDX_DIM, IDX_DIM, N_IDX_HEADS,
               D_MODEL, D_MODEL)
IN_WIDTH = (3 * SGU_WIDTH + 2 * ATT_WIDTH + 2 * KV_WIDTH
            + N_IDX_HEADS * IDX_DIM + IDX_DIM + N_IDX_HEADS + 2 * D_MODEL)

kernel_name = "chunk_causal_gmlp_dsa_hybrid_step"


def _rmsnorm(x, g):
    xf = x.astype(jnp.float32)
    y = xf * lax.rsqrt(jnp.mean(xf * xf, axis=-1, keepdims=True) + RMS_EPS)
    return (y * g.astype(jnp.float32)).astype(x.dtype)


def _chunk_mask(q_pos, k_pos):
    return (k_pos[None, :] // CHUNK) <= (q_pos[:, None] // CHUNK)


def _project(x, norm_g, w_in):
    h = _rmsnorm(x, norm_g)
    points = np.cumsum(SPLIT_SIZES)[:-1].tolist()
    return jnp.split(h @ w_in, points, axis=-1)


def _sgu(u_pre, v_pre, ln_g, ln_b, w_s, b_s):
    b, L, _ = u_pre.shape
    n = min(L, SGU_LEN)
    u = jax.nn.gelu(u_pre)
    v = jax.nn.gelu(v_pre)
    vf = v.astype(jnp.float32)
    mu = jnp.mean(vf, axis=-1, keepdims=True)
    var = jnp.mean(jnp.square(vf - mu), axis=-1, keepdims=True)
    vn = ((vf - mu) * lax.rsqrt(var + LN_EPS) * ln_g.astype(jnp.float32)
          + ln_b.astype(jnp.float32)).astype(v.dtype)
    pos = jnp.arange(n)
    ws = jnp.where(_chunk_mask(pos, pos)[None], w_s[:, :n, :n], 0)
    vc = vn.reshape(b, L // n, n, SGU_GROUPS, SGU_GDIM)
    mixed = jnp.einsum('gij,bcjgd->bcigd', ws, vc) + b_s[:, :n].T[None, None, :, :, None]
    return u * mixed.reshape(b, L, SGU_WIDTH), vn


def _sparse_attend(q, qi, wi, q_pos, k, v, kidx, k_pos, topk):
    b, nq = q.shape[0], q.shape[1]
    rel = jax.nn.relu(jnp.einsum('bqhd,bld->bqhl', qi, kidx) * (IDX_DIM ** -0.5))
    score = jnp.einsum('bqhl,bqh->bql', rel, wi).astype(jnp.float32) * (N_IDX_HEADS ** -0.5)
    score = jnp.where(_chunk_mask(q_pos, k_pos)[None], score, -jnp.inf)
    top_val, top_idx = lax.top_k(score, topk)
    valid = top_val > -jnp.inf
    gather = jax.vmap(lambda rows, idx: rows[idx])
    k_sel = gather(k, top_idx)
    v_sel = gather(v, top_idx)
    qg = q.reshape(b, nq, N_KV_HEADS, GROUP, HEAD_DIM)
    logits = jnp.einsum('bqhgd,bqnhd->bqhgn', qg, k_sel).astype(jnp.float32) * (HEAD_DIM ** -0.5)
    logits = jnp.where(valid[:, :, None, None, :], logits, -jnp.inf)
    p = jax.nn.softmax(logits, axis=-1).astype(v.dtype)
    out = jnp.einsum('bqhgn,bqnhd->bqhgd', p, v_sel)
    return out.reshape(b, nq, ATT_WIDTH)


def _prompt_attention(q, qi, wi, k, v, kidx):
    bsz, seq = q.shape[0], q.shape[1]
    topk = min(TOPK_MAX, seq // 4)
    k_pos = jnp.arange(seq)

    def block(i):
        start = i * Q_BLOCK
        sl = lambda a: lax.dynamic_slice_in_dim(a, start, Q_BLOCK, axis=1)
        return _sparse_attend(sl(q), sl(qi), sl(wi), start + jnp.arange(Q_BLOCK),
                              k, v, kidx, k_pos, topk)

    out = lax.map(block, jnp.arange(seq // Q_BLOCK))
    return jnp.swapaxes(out, 0, 1).reshape(bsz, seq, ATT_WIDTH)


def _merge(x, a_out, b_out, g_a, g_b, w_pa, w_pb, w_out):
    m = jax.nn.sigmoid(g_a) * (a_out @ w_pa) + jax.nn.sigmoid(g_b) * (b_out @ w_pb)
    return x + m @ w_out


def _heads(q, k, v, qi, ki, wi):
    b, L = q.shape[0], q.shape[1]
    return (q.reshape(b, L, N_HEADS, HEAD_DIM), k.reshape(b, L, N_KV_HEADS, HEAD_DIM),
            v.reshape(b, L, N_KV_HEADS, HEAD_DIM), qi.reshape(b, L, N_IDX_HEADS, IDX_DIM), ki, wi)


def setup_inputs(seed: int = 0) -> dict:
    key = jax.random.key(seed)
    ks = jax.random.split(key, 18)
    f32 = jnp.float32
    nrm = lambda k, s: jax.random.normal(k, s, f32)
    return {
        "x_prompt": nrm(ks[0], (BATCH, SEQ, D_MODEL)),
        "x_sample": nrm(ks[1], (DEC_BATCH, DEC_SEQ, D_MODEL)),
        "cache_k": nrm(ks[2], (DEPTH, DEC_BATCH, PAST_LEN, N_KV_HEADS, HEAD_DIM)),
        "cache_v": nrm(ks[3], (DEPTH, DEC_BATCH, PAST_LEN, N_KV_HEADS, HEAD_DIM)),
        "cache_kidx": nrm(ks[4], (DEPTH, DEC_BATCH, PAST_LEN, IDX_DIM)),
        "norm_g": 1.0 + 0.02 * nrm(ks[5], (DEPTH, D_MODEL)),
        "w_in": nrm(ks[6], (DEPTH, D_MODEL, IN_WIDTH)) * D_MODEL ** -0.5,
        "sgu_ln_g": 1.0 + 0.02 * nrm(ks[7], (DEPTH, SGU_WIDTH)),
        "sgu_ln_b": 0.02 * nrm(ks[8], (DEPTH, SGU_WIDTH)),
        "sgu_w": nrm(ks[9], (DEPTH, SGU_GROUPS, SGU_LEN, SGU_LEN)) * SGU_LEN ** -0.5,
        "sgu_b": 1.0 + 0.02 * nrm(ks[10], (DEPTH, SGU_GROUPS, SGU_LEN)),
        "w_pa": nrm(ks[11], (DEPTH, SGU_WIDTH, D_MODEL)) * SGU_WIDTH ** -0.5,
        "w_pb": nrm(ks[12], (DEPTH, ATT_WIDTH, D_MODEL)) * ATT_WIDTH ** -0.5,
        "w_out": nrm(ks[13], (DEPTH, D_MODEL, D_MODEL)) * D_MODEL ** -0.5,
        "final_g": 1.0 + 0.02 * nrm(ks[14], (D_MODEL,)),
    }


def reference(x_prompt, x_sample, cache_k, cache_v, cache_kidx, norm_g, w_in, sgu_ln_g, sgu_ln_b,
              sgu_w, sgu_b, w_pa, w_pb, w_out, final_g):
    xp, xs = x_prompt, x_sample
    pk, pv, pki, sk, sv, ski, svn = [], [], [], [], [], [], []
    topk_s = min(TOPK_MAX, (PAST_LEN + DEC_SEQ) // 4)
    k_pos_s = jnp.arange(PAST_LEN + DEC_SEQ)
    q_pos_s = PAST_LEN + jnp.arange(DEC_SEQ)
    for l in range(DEPTH):
        u, v, za, q, k, vv, zb, qi, ki, wi, ga, gb = _project(xp, norm_g[l], w_in[l])
        a_out, _ = _sgu(u, v, sgu_ln_g[l], sgu_ln_b[l], sgu_w[l], sgu_b[l])
        a_out = a_out * jax.nn.silu(za)
        q4, k4, v4, qi4, ki4, wi4 = _heads(q, k, vv, qi, ki, wi)
        b_out = _prompt_attention(q4, qi4, wi4, k4, v4, ki4) * jax.nn.silu(zb)
        xp = _merge(xp, a_out, b_out, ga, gb, w_pa[l], w_pb[l], w_out[l])
        pk.append(k4); pv.append(v4); pki.append(ki4)
        u, v, za, q, k, vv, zb, qi, ki, wi, ga, gb = _project(xs, norm_g[l], w_in[l])
        a_out, vn_s = _sgu(u, v, sgu_ln_g[l], sgu_ln_b[l], sgu_w[l], sgu_b[l])
        a_out = a_out * jax.nn.silu(za)
        q4, k4, v4, qi4, ki4, wi4 = _heads(q, k, vv, qi, ki, wi)
        k_all = jnp.concatenate([cache_k[l], k4], axis=1)
        v_all = jnp.concatenate([cache_v[l], v4], axis=1)
        ki_all = jnp.concatenate([cache_kidx[l], ki4], axis=1)
        b_out = _sparse_attend(q4, qi4, wi4, q_pos_s, k_all, v_all, ki_all, k_pos_s, topk_s)
        b_out = b_out * jax.nn.silu(zb)
        xs = _merge(xs, a_out, b_out, ga, gb, w_pa[l], w_pb[l], w_out[l])
        sk.append(k4); sv.append(v4); ski.append(ki4); svn.append(vn_s)
    y_prompt = _rmsnorm(xp, final_g)
    y_sample = _rmsnorm(xs, final_g)
    prompt_k = jnp.stack(pk)
    prompt_v = jnp.stack(pv)
    prompt_kidx = jnp.stack(pki)
    sample_k = jnp.stack(sk)
    sample_v = jnp.stack(sv)
    sample_kidx = jnp.stack(ski)
    sample_sgu_v = jnp.stack(svn)
    return (y_prompt, y_sample, prompt_k, prompt_v, prompt_kidx, sample_k, sample_v, sample_kidx, sample_sgu_v)
```

```python
import functools

import jax
import jax.numpy as jnp
from jax import lax
from jax.experimental import pallas as pl
from jax.experimental.pallas import tpu as pltpu

D_MODEL = 1024
CHUNK = 64
SGU_LEN = 128
SGU_GROUPS = 8
SGU_WIDTH = 512
SGU_GDIM = SGU_WIDTH // SGU_GROUPS
N_HEADS = 8
N_KV_HEADS = 4
HEAD_DIM = 64
GROUP = N_HEADS // N_KV_HEADS
ATT_WIDTH = N_HEADS * HEAD_DIM
KV_WIDTH = N_KV_HEADS * HEAD_DIM
N_IDX_HEADS = 8
IDX_DIM = 64
TOPK_MAX = 256
RMS_EPS = 1e-6
LN_EPS = 1e-5

BF = jnp.bfloat16
F32 = jnp.float32
I32 = jnp.int32

LANES = 128
IDX_PAD = 640
VMEM_LIMIT = 52 * 1024 * 1024
MASKED_LOGIT = -1e30
INT_MIN = -2147483648
NEG_INF_KEY = -2139095041


def _const_spec(shape):
    nd = len(shape)
    return pl.BlockSpec(shape, lambda *_: (0,) * nd, pipeline_mode=pl.Buffered(1))


def _proj_kernel(x_ref, ng_ref, wuvz_ref, watt_ref, widx_ref, wg_ref, wkt_ref, wkit_ref,
                 lng_ref, lnb_ref, wmix_ref, bmix_ref, wpa_ref,
                 ma_ref, gb_ref, zb_ref, q_ref, qi_ref, wi_ref, k_ref, v_ref, ki_ref,
                 kt_ref, vhm_ref, kit_ref, *vn_refs, n_sub):
    x = x_ref[0]
    ms = jnp.mean(x * x, axis=-1, keepdims=True)
    h = (x * lax.rsqrt(ms + RMS_EPS) * ng_ref[...]).astype(BF)

    uvz = jnp.dot(h, wuvz_ref[...], preferred_element_type=F32)
    u = jax.nn.gelu(uvz[:, :SGU_WIDTH])
    v = jax.nn.gelu(uvz[:, SGU_WIDTH:2 * SGU_WIDTH])
    za = uvz[:, 2 * SGU_WIDTH:]
    mu = jnp.mean(v, axis=-1, keepdims=True)
    var = jnp.mean(jnp.square(v - mu), axis=-1, keepdims=True)
    vn = (v - mu) * lax.rsqrt(var + LN_EPS) * lng_ref[...] + lnb_ref[...]
    if vn_refs:
        vn_refs[0][0] = vn
    lane = lax.broadcasted_iota(I32, (SGU_LEN, LANES), 1)
    cols = []
    for c in range(SGU_WIDTH // LANES):
        rows = []
        for s in range(n_sub):
            blk = vn[s * SGU_LEN:(s + 1) * SGU_LEN, c * LANES:(c + 1) * LANES]
            lo = jnp.where(lane < SGU_GDIM, blk, 0.0).astype(BF)
            hi = jnp.where(lane >= SGU_GDIM, blk, 0.0).astype(BF)
            rhs = jnp.concatenate([lo, hi], axis=0)
            rows.append(jnp.dot(wmix_ref[c], rhs, preferred_element_type=F32))
        cols.append(jnp.concatenate(rows, axis=0) if n_sub > 1 else rows[0])
    mixed = jnp.concatenate(cols, axis=1)
    bias = bmix_ref[...]
    if n_sub > 1:
        bias = jnp.concatenate([bias] * n_sub, axis=0)
    a_out = u * (mixed + bias) * jax.nn.silu(za)
    pa = jnp.dot(a_out.astype(BF), wpa_ref[...], preferred_element_type=F32)
    g = jnp.dot(h, wg_ref[...], preferred_element_type=F32)
    ma_ref[0] = (jax.nn.sigmoid(g[:, :D_MODEL]) * pa).astype(BF)
    gb_ref[0] = g[:, D_MODEL:]

    att = jnp.dot(h, watt_ref[...], preferred_element_type=F32)
    q = att[:, :ATT_WIDTH] * (HEAD_DIM ** -0.5)
    k = att[:, ATT_WIDTH:ATT_WIDTH + KV_WIDTH]
    vv = att[:, ATT_WIDTH + KV_WIDTH:ATT_WIDTH + 2 * KV_WIDTH]
    k_ref[0] = k
    v_ref[0] = vv
    zb_ref[0] = att[:, ATT_WIDTH + 2 * KV_WIDTH:]
    for hh in range(N_HEADS):
        q_ref[0, hh] = q[:, hh * HEAD_DIM:(hh + 1) * HEAD_DIM].astype(BF)
    for kv in range(N_KV_HEADS):
        vhm_ref[0, kv] = vv[:, kv * HEAD_DIM:(kv + 1) * HEAD_DIM].astype(BF)

    idx = jnp.dot(h, widx_ref[...], preferred_element_type=F32)
    qi = idx[:, :N_IDX_HEADS * IDX_DIM] * (IDX_DIM ** -0.5)
    for hh in range(N_IDX_HEADS):
        qi_ref[0, hh] = qi[:, hh * IDX_DIM:(hh + 1) * IDX_DIM].astype(BF)
    ki_ref[0] = idx[:, N_IDX_HEADS * IDX_DIM:N_IDX_HEADS * IDX_DIM + IDX_DIM]
    w0 = N_IDX_HEADS * IDX_DIM + IDX_DIM
    wi_ref[0] = idx[:, w0:w0 + N_IDX_HEADS] * (N_IDX_HEADS ** -0.5)

    nt = (((1,), (1,)), ((), ()))
    kt_ref[0, 0] = lax.dot_general(wkt_ref[...], h, nt, preferred_element_type=F32).astype(BF)
    kit_ref[0, 0] = lax.dot_general(wkit_ref[...], h, nt, preferred_element_type=F32).astype(BF)


def _proj_call(x, wts, wmix, bmix, *, tt, lt, emit_vn):
    b, s, _ = x.shape
    n_sub = tt // SGU_LEN
    r = lt // tt
    grid = (b, s // tt)
    tok = lambda w: pl.BlockSpec((1, tt, w), lambda i, j: (i, j, 0))
    hm = lambda n: pl.BlockSpec((1, n, tt, HEAD_DIM), lambda i, j: (i, 0, j, 0))
    tr = lambda n: pl.BlockSpec((1, 1, n, tt), lambda i, j: (i, j // r, 0, j % r))
    out_shape = [
        jax.ShapeDtypeStruct((b, s, D_MODEL), BF),
        jax.ShapeDtypeStruct((b, s, D_MODEL), F32),
        jax.ShapeDtypeStruct((b, s, ATT_WIDTH), F32),
        jax.ShapeDtypeStruct((b, N_HEADS, s, HEAD_DIM), BF),
        jax.ShapeDtypeStruct((b, N_IDX_HEADS, s, IDX_DIM), BF),
        jax.ShapeDtypeStruct((b, s, N_IDX_HEADS), F32),
        jax.ShapeDtypeStruct((b, s, KV_WIDTH), F32),
        jax.ShapeDtypeStruct((b, s, KV_WIDTH), F32),
        jax.ShapeDtypeStruct((b, s, IDX_DIM), F32),
        jax.ShapeDtypeStruct((b, s // lt, KV_WIDTH, lt), BF),
        jax.ShapeDtypeStruct((b, N_KV_HEADS, s, HEAD_DIM), BF),
        jax.ShapeDtypeStruct((b, s // lt, IDX_DIM, lt), BF),
    ]
    out_specs = [tok(D_MODEL), tok(D_MODEL), tok(ATT_WIDTH), hm(N_HEADS), hm(N_IDX_HEADS),
                 tok(N_IDX_HEADS), tok(KV_WIDTH), tok(KV_WIDTH), tok(IDX_DIM),
                 tr(KV_WIDTH), hm(N_KV_HEADS), tr(IDX_DIM)]
    if emit_vn:
        out_shape.append(jax.ShapeDtypeStruct((b, s, SGU_WIDTH), F32))
        out_specs.append(tok(SGU_WIDTH))
    consts = [wts["norm_g"], wts["w_uvz"], wts["w_att"], wts["w_idx"], wts["w_g"],
              wts["w_kt"], wts["w_kit"], wts["ln_g"], wts["ln_b"], wmix, bmix, wts["w_pa"]]
    return pl.pallas_call(
        functools.partial(_proj_kernel, n_sub=n_sub),
        grid=grid,
        in_specs=[tok(D_MODEL)] + [_const_spec(c.shape) for c in consts],
        out_specs=out_specs,
        out_shape=out_shape,
        compiler_params=pltpu.CompilerParams(
            dimension_semantics=("arbitrary", "arbitrary"), vmem_limit_bytes=VMEM_LIMIT),
        name="proj_vn" if emit_vn else "proj",
    )(x, *consts)


def _attn_kernel(q_ref, qi_ref, wi_ref, zb_ref, kt_ref, v_ref, kit_ref, tri_ref,
                 o_ref, key_ref, bo_ref, *, tq, lt, n_tiles, topk, n_valid, causal):
    q0 = pl.program_id(1) * tq
    row = lax.broadcasted_iota(I32, (tq, 1), 0)
    if causal:
        n_t = lax.div(q0 + tq + lt - 1, lt)
        limit = (lax.shift_right_logical(q0 + row, 6) + 1) * CHUNK
    else:
        n_t = n_tiles
        limit = jnp.full((tq, 1), n_valid, I32)

    qi = qi_ref[0].reshape(N_IDX_HEADS * tq, IDX_DIM)
    wi = wi_ref[0]
    wcols = [wi[:, hh:hh + 1] for hh in range(N_IDX_HEADS)]
    col0 = lax.broadcasted_iota(I32, (1, lt), 1)

    def score_tile(t, carry):
        s = jnp.dot(qi, kit_ref[0, t], preferred_element_type=F32)
        sc = jnp.zeros((tq, lt), F32)
        for hh in range(N_IDX_HEADS):
            sc = sc + jnp.maximum(s[hh * tq:(hh + 1) * tq], 0.0) * wcols[hh]
        sc = jnp.where(col0 + t * lt < limit, sc + 0.0, -jnp.inf)
        bits = lax.bitcast_convert_type(sc, I32)
        key_ref[t] = bits ^ (lax.shift_right_arithmetic(bits, 31) & 0x7FFFFFFF)
        return carry

    lax.fori_loop(0, n_t, score_tile, 0)

    def count_ge(cand):
        def body(t, acc):
            m = jnp.where(key_ref[t] >= cand, 1, 0).astype(I32)
            for c in range(lt // LANES):
                acc = acc + m[:, c * LANES:(c + 1) * LANES]
            return acc
        acc = lax.fori_loop(0, n_t, body, jnp.zeros((tq, LANES), I32))
        return jnp.sum(acc, axis=-1, keepdims=True)

    def bit_pass(i, carry):
        thr_u, cnt_thr = carry
        cand_u = thr_u | lax.shift_left(jnp.int32(1), 31 - i)
        cnt = count_ge(cand_u ^ INT_MIN)
        take = cnt >= topk
        return jnp.where(take, cand_u, thr_u), jnp.where(take, cnt, cnt_thr)

    thr_u, cnt_thr = lax.fori_loop(
        0, 32, bit_pass, (jnp.zeros((tq, 1), I32), jnp.full((tq, 1), n_t * lt, I32)))
    thr = thr_u ^ INT_MIN
    n_gt = count_ge(thr + 1)
    n_gt = jnp.where(thr == 2147483647, 0, n_gt)
    need = jnp.where(thr == NEG_INF_KEY, 0, topk - n_gt).astype(F32)

    qs = [q_ref[0, GROUP * kv:GROUP * (kv + 1)].reshape(GROUP * tq, HEAD_DIM)
          for kv in range(N_KV_HEADS)]
    tri = tri_ref[...]

    def attend_tile(t, carry):
        n_eq, ms, ls, accs = carry
        key = key_ref[t]
        eq = key == thr
        pref = jnp.dot(jnp.where(eq, 1.0, 0.0).astype(BF), tri, preferred_element_type=F32)
        sel = (key > thr) | (eq & (n_eq + pref - 1.0 < need))
        n_eq = n_eq + pref[:, lt - 1:lt]
        sel2 = jnp.concatenate([sel] * GROUP, axis=0)
        start = pl.multiple_of(t * lt, lt)
        new_m, new_l, new_acc = [], [], []
        for kv in range(N_KV_HEADS):
            logits = jnp.dot(qs[kv], kt_ref[0, t, kv * HEAD_DIM:(kv + 1) * HEAD_DIM, :],
                             preferred_element_type=F32)
            m_old = ms[kv]
            m_new = jnp.maximum(
                m_old, jnp.max(jnp.where(sel2, logits, MASKED_LOGIT), axis=-1, keepdims=True))
            p = jnp.where(sel2, jnp.exp(logits - m_new), 0.0)
            alpha = jnp.exp(m_old - m_new)
            new_m.append(m_new)
            new_l.append(alpha * ls[kv] + jnp.sum(p, axis=-1, keepdims=True))
            pv = jnp.dot(p.astype(BF), v_ref[0, kv, pl.ds(start, lt), :],
                         preferred_element_type=F32)
            new_acc.append(alpha * accs[kv] + pv)
        return n_eq, tuple(new_m), tuple(new_l), tuple(new_acc)

    init = (jnp.zeros((tq, 1), F32),
            tuple(jnp.full((GROUP * tq, 1), MASKED_LOGIT, F32) for _ in range(N_KV_HEADS)),
            tuple(jnp.zeros((GROUP * tq, 1), F32) for _ in range(N_KV_HEADS)),
            tuple(jnp.zeros((GROUP * tq, HEAD_DIM), F32) for _ in range(N_KV_HEADS)))
    _, _, ls, accs = lax.fori_loop(0, n_t, attend_tile, init)

    for kv in range(N_KV_HEADS):
        o = accs[kv] / ls[kv]
        for gi in range(GROUP):
            hh = GROUP * kv + gi
            bo_ref[:, hh * HEAD_DIM:(hh + 1) * HEAD_DIM] = o[gi * tq:(gi + 1) * tq]
    o_ref[0] = (bo_ref[...] * jax.nn.silu(zb_ref[0])).astype(BF)


def _attn_call(q_hm, qi_hm, wi, zb, kt, v_hm, kit, *, tq, topk, n_valid, causal):
    b, _, s, _ = q_hm.shape
    n_tiles, lt = kt.shape[1], kt.shape[3]
    l_all = v_hm.shape[2]
    tri = (lax.broadcasted_iota(I32, (lt, lt), 0) <= lax.broadcasted_iota(I32, (lt, lt), 1)).astype(BF)
    whole = lambda shape: pl.BlockSpec((1,) + shape, lambda i, j: (i, 0, 0, 0),
                                       pipeline_mode=pl.Buffered(1))
    return pl.pallas_call(
        functools.partial(_attn_kernel, tq=tq, lt=lt, n_tiles=n_tiles, topk=topk,
                          n_valid=n_valid, causal=causal),
        grid=(b, s // tq),
        in_specs=[
            pl.BlockSpec((1, N_HEADS, tq, HEAD_DIM), lambda i, j: (i, 0, j, 0)),
            pl.BlockSpec((1, N_IDX_HEADS, tq, IDX_DIM), lambda i, j: (i, 0, j, 0)),
            pl.BlockSpec((1, tq, N_IDX_HEADS), lambda i, j: (i, j, 0)),
            pl.BlockSpec((1, tq, ATT_WIDTH), lambda i, j: (i, j, 0)),
            whole((n_tiles, KV_WIDTH, lt)),
            whole((N_KV_HEADS, l_all, HEAD_DIM)),
            whole((n_tiles, IDX_DIM, lt)),
            _const_spec((lt, lt)),
        ],
        out_specs=pl.BlockSpec((1, tq, ATT_WIDTH), lambda i, j: (i, j, 0)),
        out_shape=jax.ShapeDtypeStruct((b, s, ATT_WIDTH), BF),
        scratch_shapes=[pltpu.VMEM((n_tiles, tq, lt), I32), pltpu.VMEM((tq, ATT_WIDTH), F32)],
        compiler_params=pltpu.CompilerParams(
            dimension_semantics=("arbitrary", "arbitrary"), vmem_limit_bytes=VMEM_LIMIT),
        name="attn_prompt" if causal else "attn_sample",
    )(q_hm, qi_hm, wi, zb, kt, v_hm, kit, tri)


def _merge_kernel(x_ref, ma_ref, gb_ref, bg_ref, wpb_ref, wout_ref, fg_ref, y_ref):
    pb = jnp.dot(bg_ref[0], wpb_ref[...], preferred_element_type=F32)
    m = ma_ref[0].astype(F32) + jax.nn.sigmoid(gb_ref[0]) * pb
    xo = x_ref[0] + jnp.dot(m.astype(BF), wout_ref[...], preferred_element_type=F32)
    ms = jnp.mean(xo * xo, axis=-1, keepdims=True)
    y_ref[0] = xo * lax.rsqrt(ms + RMS_EPS) * fg_ref[...]


def _merge_call(x, ma, gb, bg, wts, *, tt):
    b, s, _ = x.shape
    tok = lambda w: pl.BlockSpec((1, tt, w), lambda i, j: (i, j, 0))
    consts = [wts["w_pb"], wts["w_out"], wts["final_g"]]
    return pl.pallas_call(
        _merge_kernel,
        grid=(b, s // tt),
        in_specs=[tok(D_MODEL), tok(D_MODEL), tok(D_MODEL), tok(ATT_WIDTH)]
                 + [_const_spec(c.shape) for c in consts],
        out_specs=tok(D_MODEL),
        out_shape=jax.ShapeDtypeStruct((b, s, D_MODEL), F32),
        compiler_params=pltpu.CompilerParams(
            dimension_semantics=("arbitrary", "arbitrary"), vmem_limit_bytes=VMEM_LIMIT),
        name="merge",
    )(x, ma, gb, bg, *consts)


def _prep_weights(norm_g, w_in, ln_g, ln_b, w_pa, w_pb, w_out, final_g):
    o = 0
    cols = {}
    for name, width in (("u", SGU_WIDTH), ("v", SGU_WIDTH), ("za", SGU_WIDTH), ("q", ATT_WIDTH),
                        ("k", KV_WIDTH), ("vv", KV_WIDTH), ("zb", ATT_WIDTH),
                        ("qi", N_IDX_HEADS * IDX_DIM), ("ki", IDX_DIM), ("wi", N_IDX_HEADS),
                        ("ga", D_MODEL), ("gb", D_MODEL)):
        cols[name] = (o, o + width)
        o += width
    wb = w_in.astype(BF)
    sl = lambda a, z: wb[:, cols[a][0]:cols[z][1]]
    w_idx = sl("qi", "wi")
    w_idx = jnp.pad(w_idx, ((0, 0), (0, IDX_PAD - w_idx.shape[1])))
    return {
        "norm_g": norm_g.reshape(1, D_MODEL), "w_uvz": sl("u", "za"), "w_att": sl("q", "zb"),
        "w_idx": w_idx, "w_g": sl("ga", "gb"), "w_kt": sl("k", "k").T, "w_kit": sl("ki", "ki").T,
        "ln_g": ln_g.reshape(1, SGU_WIDTH), "ln_b": ln_b.reshape(1, SGU_WIDTH),
        "w_pa": w_pa.astype(BF), "w_pb": w_pb.astype(BF), "w_out": w_out.astype(BF),
        "final_g": final_g.reshape(1, D_MODEL),
    }


def _mix_operands(w_mix, b_pos):
    pairs = w_mix.reshape(SGU_GROUPS // 2, 2, SGU_LEN, SGU_LEN)
    wmix = jnp.concatenate([pairs[:, 0], pairs[:, 1]], axis=-1).astype(BF)
    bmix = jnp.repeat(b_pos, SGU_GDIM, axis=1)
    return wmix, bmix


def kernel(x_prompt, x_sample, cache_k, cache_v, cache_kidx, norm_g, w_in, sgu_ln_g, sgu_ln_b,
           sgu_w, sgu_b, w_pa, w_pb, w_out, final_g):
    b, s, _ = x_prompt.shape
    db, ds, _ = x_sample.shape
    past = cache_k.shape[2]
    depth = norm_g.shape[0]
    assert depth == 1 and db * ds == SGU_LEN and s % 512 == 0
    wts = _prep_weights(norm_g[0], w_in[0], sgu_ln_g[0], sgu_ln_b[0], w_pa[0], w_pb[0], w_out[0],
                        final_g)

    pos = jnp.arange(SGU_LEN)
    chunk_ok = (pos[None, :] // CHUNK) <= (pos[:, None] // CHUNK)
    wmix_p, bmix_p = _mix_operands(jnp.where(chunk_ok[None], sgu_w[0], 0.0), sgu_b[0].T)
    (ma, gb, zb, q_hm, qi_hm, wi, k, v, ki, kt, v_hm, kit) = _proj_call(
        x_prompt, wts, wmix_p, bmix_p, tt=256, lt=512, emit_vn=False)
    bg = _attn_call(q_hm, qi_hm, wi, zb, kt, v_hm, kit, tq=128, topk=min(TOPK_MAX, s // 4),
                    n_valid=s, causal=True)
    y_prompt = _merge_call(x_prompt, ma, gb, bg, wts, tt=512)

    assert past % CHUNK == 0 and ds <= CHUNK
    w_blk = jnp.einsum("ab,gij->gaibj", jnp.eye(db, dtype=F32), sgu_w[0][:, :ds, :ds])
    wmix_s, bmix_s = _mix_operands(w_blk.reshape(SGU_GROUPS, SGU_LEN, SGU_LEN),
                                   jnp.tile(sgu_b[0][:, :ds].T, (db, 1)))
    n_tok = db * ds
    (ma_s, gb_s, zb_s, q_s, qi_s, wi_s, k_s, v_s, ki_s, _, _, _, vn_s) = _proj_call(
        x_sample.reshape(1, n_tok, D_MODEL), wts, wmix_s, bmix_s, tt=n_tok, lt=n_tok, emit_vn=True)
    n_keys = past + ds
    l_pad = -(-n_keys // LANES) * LANES
    per_stream = lambda a, n: a.reshape(n, db, ds, HEAD_DIM).transpose(1, 0, 2, 3)
    k_all = jnp.concatenate([cache_k[0].reshape(db, past, KV_WIDTH), k_s.reshape(db, ds, KV_WIDTH)], 1)
    v_all = jnp.concatenate([cache_v[0].reshape(db, past, KV_WIDTH), v_s.reshape(db, ds, KV_WIDTH)], 1)
    ki_all = jnp.concatenate([cache_kidx[0], ki_s.reshape(db, ds, IDX_DIM)], 1)
    padk = lambda a: jnp.pad(a, ((0, 0), (0, l_pad - n_keys), (0, 0))).astype(BF)
    kt_s = padk(k_all).transpose(0, 2, 1)[:, None]
    kit_s = padk(ki_all).transpose(0, 2, 1)[:, None]
    v_hm_s = padk(v_all).reshape(db, l_pad, N_KV_HEADS, HEAD_DIM).transpose(0, 2, 1, 3)
    bg_s = _attn_call(per_stream(q_s, N_HEADS), per_stream(qi_s, N_IDX_HEADS),
                      wi_s.reshape(db, ds, N_IDX_HEADS), zb_s.reshape(db, ds, ATT_WIDTH),
                      kt_s, v_hm_s, kit_s, tq=ds, topk=min(TOPK_MAX, n_keys // 4),
                      n_valid=n_keys, causal=False)
    y_sample = _merge_call(x_sample.reshape(1, n_tok, D_MODEL), ma_s, gb_s,
                           bg_s.reshape(1, n_tok, ATT_WIDTH), wts, tt=n_tok)

    return (y_prompt, y_sample.reshape(db, ds, D_MODEL),
            k.reshape(1, b, s, N_KV_HEADS, HEAD_DIM), v.reshape(1, b, s, N_KV_HEADS, HEAD_DIM),
            ki.reshape(1, b, s, IDX_DIM),
            k_s.reshape(1, db, ds, N_KV_HEADS, HEAD_DIM), v_s.reshape(1, db, ds, N_KV_HEADS, HEAD_DIM),
            ki_s.reshape(1, db, ds, IDX_DIM), vn_s.reshape(1, db, ds, SGU_WIDTH))
```

```python
import functools

import jax
import jax.numpy as jnp
from jax import lax
from jax.experimental import pallas as pl
from jax.experimental.pallas import tpu as pltpu

D_MODEL = 1024
CHUNK = 64
SGU_LEN = 128
SGU_GROUPS = 8
SGU_WIDTH = 512
SGU_GDIM = SGU_WIDTH // SGU_GROUPS
N_HEADS = 8
N_KV_HEADS = 4
HEAD_DIM = 64
GROUP = N_HEADS // N_KV_HEADS
ATT_WIDTH = N_HEADS * HEAD_DIM
KV_WIDTH = N_KV_HEADS * HEAD_DIM
N_IDX_HEADS = 8
IDX_DIM = 64
TOPK_MAX = 256
RMS_EPS = 1e-6
LN_EPS = 1e-5

BF = jnp.bfloat16
F32 = jnp.float32
I32 = jnp.int32

LANES = 128
SUBLANES = 8
BF16_ROWS = 16
TQ = LANES
LT = 512
V_ROWS = HEAD_DIM + BF16_ROWS
VMEM_LIMIT = 52 * 1024 * 1024
MASKED_LOGIT = -1e30
TINY = 1.1754944e-38
ALL_TIES = 3e38
MAX_SEARCH_PASSES = 20
SEARCH_CLIP = 0.05
LOG2_E = 1.4426950408889634
INT_MIN = -2147483648


def _const_spec(shape):
    nd = len(shape)
    return pl.BlockSpec(shape, lambda *_: (0,) * nd, pipeline_mode=pl.Buffered(1))


def _proj_kernel(x_ref, ng_ref, wuvz_ref, wnat_ref, wg_ref, wt_ref,
                 lng_ref, lnb_ref, wmix_ref, bmix_ref, wpa_ref,
                 ma_ref, gb_ref, zb_ref, k_ref, v_ref, ki_ref, khm_ref, kib_ref,
                 qt_ref, qit_ref, wit_ref, vt_ref, *vn_refs, n_sub):
    x = x_ref[0]
    ms = jnp.mean(x * x, axis=-1, keepdims=True)
    h = (x * lax.rsqrt(ms + RMS_EPS) * ng_ref[...]).astype(BF)

    uvz = jnp.dot(h, wuvz_ref[...], preferred_element_type=F32)
    u = jax.nn.gelu(uvz[:, :SGU_WIDTH])
    v = jax.nn.gelu(uvz[:, SGU_WIDTH:2 * SGU_WIDTH])
    za = uvz[:, 2 * SGU_WIDTH:]
    mu = jnp.mean(v, axis=-1, keepdims=True)
    var = jnp.mean(jnp.square(v - mu), axis=-1, keepdims=True)
    vn = (v - mu) * lax.rsqrt(var + LN_EPS) * lng_ref[...] + lnb_ref[...]
    if vn_refs:
        vn_refs[0][0] = vn
    lane = lax.broadcasted_iota(I32, (SGU_LEN, LANES), 1)
    cols = []
    for c in range(SGU_WIDTH // LANES):
        rows = []
        for s in range(n_sub):
            blk = vn[s * SGU_LEN:(s + 1) * SGU_LEN, c * LANES:(c + 1) * LANES]
            lo = jnp.where(lane < SGU_GDIM, blk, 0.0).astype(BF)
            hi = jnp.where(lane >= SGU_GDIM, blk, 0.0).astype(BF)
            rhs = jnp.concatenate([lo, hi], axis=0)
            rows.append(jnp.dot(wmix_ref[c], rhs, preferred_element_type=F32))
        cols.append(jnp.concatenate(rows, axis=0) if n_sub > 1 else rows[0])
    mixed = jnp.concatenate(cols, axis=1)
    bias = bmix_ref[...]
    if n_sub > 1:
        bias = jnp.concatenate([bias] * n_sub, axis=0)
    a_out = u * (mixed + bias) * jax.nn.silu(za)
    pa = jnp.dot(a_out.astype(BF), wpa_ref[...], preferred_element_type=F32)
    g = jnp.dot(h, wg_ref[...], preferred_element_type=F32)
    ma_ref[0] = (jax.nn.sigmoid(g[:, :D_MODEL]) * pa).astype(BF)
    gb_ref[0] = g[:, D_MODEL:]

    nat = jnp.dot(h, wnat_ref[...], preferred_element_type=F32)
    k = nat[:, :KV_WIDTH]
    k_ref[0] = k
    v_ref[0] = nat[:, KV_WIDTH:2 * KV_WIDTH]
    zb_ref[0] = nat[:, 2 * KV_WIDTH:2 * KV_WIDTH + ATT_WIDTH]
    ki = nat[:, 2 * KV_WIDTH + ATT_WIDTH:]
    ki_ref[0] = ki
    kib_ref[0] = ki.astype(BF)
    for kv in range(N_KV_HEADS):
        khm_ref[0, kv] = k[:, kv * HEAD_DIM:(kv + 1) * HEAD_DIM].astype(BF)

    tm = lax.dot_general(wt_ref[...], h, (((1,), (1,)), ((), ())),
                         preferred_element_type=F32)
    o_qi, o_v, o_wi = ATT_WIDTH, 2 * ATT_WIDTH, 2 * ATT_WIDTH + KV_WIDTH
    for jb in range(n_sub):
        tok = slice(jb * TQ, (jb + 1) * TQ)
        for hh in range(N_HEADS):
            kv, gi = divmod(hh, GROUP)
            qt_ref[0, jb, kv, :, gi * TQ:(gi + 1) * TQ] = (
                tm[hh * HEAD_DIM:(hh + 1) * HEAD_DIM, tok] * (HEAD_DIM ** -0.5 * LOG2_E)).astype(BF)
        for hh in range(N_IDX_HEADS):
            qit_ref[0, jb, :, hh * TQ:(hh + 1) * TQ] = (
                tm[o_qi + hh * IDX_DIM:o_qi + (hh + 1) * IDX_DIM, tok] * (IDX_DIM ** -0.5)).astype(BF)
    for kv in range(N_KV_HEADS):
        vt_ref[0, 0, kv * V_ROWS:kv * V_ROWS + HEAD_DIM, :] = (
            tm[o_v + kv * HEAD_DIM:o_v + (kv + 1) * HEAD_DIM, :].astype(BF))
        vt_ref[0, 0, kv * V_ROWS + HEAD_DIM:(kv + 1) * V_ROWS, :] = jnp.ones(
            (BF16_ROWS, tm.shape[1]), BF)
    wit_ref[0] = tm[o_wi:o_wi + N_IDX_HEADS, :] * (N_IDX_HEADS ** -0.5)


def _proj_call(x, wts, wmix, bmix, *, tt, lt, emit_vn):
    b, s, _ = x.shape
    n_sub = tt // SGU_LEN
    r = lt // tt
    grid = (b, s // tt)
    tok = lambda w, dt=F32: (jax.ShapeDtypeStruct((b, s, w), dt),
                             pl.BlockSpec((1, tt, w), lambda i, j: (i, j, 0)))
    outs = [
        tok(D_MODEL, BF),
        tok(D_MODEL),
        tok(ATT_WIDTH),
        tok(KV_WIDTH), tok(KV_WIDTH), tok(IDX_DIM),
        (jax.ShapeDtypeStruct((b, N_KV_HEADS, s, HEAD_DIM), BF),
         pl.BlockSpec((1, N_KV_HEADS, tt, HEAD_DIM), lambda i, j: (i, 0, j, 0))),
        tok(IDX_DIM, BF),
        (jax.ShapeDtypeStruct((b, s // TQ, N_KV_HEADS, HEAD_DIM, GROUP * TQ), BF),
         pl.BlockSpec((1, n_sub, N_KV_HEADS, HEAD_DIM, GROUP * TQ), lambda i, j: (i, j, 0, 0, 0))),
        (jax.ShapeDtypeStruct((b, s // TQ, IDX_DIM, N_IDX_HEADS * TQ), BF),
         pl.BlockSpec((1, n_sub, IDX_DIM, N_IDX_HEADS * TQ), lambda i, j: (i, j, 0, 0))),
        (jax.ShapeDtypeStruct((b, N_IDX_HEADS, s), F32),
         pl.BlockSpec((1, N_IDX_HEADS, tt), lambda i, j: (i, 0, j))),
        (jax.ShapeDtypeStruct((b, s // lt, N_KV_HEADS * V_ROWS, lt), BF),
         pl.BlockSpec((1, 1, N_KV_HEADS * V_ROWS, tt), lambda i, j: (i, j // r, 0, j % r))),
    ]
    if emit_vn:
        outs.append(tok(SGU_WIDTH))
    consts = [wts["norm_g"], wts["w_uvz"], wts["w_nat"], wts["w_g"], wts["w_t"],
              wts["ln_g"], wts["ln_b"], wmix, bmix, wts["w_pa"]]
    return pl.pallas_call(
        functools.partial(_proj_kernel, n_sub=n_sub),
        grid=grid,
        in_specs=[pl.BlockSpec((1, tt, D_MODEL), lambda i, j: (i, j, 0))]
                 + [_const_spec(c.shape) for c in consts],
        out_specs=[o[1] for o in outs],
        out_shape=[o[0] for o in outs],
        compiler_params=pltpu.CompilerParams(
            dimension_semantics=("arbitrary", "arbitrary"), vmem_limit_bytes=VMEM_LIMIT),
        name="proj_vn" if emit_vn else "proj",
    )(x, *consts)


def _key_to_f32(key_u):
    s = key_u ^ INT_MIN
    bits = s ^ (lax.shift_right_arithmetic(s, 31) & 0x7FFFFFFF)
    return lax.bitcast_convert_type(bits, F32)


def _attn_kernel(qt_ref, qit_ref, wit_ref, zb_ref, k_ref, vt_ref, ki_ref, tri_ref,
                 o_ref, sc_ref, bot_ref, *, n_tiles, topk, causal, stream_len, n_valid):
    lane = lax.broadcasted_iota(I32, (1, TQ), 1)
    if causal:
        q0 = pl.program_id(1) * TQ
        n_t = lax.div(q0 + TQ + LT - 1, LT)
        key_lo = jnp.zeros((1, TQ), I32)
        key_hi = (lax.shift_right_logical(q0 + lane, 6) + 1) * CHUNK
    else:
        n_t = n_tiles
        slab = (n_tiles * LT) // (TQ // stream_len)
        key_lo = lax.div(lane, stream_len) * slab
        key_hi = key_lo + n_valid

    qit = qit_ref[0, 0]
    wit = wit_ref[0]
    wrows = [wit[hh:hh + 1, :] for hh in range(N_IDX_HEADS)]
    row0 = lax.broadcasted_iota(I32, (LT, 1), 0)

    fold = lambda x, op: op(x.reshape(LT // SUBLANES, SUBLANES, TQ), axis=0)

    def score_tile(t, carry):
        mx, mn = carry
        start = pl.multiple_of(t * LT, LT)
        s = jnp.dot(ki_ref[0, pl.ds(start, LT), :], qit, preferred_element_type=F32)
        sc = jnp.zeros((LT, TQ), F32)
        for hh in range(N_IDX_HEADS):
            sc = sc + jnp.maximum(s[:, hh * TQ:(hh + 1) * TQ], 0.0) * wrows[hh]
        pos = row0 + t * LT
        ok = (pos >= key_lo) & (pos < key_hi)
        sc_ref[t] = jnp.where(ok, sc, -jnp.inf)
        return (jnp.maximum(mx, fold(jnp.where(ok, sc, -jnp.inf), jnp.max)),
                jnp.minimum(mn, fold(jnp.where(ok, sc, jnp.inf), jnp.min)))

    mx, mn = lax.fori_loop(0, n_t, score_tile, (jnp.full((SUBLANES, TQ), -jnp.inf, F32),
                                                jnp.full((SUBLANES, TQ), jnp.inf, F32)))
    mx = jnp.max(mx, axis=0, keepdims=True)
    mn = jnp.min(mn, axis=0, keepdims=True)

    def count(pred):
        def body(t, acc):
            return acc + fold(jnp.where(pred(sc_ref[t]), 1, 0).astype(I32), jnp.sum)
        acc = lax.fori_loop(0, n_t, body, jnp.zeros((SUBLANES, TQ), I32))
        return jnp.sum(acc, axis=0, keepdims=True)

    k_f = float(topk)
    n_adm = (key_hi - key_lo).astype(F32)
    short = n_adm < k_f

    def bisect_bits():
        def bit_pass(i, carry):
            thr_u, n_ge = carry
            cand_u = thr_u | lax.shift_left(jnp.int32(1), 31 - i)
            cand = _key_to_f32(cand_u)
            cnt = count(lambda sc: sc >= cand)
            take = cnt >= topk
            return jnp.where(take, cand_u, thr_u), jnp.where(take, cnt, n_ge)

        thr_u, n_ge = lax.fori_loop(
            0, 32, bit_pass, (jnp.zeros((1, TQ), I32), jnp.zeros((1, TQ), I32)))
        thr = jnp.where(short, -jnp.inf, _key_to_f32(thr_u))
        n_gt = count(lambda sc: sc > thr)
        need = jnp.where(short, 0, topk - n_gt).astype(F32)
        tie = jnp.logical_and(jnp.logical_not(short), n_ge > topk)
        return thr, need, jnp.max(jnp.where(tie, 1, 0))

    count_f = lambda pred: count(pred).astype(F32)
    n_nonneg = count_f(lambda sc: sc >= 0.0)
    n_pos = count_f(lambda sc: sc >= TINY)
    zero_thr = jnp.logical_and(n_pos < k_f, n_nonneg >= k_f)
    pos_side = n_pos >= k_f
    lo0 = jnp.where(pos_side, TINY, mn)
    hi0 = jnp.where(pos_side, mx, 0.0)
    clo0 = jnp.where(pos_side, n_pos, n_adm)
    chi0 = jnp.where(pos_side, 0.0, n_nonneg)
    logk = jnp.log(k_f)
    f_of = lambda c: jnp.log(jnp.maximum(c, 0.5)) - logk
    done0 = jnp.where(short | zero_thr | (clo0 == k_f), 1, 0)
    pending = lambda done: jnp.sum(1 - done)

    def search_more(st):
        return jnp.logical_and(st[0] < MAX_SEARCH_PASSES, st[1] > 0)

    def search_pass(st):
        it, _, lo, hi, clo, flo, fhi, last, done = st
        frac = jnp.clip(flo / (flo - fhi), SEARCH_CLIP, 1.0 - SEARCH_CLIP)
        cand = lo + (hi - lo) * frac
        cnt = count_f(lambda sc: sc >= cand)
        ge = cnt >= k_f
        up_lo = jnp.logical_and(done == 0, ge)
        up_hi = jnp.logical_and(done == 0, jnp.logical_not(ge))
        fhi = jnp.where(up_lo & (last == 1), fhi * 0.5, fhi)
        flo = jnp.where(up_hi & (last == -1), flo * 0.5, flo)
        lo = jnp.where(up_lo, cand, lo)
        clo = jnp.where(up_lo, cnt, clo)
        flo = jnp.where(up_lo, f_of(cnt), flo)
        hi = jnp.where(up_hi, cand, hi)
        fhi = jnp.where(up_hi, f_of(cnt), fhi)
        last = jnp.where(up_lo, 1, jnp.where(up_hi, -1, last))
        done = jnp.where(clo == k_f, 1, done)
        return it + 1, pending(done), lo, hi, clo, flo, fhi, last, done

    st = lax.while_loop(search_more, search_pass,
                        (jnp.int32(0), pending(done0), lo0, hi0, clo0, f_of(clo0), f_of(chi0),
                         jnp.zeros((1, TQ), I32), done0))

    def searched():
        thr = jnp.where(short, -jnp.inf, jnp.where(zero_thr, 0.0, st[2]))
        tie = jnp.logical_and(zero_thr, n_nonneg > k_f)
        need = jnp.where(short, 0.0, jnp.where(tie, k_f - n_pos, ALL_TIES))
        return thr, need, jnp.max(jnp.where(tie, 1, 0))

    thr, need, n_tie = lax.cond(st[1] == 0, searched, bisect_bits)
    any_tie = n_tie > 0

    def attend(with_ties):
        def attend_tile(t, carry):
            n_eq, ms, accs = carry
            sc = sc_ref[t]
            if with_ties:
                eq = sc == thr
                pref = jnp.dot(tri_ref[...], jnp.where(eq, 1.0, 0.0).astype(BF),
                               preferred_element_type=F32)
                sel = (sc > thr) | (eq & (n_eq + pref - 1.0 < need))
                n_eq = n_eq + pref[LT - 1:LT, :]
            else:
                sel = (sc >= thr) & (sc > -jnp.inf)
            bias = jnp.where(sel, 0.0, MASKED_LOGIT)
            bias = jnp.concatenate([bias] * GROUP, axis=1)
            start = pl.multiple_of(t * LT, LT)
            logits = [jnp.dot(k_ref[0, kv, pl.ds(start, LT), :], qt_ref[0, 0, kv],
                              preferred_element_type=F32) for kv in range(N_KV_HEADS)]
            new_m, ps, alphas = [], [], []
            for kv in range(N_KV_HEADS):
                lg = logits[kv] + bias
                m_new = jnp.maximum(ms[kv], jnp.max(lg, axis=0, keepdims=True))
                ps.append(jnp.exp2(lg - m_new).astype(BF))
                alphas.append(jnp.exp2(ms[kv] - m_new))
                new_m.append(m_new)
            new_acc = []
            for kv in range(N_KV_HEADS):
                pv = jnp.dot(vt_ref[0, t, kv * V_ROWS:(kv + 1) * V_ROWS, :], ps[kv],
                             preferred_element_type=F32)
                new_acc.append(alphas[kv] * accs[kv] + pv)
            return n_eq, tuple(new_m), tuple(new_acc)

        init = (jnp.zeros((1, TQ), F32),
                tuple(jnp.full((1, GROUP * TQ), MASKED_LOGIT, F32) for _ in range(N_KV_HEADS)),
                tuple(jnp.zeros((V_ROWS, GROUP * TQ), F32) for _ in range(N_KV_HEADS)))
        return lax.fori_loop(0, n_t, attend_tile, init)[2]

    accs = lax.cond(any_tie, lambda: attend(True), lambda: attend(False))

    for kv in range(N_KV_HEADS):
        o_t = accs[kv][:HEAD_DIM] / accs[kv][HEAD_DIM:HEAD_DIM + 1]
        for gi in range(GROUP):
            hh = GROUP * kv + gi
            bot_ref[hh * HEAD_DIM:(hh + 1) * HEAD_DIM, :] = o_t[:, gi * TQ:(gi + 1) * TQ]
    o_ref[0] = (bot_ref[...].T * jax.nn.silu(zb_ref[0])).astype(BF)


def _attn_call(qt, qit, wit, zb, k_hm, vt, ki_b, *, topk, causal, stream_len=0, n_valid=0):
    b, nq = qt.shape[0], qt.shape[1]
    s = nq * TQ
    n_tiles = vt.shape[1]
    l_all = k_hm.shape[2]
    assert l_all == n_tiles * LT
    tri = (lax.broadcasted_iota(I32, (LT, LT), 0) >= lax.broadcasted_iota(I32, (LT, LT), 1)).astype(BF)
    whole = lambda shape: pl.BlockSpec((1,) + shape, lambda i, j: (i,) + (0,) * len(shape),
                                       pipeline_mode=pl.Buffered(1))
    return pl.pallas_call(
        functools.partial(_attn_kernel, n_tiles=n_tiles, topk=topk, causal=causal,
                          stream_len=stream_len, n_valid=n_valid),
        grid=(b, nq),
        in_specs=[
            pl.BlockSpec((1, 1, N_KV_HEADS, HEAD_DIM, GROUP * TQ), lambda i, j: (i, j, 0, 0, 0)),
            pl.BlockSpec((1, 1, IDX_DIM, N_IDX_HEADS * TQ), lambda i, j: (i, j, 0, 0)),
            pl.BlockSpec((1, N_IDX_HEADS, TQ), lambda i, j: (i, 0, j)),
            pl.BlockSpec((1, TQ, ATT_WIDTH), lambda i, j: (i, j, 0)),
            whole((N_KV_HEADS, l_all, HEAD_DIM)),
            whole((n_tiles, N_KV_HEADS * V_ROWS, LT)),
            whole((l_all, IDX_DIM)),
            _const_spec((LT, LT)),
        ],
        out_specs=pl.BlockSpec((1, TQ, ATT_WIDTH), lambda i, j: (i, j, 0)),
        out_shape=jax.ShapeDtypeStruct((b, s, ATT_WIDTH), BF),
        scratch_shapes=[pltpu.VMEM((n_tiles, LT, TQ), F32), pltpu.VMEM((ATT_WIDTH, TQ), F32)],
        compiler_params=pltpu.CompilerParams(
            dimension_semantics=("arbitrary", "arbitrary"), vmem_limit_bytes=VMEM_LIMIT),
        name="attn_prompt" if causal else "attn_sample",
    )(qt, qit, wit, zb, k_hm, vt, ki_b, tri)


def _merge_kernel(x_ref, ma_ref, gb_ref, bg_ref, wpb_ref, wout_ref, fg_ref, y_ref):
    pb = jnp.dot(bg_ref[0], wpb_ref[...], preferred_element_type=F32)
    m = ma_ref[0].astype(F32) + jax.nn.sigmoid(gb_ref[0]) * pb
    xo = x_ref[0] + jnp.dot(m.astype(BF), wout_ref[...], preferred_element_type=F32)
    ms = jnp.mean(xo * xo, axis=-1, keepdims=True)
    y_ref[0] = xo * lax.rsqrt(ms + RMS_EPS) * fg_ref[...]


def _merge_call(x, ma, gb, bg, wts, *, tt):
    b, s, _ = x.shape
    tok = lambda w: pl.BlockSpec((1, tt, w), lambda i, j: (i, j, 0))
    consts = [wts["w_pb"], wts["w_out"], wts["final_g"]]
    return pl.pallas_call(
        _merge_kernel,
        grid=(b, s // tt),
        in_specs=[tok(D_MODEL), tok(D_MODEL), tok(D_MODEL), tok(ATT_WIDTH)]
                 + [_const_spec(c.shape) for c in consts],
        out_specs=tok(D_MODEL),
        out_shape=jax.ShapeDtypeStruct((b, s, D_MODEL), F32),
        compiler_params=pltpu.CompilerParams(
            dimension_semantics=("arbitrary", "arbitrary"), vmem_limit_bytes=VMEM_LIMIT),
        name="merge",
    )(x, ma, gb, bg, *consts)


def _prep_weights(norm_g, w_in, ln_g, ln_b, w_pa, w_pb, w_out, final_g):
    o = 0
    cols = {}
    for name, width in (("u", SGU_WIDTH), ("v", SGU_WIDTH), ("za", SGU_WIDTH), ("q", ATT_WIDTH),
                        ("k", KV_WIDTH), ("vv", KV_WIDTH), ("zb", ATT_WIDTH),
                        ("qi", N_IDX_HEADS * IDX_DIM), ("ki", IDX_DIM), ("wi", N_IDX_HEADS),
                        ("ga", D_MODEL), ("gb", D_MODEL)):
        cols[name] = (o, o + width)
        o += width
    wb = w_in.astype(BF)
    sl = lambda a, z: wb[:, cols[a][0]:cols[z][1]]
    return {
        "norm_g": norm_g.reshape(1, D_MODEL), "w_uvz": sl("u", "za"),
        "w_nat": jnp.concatenate([sl("k", "zb"), sl("ki", "ki")], axis=1),
        "w_g": sl("ga", "gb"),
        "w_t": jnp.concatenate([sl("q", "q"), sl("qi", "qi"), sl("vv", "vv"), sl("wi", "wi")], axis=1).T,
        "ln_g": ln_g.reshape(1, SGU_WIDTH), "ln_b": ln_b.reshape(1, SGU_WIDTH),
        "w_pa": w_pa.astype(BF), "w_pb": w_pb.astype(BF), "w_out": w_out.astype(BF),
        "final_g": final_g.reshape(1, D_MODEL),
    }


def _mix_operands(w_mix, b_pos):
    pairs = w_mix.reshape(SGU_GROUPS // 2, 2, SGU_LEN, SGU_LEN)
    wmix = jnp.concatenate([pairs[:, 0], pairs[:, 1]], axis=-1).astype(BF)
    bmix = jnp.repeat(b_pos, SGU_GDIM, axis=1)
    return wmix, bmix


def kernel(x_prompt, x_sample, cache_k, cache_v, cache_kidx, norm_g, w_in, sgu_ln_g, sgu_ln_b,
           sgu_w, sgu_b, w_pa, w_pb, w_out, final_g):
    b, s, _ = x_prompt.shape
    db, ds, _ = x_sample.shape
    past = cache_k.shape[2]
    assert norm_g.shape[0] == 1 and db * ds == TQ and s % LT == 0
    wts = _prep_weights(norm_g[0], w_in[0], sgu_ln_g[0], sgu_ln_b[0], w_pa[0], w_pb[0], w_out[0],
                        final_g)

    pos = jnp.arange(SGU_LEN)
    chunk_ok = (pos[None, :] // CHUNK) <= (pos[:, None] // CHUNK)
    wmix_p, bmix_p = _mix_operands(jnp.where(chunk_ok[None], sgu_w[0], 0.0), sgu_b[0].T)
    (ma, gb, zb, k, v, ki, k_hm, ki_b, qt, qit, wit, vt) = _proj_call(
        x_prompt, wts, wmix_p, bmix_p, tt=2 * TQ, lt=LT, emit_vn=False)
    bg = _attn_call(qt, qit, wit, zb, k_hm, vt, ki_b, topk=min(TOPK_MAX, s // 4), causal=True)
    y_prompt = _merge_call(x_prompt, ma, gb, bg, wts, tt=512)

    assert past % CHUNK == 0 and ds <= CHUNK
    w_blk = jnp.einsum("ab,gij->gaibj", jnp.eye(db, dtype=F32), sgu_w[0][:, :ds, :ds])
    wmix_s, bmix_s = _mix_operands(w_blk.reshape(SGU_GROUPS, SGU_LEN, SGU_LEN),
                                   jnp.tile(sgu_b[0][:, :ds].T, (db, 1)))
    n_tok = db * ds
    (ma_s, gb_s, zb_s, k_s, v_s, ki_s, _, _, qt_s, qit_s, wit_s, _, vn_s) = _proj_call(
        x_sample.reshape(1, n_tok, D_MODEL), wts, wmix_s, bmix_s, tt=n_tok, lt=n_tok, emit_vn=True)
    n_keys = past + ds
    slab = -(-n_keys * db // LT) * LT // db
    assert (slab * db) % LT == 0 and slab >= n_keys
    cat = lambda c, n, w: jnp.pad(
        jnp.concatenate([c.reshape(db, past, w), n.reshape(db, ds, w)], axis=1),
        ((0, 0), (0, slab - n_keys), (0, 0))).reshape(db * slab, w).astype(BF)
    k_all, v_all, ki_all = (cat(cache_k[0], k_s, KV_WIDTH), cat(cache_v[0], v_s, KV_WIDTH),
                            cat(cache_kidx[0], ki_s, IDX_DIM))
    n_t_s = db * slab // LT
    k_hm_s = k_all.reshape(1, db * slab, N_KV_HEADS, HEAD_DIM).transpose(0, 2, 1, 3)
    vt_s = v_all.reshape(n_t_s, LT, N_KV_HEADS, HEAD_DIM).transpose(0, 2, 3, 1)
    vt_s = jnp.concatenate([vt_s, jnp.ones((n_t_s, N_KV_HEADS, BF16_ROWS, LT), BF)], axis=2)
    vt_s = vt_s.reshape(1, n_t_s, N_KV_HEADS * V_ROWS, LT)
    bg_s = _attn_call(qt_s, qit_s, wit_s, zb_s, k_hm_s, vt_s, ki_all[None],
                      topk=min(TOPK_MAX, n_keys // 4), causal=False, stream_len=ds, n_valid=n_keys)
    y_sample = _merge_call(x_sample.reshape(1, n_tok, D_MODEL), ma_s, gb_s, bg_s, wts, tt=n_tok)

    return (y_prompt, y_sample.reshape(db, ds, D_MODEL),
            k.reshape(1, b, s, N_KV_HEADS, HEAD_DIM), v.reshape(1, b, s, N_KV_HEADS, HEAD_DIM),
            ki.reshape(1, b, s, IDX_DIM),
            k_s.reshape(1, db, ds, N_KV_HEADS, HEAD_DIM), v_s.reshape(1, db, ds, N_KV_HEADS, HEAD_DIM),
            ki_s.reshape(1, db, ds, IDX_DIM), vn_s.reshape(1, db, ds, SGU_WIDTH))
```

```python
import functools

import jax
import jax.numpy as jnp
from jax import lax
from jax.experimental import pallas as pl
from jax.experimental.pallas import tpu as pltpu

D_MODEL = 1024
CHUNK = 64
SGU_LEN = 128
SGU_GROUPS = 8
SGU_WIDTH = 512
SGU_GDIM = SGU_WIDTH // SGU_GROUPS
N_HEADS = 8
N_KV_HEADS = 4
HEAD_DIM = 64
GROUP = N_HEADS // N_KV_HEADS
ATT_WIDTH = N_HEADS * HEAD_DIM
KV_WIDTH = N_KV_HEADS * HEAD_DIM
N_IDX_HEADS = 8
IDX_DIM = 64
TOPK_MAX = 256
RMS_EPS = 1e-6
LN_EPS = 1e-5

BF = jnp.bfloat16
F32 = jnp.float32
I32 = jnp.int32

LANES = 128
SUBLANES = 8
BF16_ROWS = 16
TQ = LANES
LT = 512
V_ROWS = HEAD_DIM + BF16_ROWS
VMEM_LIMIT = 52 * 1024 * 1024
MASKED_LOGIT = -1e30
TINY = 1.1754944e-38
ALL_TIES = 3e38
UNCHECKED_SEARCH_PASSES = 8
MAX_SEARCH_PASSES = 28
SEARCH_CLIP = 0.05
LOG2_E = 1.4426950408889634
INT_MIN = -2147483648


def _const_spec(shape):
    nd = len(shape)
    return pl.BlockSpec(shape, lambda *_: (0,) * nd, pipeline_mode=pl.Buffered(1))


def _proj_kernel(x_ref, ng_ref, wuvz_ref, wnat_ref, wg_ref, wt_ref,
                 lng_ref, lnb_ref, wmix_ref, bmix_ref, wpa_ref,
                 ma_ref, gb_ref, zb_ref, k_ref, v_ref, ki_ref, khm_ref, kib_ref,
                 qt_ref, qit_ref, wit_ref, vt_ref, *vn_refs, n_sub):
    x = x_ref[0]
    ms = jnp.mean(x * x, axis=-1, keepdims=True)
    h = (x * lax.rsqrt(ms + RMS_EPS) * ng_ref[...]).astype(BF)

    uvz = jnp.dot(h, wuvz_ref[...], preferred_element_type=F32)
    u = jax.nn.gelu(uvz[:, :SGU_WIDTH])
    v = jax.nn.gelu(uvz[:, SGU_WIDTH:2 * SGU_WIDTH])
    za = uvz[:, 2 * SGU_WIDTH:]
    mu = jnp.mean(v, axis=-1, keepdims=True)
    var = jnp.mean(jnp.square(v - mu), axis=-1, keepdims=True)
    vn = (v - mu) * lax.rsqrt(var + LN_EPS) * lng_ref[...] + lnb_ref[...]
    if vn_refs:
        vn_refs[0][0] = vn
    lane = lax.broadcasted_iota(I32, (SGU_LEN, LANES), 1)
    cols = []
    for c in range(SGU_WIDTH // LANES):
        rows = []
        for s in range(n_sub):
            blk = vn[s * SGU_LEN:(s + 1) * SGU_LEN, c * LANES:(c + 1) * LANES]
            lo = jnp.where(lane < SGU_GDIM, blk, 0.0).astype(BF)
            hi = jnp.where(lane >= SGU_GDIM, blk, 0.0).astype(BF)
            rhs = jnp.concatenate([lo, hi], axis=0)
            rows.append(jnp.dot(wmix_ref[c], rhs, preferred_element_type=F32))
        cols.append(jnp.concatenate(rows, axis=0) if n_sub > 1 else rows[0])
    mixed = jnp.concatenate(cols, axis=1)
    bias = bmix_ref[...]
    if n_sub > 1:
        bias = jnp.concatenate([bias] * n_sub, axis=0)
    a_out = u * (mixed + bias) * jax.nn.silu(za)
    pa = jnp.dot(a_out.astype(BF), wpa_ref[...], preferred_element_type=F32)
    g = jnp.dot(h, wg_ref[...], preferred_element_type=F32)
    ma_ref[0] = (jax.nn.sigmoid(g[:, :D_MODEL]) * pa).astype(BF)
    gb_ref[0] = g[:, D_MODEL:]

    nat = jnp.dot(h, wnat_ref[...], preferred_element_type=F32)
    k = nat[:, :KV_WIDTH]
    k_ref[0] = k
    v_ref[0] = nat[:, KV_WIDTH:2 * KV_WIDTH]
    zb_ref[0] = nat[:, 2 * KV_WIDTH:2 * KV_WIDTH + ATT_WIDTH]
    ki = nat[:, 2 * KV_WIDTH + ATT_WIDTH:]
    ki_ref[0] = ki
    kib_ref[0] = ki.astype(BF)
    for kv in range(N_KV_HEADS):
        khm_ref[0, kv] = k[:, kv * HEAD_DIM:(kv + 1) * HEAD_DIM].astype(BF)

    tm = lax.dot_general(wt_ref[...], h, (((1,), (1,)), ((), ())),
                         preferred_element_type=F32)
    o_qi, o_v, o_wi = ATT_WIDTH, 2 * ATT_WIDTH, 2 * ATT_WIDTH + KV_WIDTH
    for jb in range(n_sub):
        tok = slice(jb * TQ, (jb + 1) * TQ)
        for hh in range(N_HEADS):
            kv, gi = divmod(hh, GROUP)
            qt_ref[0, jb, kv, :, gi * TQ:(gi + 1) * TQ] = (
                tm[hh * HEAD_DIM:(hh + 1) * HEAD_DIM, tok] * (HEAD_DIM ** -0.5 * LOG2_E)).astype(BF)
        for hh in range(N_IDX_HEADS):
            qit_ref[0, jb, :, hh * TQ:(hh + 1) * TQ] = (
                tm[o_qi + hh * IDX_DIM:o_qi + (hh + 1) * IDX_DIM, tok] * (IDX_DIM ** -0.5)).astype(BF)
    for kv in range(N_KV_HEADS):
        vt_ref[0, 0, kv * V_ROWS:kv * V_ROWS + HEAD_DIM, :] = (
            tm[o_v + kv * HEAD_DIM:o_v + (kv + 1) * HEAD_DIM, :].astype(BF))
        vt_ref[0, 0, kv * V_ROWS + HEAD_DIM:(kv + 1) * V_ROWS, :] = jnp.ones(
            (BF16_ROWS, tm.shape[1]), BF)
    wit_ref[0] = tm[o_wi:o_wi + N_IDX_HEADS, :] * (N_IDX_HEADS ** -0.5)


def _proj_call(x, wts, wmix, bmix, *, tt, lt, emit_vn):
    b, s, _ = x.shape
    n_sub = tt // SGU_LEN
    r = lt // tt
    grid = (b, s // tt)
    tok = lambda w, dt=F32: (jax.ShapeDtypeStruct((b, s, w), dt),
                             pl.BlockSpec((1, tt, w), lambda i, j: (i, j, 0)))
    outs = [
        tok(D_MODEL, BF),
        tok(D_MODEL),
        tok(ATT_WIDTH),
        tok(KV_WIDTH), tok(KV_WIDTH), tok(IDX_DIM),
        (jax.ShapeDtypeStruct((b, N_KV_HEADS, s, HEAD_DIM), BF),
         pl.BlockSpec((1, N_KV_HEADS, tt, HEAD_DIM), lambda i, j: (i, 0, j, 0))),
        tok(IDX_DIM, BF),
        (jax.ShapeDtypeStruct((b, s // TQ, N_KV_HEADS, HEAD_DIM, GROUP * TQ), BF),
         pl.BlockSpec((1, n_sub, N_KV_HEADS, HEAD_DIM, GROUP * TQ), lambda i, j: (i, j, 0, 0, 0))),
        (jax.ShapeDtypeStruct((b, s // TQ, IDX_DIM, N_IDX_HEADS * TQ), BF),
         pl.BlockSpec((1, n_sub, IDX_DIM, N_IDX_HEADS * TQ), lambda i, j: (i, j, 0, 0))),
        (jax.ShapeDtypeStruct((b, N_IDX_HEADS, s), F32),
         pl.BlockSpec((1, N_IDX_HEADS, tt), lambda i, j: (i, 0, j))),
        (jax.ShapeDtypeStruct((b, s // lt, N_KV_HEADS * V_ROWS, lt), BF),
         pl.BlockSpec((1, 1, N_KV_HEADS * V_ROWS, tt), lambda i, j: (i, j // r, 0, j % r))),
    ]
    if emit_vn:
        outs.append(tok(SGU_WIDTH))
    consts = [wts["norm_g"], wts["w_uvz"], wts["w_nat"], wts["w_g"], wts["w_t"],
              wts["ln_g"], wts["ln_b"], wmix, bmix, wts["w_pa"]]
    return pl.pallas_call(
        functools.partial(_proj_kernel, n_sub=n_sub),
        grid=grid,
        in_specs=[pl.BlockSpec((1, tt, D_MODEL), lambda i, j: (i, j, 0))]
                 + [_const_spec(c.shape) for c in consts],
        out_specs=[o[1] for o in outs],
        out_shape=[o[0] for o in outs],
        compiler_params=pltpu.CompilerParams(
            dimension_semantics=("arbitrary", "arbitrary"), vmem_limit_bytes=VMEM_LIMIT),
        name="proj_vn" if emit_vn else "proj",
    )(x, *consts)


def _key_to_f32(key_u):
    s = key_u ^ INT_MIN
    bits = s ^ (lax.shift_right_arithmetic(s, 31) & 0x7FFFFFFF)
    return lax.bitcast_convert_type(bits, F32)


def _attn_kernel(qt_ref, qit_ref, wit_ref, zb_ref, k_ref, vt_ref, ki_ref, tri_ref,
                 o_ref, sc_ref, bot_ref, mma_ref, mmb_ref, *, n_tiles, topk, causal, stream_len,
                 n_valid):
    lane = lax.broadcasted_iota(I32, (1, TQ), 1)
    if causal:
        q0 = pl.program_id(1) * TQ
        n_t = lax.div(q0 + TQ + LT - 1, LT)
        key_lo = jnp.zeros((1, TQ), I32)
        key_hi = (lax.shift_right_logical(q0 + lane, 6) + 1) * CHUNK
    else:
        n_t = n_tiles
        slab = (n_tiles * LT) // (TQ // stream_len)
        key_lo = lax.div(lane, stream_len) * slab
        key_hi = key_lo + n_valid

    qit = qit_ref[0, 0]
    wit = wit_ref[0]
    wrows = [wit[hh:hh + 1, :] for hh in range(N_IDX_HEADS)]
    row0 = lax.broadcasted_iota(I32, (LT, 1), 0)

    fold = lambda x, op: op(x.reshape(LT // SUBLANES, SUBLANES, TQ), axis=0)

    last = n_t - 1
    n_pairs = lax.div(n_t + 1, 2)

    def idx_dot(t, buf):
        start = pl.multiple_of(t * LT, LT)
        buf[...] = jnp.dot(ki_ref[0, pl.ds(start, LT), :], qit, preferred_element_type=F32)

    def score_tile(t, buf, real, carry):
        mx, mn, n_nonneg, n_pos = carry
        sc = jnp.zeros((LT, TQ), F32)
        for hh in range(N_IDX_HEADS):
            sc = sc + jnp.maximum(buf[:, hh * TQ:(hh + 1) * TQ], 0.0) * wrows[hh]
        pos = row0 + t * LT
        ok = pos < key_hi if causal else (pos >= key_lo) & (pos < key_hi)
        mn = jnp.minimum(mn, fold(sc, jnp.min))
        sc = jnp.where(ok, sc, -jnp.inf)
        sc_ref[t] = sc
        ones = lambda m: jnp.where(real, fold(jnp.where(m, 1, 0).astype(I32), jnp.sum), 0)
        return (jnp.maximum(mx, fold(sc, jnp.max)), mn,
                n_nonneg + ones(sc >= 0.0), n_pos + ones(sc >= TINY))

    def score_pair(i, carry):
        t0 = 2 * i
        t1 = jnp.minimum(t0 + 1, last)
        idx_dot(t1, mmb_ref)
        carry = score_tile(t0, mma_ref, True, carry)
        idx_dot(jnp.minimum(t0 + 2, last), mma_ref)
        return score_tile(t1, mmb_ref, t0 + 1 <= last, carry)

    idx_dot(0, mma_ref)
    mx, mn, n_nonneg, n_pos = lax.fori_loop(
        0, n_pairs, score_pair,
        (jnp.full((SUBLANES, TQ), -jnp.inf, F32), jnp.full((SUBLANES, TQ), jnp.inf, F32),
         jnp.zeros((SUBLANES, TQ), I32), jnp.zeros((SUBLANES, TQ), I32)))
    mx = jnp.max(mx, axis=0, keepdims=True)
    mn = jnp.min(mn, axis=0, keepdims=True)
    n_nonneg = jnp.sum(n_nonneg, axis=0, keepdims=True).astype(F32)
    n_pos = jnp.sum(n_pos, axis=0, keepdims=True).astype(F32)

    def count(pred):
        def body(t, acc):
            return acc + fold(jnp.where(pred(sc_ref[t]), 1, 0).astype(I32), jnp.sum)
        acc = lax.fori_loop(0, n_t, body, jnp.zeros((SUBLANES, TQ), I32))
        return jnp.sum(acc, axis=0, keepdims=True)

    k_f = float(topk)
    n_adm = (key_hi - key_lo).astype(F32)
    short = n_adm < k_f

    def bisect_bits():
        def bit_pass(i, carry):
            thr_u, n_ge = carry
            cand_u = thr_u | lax.shift_left(jnp.int32(1), 31 - i)
            cand = _key_to_f32(cand_u)
            cnt = count(lambda sc: sc >= cand)
            take = cnt >= topk
            return jnp.where(take, cand_u, thr_u), jnp.where(take, cnt, n_ge)

        thr_u, n_ge = lax.fori_loop(
            0, 32, bit_pass, (jnp.zeros((1, TQ), I32), jnp.zeros((1, TQ), I32)))
        thr = jnp.where(short, -jnp.inf, _key_to_f32(thr_u))
        n_gt = count(lambda sc: sc > thr)
        need = jnp.where(short, 0, topk - n_gt).astype(F32)
        tie = jnp.logical_and(jnp.logical_not(short), n_ge > topk)
        return thr, need, jnp.max(jnp.where(tie, 1, 0))

    count_f = lambda pred: count(pred).astype(F32)
    zero_thr = jnp.logical_and(n_pos < k_f, n_nonneg >= k_f)
    pos_side = n_pos >= k_f
    lo0 = jnp.where(pos_side, TINY, mn)
    hi0 = jnp.where(pos_side, mx, 0.0)
    clo0 = jnp.where(pos_side, n_pos, n_adm)
    chi0 = jnp.where(pos_side, 0.0, n_nonneg)
    logk = jnp.log(k_f)
    f_of = lambda c: jnp.log(jnp.maximum(c, 0.5)) - logk
    done0 = jnp.where(short | zero_thr | (clo0 == k_f), 1, 0)
    pending = lambda done: jnp.sum(1 - done)

    def search_pass(st):
        lo, hi, clo, flo, fhi, side, done = st
        frac = jnp.clip(flo / (flo - fhi), SEARCH_CLIP, 1.0 - SEARCH_CLIP)
        cand = lo + (hi - lo) * frac
        cnt = count_f(lambda sc: sc >= cand)
        ge = cnt >= k_f
        up_lo = jnp.logical_and(done == 0, ge)
        up_hi = jnp.logical_and(done == 0, jnp.logical_not(ge))
        fhi = jnp.where(up_lo & (side == 1), fhi * 0.5, fhi)
        flo = jnp.where(up_hi & (side == -1), flo * 0.5, flo)
        lo = jnp.where(up_lo, cand, lo)
        clo = jnp.where(up_lo, cnt, clo)
        flo = jnp.where(up_lo, f_of(cnt), flo)
        hi = jnp.where(up_hi, cand, hi)
        fhi = jnp.where(up_hi, f_of(cnt), fhi)
        side = jnp.where(up_lo, 1, jnp.where(up_hi, -1, side))
        done = jnp.where(clo == k_f, 1, done)
        return lo, hi, clo, flo, fhi, side, done

    st = lax.fori_loop(0, UNCHECKED_SEARCH_PASSES, lambda i, s: search_pass(s),
                       (lo0, hi0, clo0, f_of(clo0), f_of(chi0), jnp.zeros((1, TQ), I32), done0))

    def checked_pass(c):
        s = search_pass(c[2])
        return c[0] + 1, pending(s[6]), s

    _, n_pending, st = lax.while_loop(
        lambda c: jnp.logical_and(c[0] < MAX_SEARCH_PASSES, c[1] > 0), checked_pass,
        (jnp.int32(UNCHECKED_SEARCH_PASSES), pending(st[6]), st))

    def searched():
        thr = jnp.where(short, -jnp.inf, jnp.where(zero_thr, 0.0, st[0]))
        tie = jnp.logical_and(zero_thr, n_nonneg > k_f)
        need = jnp.where(short, 0.0, jnp.where(tie, k_f - n_pos, ALL_TIES))
        return thr, need, jnp.max(jnp.where(tie, 1, 0))

    thr, need, n_tie = lax.cond(n_pending == 0, searched, bisect_bits)
    any_tie = n_tie > 0

    gw = GROUP * TQ

    def qk_dots(t, buf):
        start = pl.multiple_of(t * LT, LT)
        for kv in range(N_KV_HEADS):
            buf[:, kv * gw:(kv + 1) * gw] = jnp.dot(
                k_ref[0, kv, pl.ds(start, LT), :], qt_ref[0, 0, kv], preferred_element_type=F32)

    def attend(with_ties):
        def attend_tile(t, buf, real, carry):
            n_eq, ms, accs = carry
            sc = sc_ref[t]
            thr_t = jnp.where(real, thr, jnp.inf)
            if with_ties:
                eq = sc == thr_t
                pref = jnp.dot(tri_ref[...], jnp.where(eq, 1.0, 0.0).astype(BF),
                               preferred_element_type=F32)
                sel = (sc > thr_t) | (eq & (n_eq + pref - 1.0 < need))
                n_eq = n_eq + pref[LT - 1:LT, :]
            else:
                sel = (sc >= thr_t) & (sc > -jnp.inf)
            bias = jnp.where(sel, 0.0, MASKED_LOGIT)
            bias = jnp.concatenate([bias] * GROUP, axis=1)
            new_m, ps, alphas = [], [], []
            for kv in range(N_KV_HEADS):
                lg = buf[:, kv * gw:(kv + 1) * gw] + bias
                m_new = jnp.maximum(ms[kv], jnp.max(lg, axis=0, keepdims=True))
                ps.append(jnp.exp2(lg - m_new).astype(BF))
                alphas.append(jnp.exp2(ms[kv] - m_new))
                new_m.append(m_new)
            new_acc = []
            for kv in range(N_KV_HEADS):
                pv = jnp.dot(vt_ref[0, t, kv * V_ROWS:(kv + 1) * V_ROWS, :], ps[kv],
                             preferred_element_type=F32)
                new_acc.append(alphas[kv] * accs[kv] + pv)
            return n_eq, tuple(new_m), tuple(new_acc)

        def attend_pair(i, carry):
            t0 = 2 * i
            t1 = jnp.minimum(t0 + 1, last)
            qk_dots(t1, mmb_ref)
            carry = attend_tile(t0, mma_ref, True, carry)
            qk_dots(jnp.minimum(t0 + 2, last), mma_ref)
            return attend_tile(t1, mmb_ref, t0 + 1 <= last, carry)

        init = (jnp.zeros((1, TQ), F32),
                tuple(jnp.full((1, gw), MASKED_LOGIT, F32) for _ in range(N_KV_HEADS)),
                tuple(jnp.zeros((V_ROWS, gw), F32) for _ in range(N_KV_HEADS)))
        qk_dots(0, mma_ref)
        return lax.fori_loop(0, n_pairs, attend_pair, init)[2]

    accs = lax.cond(any_tie, lambda: attend(True), lambda: attend(False))

    for kv in range(N_KV_HEADS):
        o_t = accs[kv][:HEAD_DIM] / accs[kv][HEAD_DIM:HEAD_DIM + 1]
        for gi in range(GROUP):
            hh = GROUP * kv + gi
            bot_ref[hh * HEAD_DIM:(hh + 1) * HEAD_DIM, :] = o_t[:, gi * TQ:(gi + 1) * TQ]
    o_ref[0] = (bot_ref[...].T * jax.nn.silu(zb_ref[0])).astype(BF)


def _attn_call(qt, qit, wit, zb, k_hm, vt, ki_b, *, topk, causal, stream_len=0, n_valid=0):
    b, nq = qt.shape[0], qt.shape[1]
    s = nq * TQ
    n_tiles = vt.shape[1]
    l_all = k_hm.shape[2]
    assert l_all == n_tiles * LT
    tri = (lax.broadcasted_iota(I32, (LT, LT), 0) >= lax.broadcasted_iota(I32, (LT, LT), 1)).astype(BF)
    whole = lambda shape: pl.BlockSpec((1,) + shape, lambda i, j: (i,) + (0,) * len(shape),
                                       pipeline_mode=pl.Buffered(1))
    return pl.pallas_call(
        functools.partial(_attn_kernel, n_tiles=n_tiles, topk=topk, causal=causal,
                          stream_len=stream_len, n_valid=n_valid),
        grid=(b, nq),
        in_specs=[
            pl.BlockSpec((1, 1, N_KV_HEADS, HEAD_DIM, GROUP * TQ), lambda i, j: (i, j, 0, 0, 0)),
            pl.BlockSpec((1, 1, IDX_DIM, N_IDX_HEADS * TQ), lambda i, j: (i, j, 0, 0)),
            pl.BlockSpec((1, N_IDX_HEADS, TQ), lambda i, j: (i, 0, j)),
            pl.BlockSpec((1, TQ, ATT_WIDTH), lambda i, j: (i, j, 0)),
            whole((N_KV_HEADS, l_all, HEAD_DIM)),
            whole((n_tiles, N_KV_HEADS * V_ROWS, LT)),
            whole((l_all, IDX_DIM)),
            _const_spec((LT, LT)),
        ],
        out_specs=pl.BlockSpec((1, TQ, ATT_WIDTH), lambda i, j: (i, j, 0)),
        out_shape=jax.ShapeDtypeStruct((b, s, ATT_WIDTH), BF),
        scratch_shapes=[pltpu.VMEM((n_tiles, LT, TQ), F32), pltpu.VMEM((ATT_WIDTH, TQ), F32),
                        pltpu.VMEM((LT, N_IDX_HEADS * TQ), F32), pltpu.VMEM((LT, N_IDX_HEADS * TQ), F32)],
        compiler_params=pltpu.CompilerParams(
            dimension_semantics=("arbitrary", "arbitrary"), vmem_limit_bytes=VMEM_LIMIT),
        name="attn_prompt" if causal else "attn_sample",
    )(qt, qit, wit, zb, k_hm, vt, ki_b, tri)


def _merge_kernel(x_ref, ma_ref, gb_ref, bg_ref, wpb_ref, wout_ref, fg_ref, y_ref):
    pb = jnp.dot(bg_ref[0], wpb_ref[...], preferred_element_type=F32)
    m = ma_ref[0].astype(F32) + jax.nn.sigmoid(gb_ref[0]) * pb
    xo = x_ref[0] + jnp.dot(m.astype(BF), wout_ref[...], preferred_element_type=F32)
    ms = jnp.mean(xo * xo, axis=-1, keepdims=True)
    y_ref[0] = xo * lax.rsqrt(ms + RMS_EPS) * fg_ref[...]


def _merge_call(x, ma, gb, bg, wts, *, tt):
    b, s, _ = x.shape
    tok = lambda w: pl.BlockSpec((1, tt, w), lambda i, j: (i, j, 0))
    consts = [wts["w_pb"], wts["w_out"], wts["final_g"]]
    return pl.pallas_call(
        _merge_kernel,
        grid=(b, s // tt),
        in_specs=[tok(D_MODEL), tok(D_MODEL), tok(D_MODEL), tok(ATT_WIDTH)]
                 + [_const_spec(c.shape) for c in consts],
        out_specs=tok(D_MODEL),
        out_shape=jax.ShapeDtypeStruct((b, s, D_MODEL), F32),
        compiler_params=pltpu.CompilerParams(
            dimension_semantics=("arbitrary", "arbitrary"), vmem_limit_bytes=VMEM_LIMIT),
        name="merge",
    )(x, ma, gb, bg, *consts)


def _prep_weights(norm_g, w_in, ln_g, ln_b, w_pa, w_pb, w_out, final_g):
    o = 0
    cols = {}
    for name, width in (("u", SGU_WIDTH), ("v", SGU_WIDTH), ("za", SGU_WIDTH), ("q", ATT_WIDTH),
                        ("k", KV_WIDTH), ("vv", KV_WIDTH), ("zb", ATT_WIDTH),
                        ("qi", N_IDX_HEADS * IDX_DIM), ("ki", IDX_DIM), ("wi", N_IDX_HEADS),
                        ("ga", D_MODEL), ("gb", D_MODEL)):
        cols[name] = (o, o + width)
        o += width
    wb = w_in.astype(BF)
    sl = lambda a, z: wb[:, cols[a][0]:cols[z][1]]
    return {
        "norm_g": norm_g.reshape(1, D_MODEL), "w_uvz": sl("u", "za"),
        "w_nat": jnp.concatenate([sl("k", "zb"), sl("ki", "ki")], axis=1),
        "w_g": sl("ga", "gb"),
        "w_t": jnp.concatenate([sl("q", "q"), sl("qi", "qi"), sl("vv", "vv"), sl("wi", "wi")], axis=1).T,
        "ln_g": ln_g.reshape(1, SGU_WIDTH), "ln_b": ln_b.reshape(1, SGU_WIDTH),
        "w_pa": w_pa.astype(BF), "w_pb": w_pb.astype(BF), "w_out": w_out.astype(BF),
        "final_g": final_g.reshape(1, D_MODEL),
    }


def _mix_operands(w_mix, b_pos):
    pairs = w_mix.reshape(SGU_GROUPS // 2, 2, SGU_LEN, SGU_LEN)
    wmix = jnp.concatenate([pairs[:, 0], pairs[:, 1]], axis=-1).astype(BF)
    bmix = jnp.repeat(b_pos, SGU_GDIM, axis=1)
    return wmix, bmix


def kernel(x_prompt, x_sample, cache_k, cache_v, cache_kidx, norm_g, w_in, sgu_ln_g, sgu_ln_b,
           sgu_w, sgu_b, w_pa, w_pb, w_out, final_g):
    b, s, _ = x_prompt.shape
    db, ds, _ = x_sample.shape
    past = cache_k.shape[2]
    assert norm_g.shape[0] == 1 and db * ds == TQ and s % LT == 0
    wts = _prep_weights(norm_g[0], w_in[0], sgu_ln_g[0], sgu_ln_b[0], w_pa[0], w_pb[0], w_out[0],
                        final_g)

    pos = jnp.arange(SGU_LEN)
    chunk_ok = (pos[None, :] // CHUNK) <= (pos[:, None] // CHUNK)
    wmix_p, bmix_p = _mix_operands(jnp.where(chunk_ok[None], sgu_w[0], 0.0), sgu_b[0].T)
    (ma, gb, zb, k, v, ki, k_hm, ki_b, qt, qit, wit, vt) = _proj_call(
        x_prompt, wts, wmix_p, bmix_p, tt=2 * TQ, lt=LT, emit_vn=False)
    bg = _attn_call(qt, qit, wit, zb, k_hm, vt, ki_b, topk=min(TOPK_MAX, s // 4), causal=True)
    y_prompt = _merge_call(x_prompt, ma, gb, bg, wts, tt=512)

    assert past % CHUNK == 0 and ds <= CHUNK
    w_blk = jnp.einsum("ab,gij->gaibj", jnp.eye(db, dtype=F32), sgu_w[0][:, :ds, :ds])
    wmix_s, bmix_s = _mix_operands(w_blk.reshape(SGU_GROUPS, SGU_LEN, SGU_LEN),
                                   jnp.tile(sgu_b[0][:, :ds].T, (db, 1)))
    n_tok = db * ds
    (ma_s, gb_s, zb_s, k_s, v_s, ki_s, _, _, qt_s, qit_s, wit_s, _, vn_s) = _proj_call(
        x_sample.reshape(1, n_tok, D_MODEL), wts, wmix_s, bmix_s, tt=n_tok, lt=n_tok, emit_vn=True)
    n_keys = past + ds
    slab = -(-n_keys * db // LT) * LT // db
    assert (slab * db) % LT == 0 and slab >= n_keys
    cat = lambda c, n, w: jnp.pad(
        jnp.concatenate([c.reshape(db, past, w), n.reshape(db, ds, w)], axis=1),
        ((0, 0), (0, slab - n_keys), (0, 0))).reshape(db * slab, w).astype(BF)
    k_all, v_all, ki_all = (cat(cache_k[0], k_s, KV_WIDTH), cat(cache_v[0], v_s, KV_WIDTH),
                            cat(cache_kidx[0], ki_s, IDX_DIM))
    n_t_s = db * slab // LT
    k_hm_s = k_all.reshape(1, db * slab, N_KV_HEADS, HEAD_DIM).transpose(0, 2, 1, 3)
    vt_s = v_all.reshape(n_t_s, LT, N_KV_HEADS, HEAD_DIM).transpose(0, 2, 3, 1)
    vt_s = jnp.concatenate([vt_s, jnp.ones((n_t_s, N_KV_HEADS, BF16_ROWS, LT), BF)], axis=2)
    vt_s = vt_s.reshape(1, n_t_s, N_KV_HEADS * V_ROWS, LT)
    bg_s = _attn_call(qt_s, qit_s, wit_s, zb_s, k_hm_s, vt_s, ki_all[None],
                      topk=min(TOPK_MAX, n_keys // 4), causal=False, stream_len=ds, n_valid=n_keys)
    y_sample = _merge_call(x_sample.reshape(1, n_tok, D_MODEL), ma_s, gb_s, bg_s, wts, tt=n_tok)

    return (y_prompt, y_sample.reshape(db, ds, D_MODEL),
            k.reshape(1, b, s, N_KV_HEADS, HEAD_DIM), v.reshape(1, b, s, N_KV_HEADS, HEAD_DIM),
            ki.reshape(1, b, s, IDX_DIM),
            k_s.reshape(1, db, ds, N_KV_HEADS, HEAD_DIM), v_s.reshape(1, db, ds, N_KV_HEADS, HEAD_DIM),
            ki_s.reshape(1, db, ds, IDX_DIM), vn_s.reshape(1, db, ds, SGU_WIDTH))
```

```python
import functools

import jax
import jax.numpy as jnp
from jax import lax
from jax.experimental import pallas as pl
from jax.experimental.pallas import tpu as pltpu

D_MODEL = 1024
CHUNK = 64
SGU_LEN = 128
SGU_GROUPS = 8
SGU_WIDTH = 512
SGU_GDIM = SGU_WIDTH // SGU_GROUPS
N_HEADS = 8
N_KV_HEADS = 4
HEAD_DIM = 64
GROUP = N_HEADS // N_KV_HEADS
ATT_WIDTH = N_HEADS * HEAD_DIM
KV_WIDTH = N_KV_HEADS * HEAD_DIM
N_IDX_HEADS = 8
IDX_DIM = 64
TOPK_MAX = 256
RMS_EPS = 1e-6
LN_EPS = 1e-5

BF = jnp.bfloat16
F32 = jnp.float32
I32 = jnp.int32

LANES = 128
SUBLANES = 8
BF16_ROWS = 16
TQ = LANES
LT = 512
V_ROWS = HEAD_DIM + BF16_ROWS
VMEM_LIMIT = 52 * 1024 * 1024
MASKED_LOGIT = -1e30
ALL_TIES = 3e38
UNCHECKED_SEARCH_PASSES = 8
MAX_SEARCH_PASSES = 28
MAX_WALK_STEPS = 8
SEARCH_CLIP = 0.05
LOG2_E = 1.4426950408889634
INT_MIN = -2147483648


def _const_spec(shape):
    nd = len(shape)
    return pl.BlockSpec(shape, lambda *_: (0,) * nd, pipeline_mode=pl.Buffered(1))


def _proj_kernel(x_ref, ng_ref, wuvz_ref, wnat_ref, wg_ref, wt_ref,
                 lng_ref, lnb_ref, wmix_ref, bmix_ref, wpa_ref,
                 ma_ref, gb_ref, zb_ref, k_ref, v_ref, ki_ref, khm_ref, kib_ref,
                 qt_ref, qit_ref, wit_ref, vt_ref, *vn_refs, n_sub):
    x = x_ref[0]
    ms = jnp.mean(x * x, axis=-1, keepdims=True)
    h = (x * lax.rsqrt(ms + RMS_EPS) * ng_ref[...]).astype(BF)

    uvz = jnp.dot(h, wuvz_ref[...], preferred_element_type=F32)
    u = jax.nn.gelu(uvz[:, :SGU_WIDTH])
    v = jax.nn.gelu(uvz[:, SGU_WIDTH:2 * SGU_WIDTH])
    za = uvz[:, 2 * SGU_WIDTH:]
    mu = jnp.mean(v, axis=-1, keepdims=True)
    var = jnp.mean(jnp.square(v - mu), axis=-1, keepdims=True)
    vn = (v - mu) * lax.rsqrt(var + LN_EPS) * lng_ref[...] + lnb_ref[...]
    if vn_refs:
        vn_refs[0][0] = vn
    lane = lax.broadcasted_iota(I32, (SGU_LEN, LANES), 1)
    cols = []
    for c in range(SGU_WIDTH // LANES):
        rows = []
        for s in range(n_sub):
            blk = vn[s * SGU_LEN:(s + 1) * SGU_LEN, c * LANES:(c + 1) * LANES]
            lo = jnp.where(lane < SGU_GDIM, blk, 0.0).astype(BF)
            hi = jnp.where(lane >= SGU_GDIM, blk, 0.0).astype(BF)
            rhs = jnp.concatenate([lo, hi], axis=0)
            rows.append(jnp.dot(wmix_ref[c], rhs, preferred_element_type=F32))
        cols.append(jnp.concatenate(rows, axis=0) if n_sub > 1 else rows[0])
    mixed = jnp.concatenate(cols, axis=1)
    bias = bmix_ref[...]
    if n_sub > 1:
        bias = jnp.concatenate([bias] * n_sub, axis=0)
    a_out = u * (mixed + bias) * jax.nn.silu(za)
    pa = jnp.dot(a_out.astype(BF), wpa_ref[...], preferred_element_type=F32)
    g = jnp.dot(h, wg_ref[...], preferred_element_type=F32)
    ma_ref[0] = (jax.nn.sigmoid(g[:, :D_MODEL]) * pa).astype(BF)
    gb_ref[0] = g[:, D_MODEL:]

    nat = jnp.dot(h, wnat_ref[...], preferred_element_type=F32)
    k = nat[:, :KV_WIDTH]
    k_ref[0] = k
    v_ref[0] = nat[:, KV_WIDTH:2 * KV_WIDTH]
    zb_ref[0] = nat[:, 2 * KV_WIDTH:2 * KV_WIDTH + ATT_WIDTH]
    ki = nat[:, 2 * KV_WIDTH + ATT_WIDTH:]
    ki_ref[0] = ki
    kib_ref[0] = ki.astype(BF)
    for kv in range(N_KV_HEADS):
        khm_ref[0, kv] = k[:, kv * HEAD_DIM:(kv + 1) * HEAD_DIM].astype(BF)

    tm = lax.dot_general(wt_ref[...], h, (((1,), (1,)), ((), ())),
                         preferred_element_type=F32)
    o_qi, o_v, o_wi = ATT_WIDTH, 2 * ATT_WIDTH, 2 * ATT_WIDTH + KV_WIDTH
    for jb in range(n_sub):
        tok = slice(jb * TQ, (jb + 1) * TQ)
        for hh in range(N_HEADS):
            kv, gi = divmod(hh, GROUP)
            qt_ref[0, jb, kv, :, gi * TQ:(gi + 1) * TQ] = (
                tm[hh * HEAD_DIM:(hh + 1) * HEAD_DIM, tok] * (HEAD_DIM ** -0.5 * LOG2_E)).astype(BF)
        for hh in range(N_IDX_HEADS):
            qit_ref[0, jb, :, hh * TQ:(hh + 1) * TQ] = (
                tm[o_qi + hh * IDX_DIM:o_qi + (hh + 1) * IDX_DIM, tok] * (IDX_DIM ** -0.5)).astype(BF)
    for kv in range(N_KV_HEADS):
        vt_ref[0, 0, kv * V_ROWS:kv * V_ROWS + HEAD_DIM, :] = (
            tm[o_v + kv * HEAD_DIM:o_v + (kv + 1) * HEAD_DIM, :].astype(BF))
        vt_ref[0, 0, kv * V_ROWS + HEAD_DIM:(kv + 1) * V_ROWS, :] = jnp.ones(
            (BF16_ROWS, tm.shape[1]), BF)
    wit_ref[0] = tm[o_wi:o_wi + N_IDX_HEADS, :] * (N_IDX_HEADS ** -0.5)


def _proj_call(x, wts, wmix, bmix, *, tt, lt, emit_vn):
    b, s, _ = x.shape
    n_sub = tt // SGU_LEN
    r = lt // tt
    grid = (b, s // tt)
    tok = lambda w, dt=F32: (jax.ShapeDtypeStruct((b, s, w), dt),
                             pl.BlockSpec((1, tt, w), lambda i, j: (i, j, 0)))
    outs = [
        tok(D_MODEL, BF),
        tok(D_MODEL),
        tok(ATT_WIDTH),
        tok(KV_WIDTH), tok(KV_WIDTH), tok(IDX_DIM),
        (jax.ShapeDtypeStruct((b, N_KV_HEADS, s, HEAD_DIM), BF),
         pl.BlockSpec((1, N_KV_HEADS, tt, HEAD_DIM), lambda i, j: (i, 0, j, 0))),
        tok(IDX_DIM, BF),
        (jax.ShapeDtypeStruct((b, s // TQ, N_KV_HEADS, HEAD_DIM, GROUP * TQ), BF),
         pl.BlockSpec((1, n_sub, N_KV_HEADS, HEAD_DIM, GROUP * TQ), lambda i, j: (i, j, 0, 0, 0))),
        (jax.ShapeDtypeStruct((b, s // TQ, IDX_DIM, N_IDX_HEADS * TQ), BF),
         pl.BlockSpec((1, n_sub, IDX_DIM, N_IDX_HEADS * TQ), lambda i, j: (i, j, 0, 0))),
        (jax.ShapeDtypeStruct((b, N_IDX_HEADS, s), F32),
         pl.BlockSpec((1, N_IDX_HEADS, tt), lambda i, j: (i, 0, j))),
        (jax.ShapeDtypeStruct((b, s // lt, N_KV_HEADS * V_ROWS, lt), BF),
         pl.BlockSpec((1, 1, N_KV_HEADS * V_ROWS, tt), lambda i, j: (i, j // r, 0, j % r))),
    ]
    if emit_vn:
        outs.append(tok(SGU_WIDTH))
    consts = [wts["norm_g"], wts["w_uvz"], wts["w_nat"], wts["w_g"], wts["w_t"],
              wts["ln_g"], wts["ln_b"], wmix, bmix, wts["w_pa"]]
    return pl.pallas_call(
        functools.partial(_proj_kernel, n_sub=n_sub),
        grid=grid,
        in_specs=[pl.BlockSpec((1, tt, D_MODEL), lambda i, j: (i, j, 0))]
                 + [_const_spec(c.shape) for c in consts],
        out_specs=[o[1] for o in outs],
        out_shape=[o[0] for o in outs],
        compiler_params=pltpu.CompilerParams(
            dimension_semantics=("arbitrary", "arbitrary"), vmem_limit_bytes=VMEM_LIMIT),
        name="proj_vn" if emit_vn else "proj",
    )(x, *consts)


def _key_to_f32(key_u):
    s = key_u ^ INT_MIN
    bits = s ^ (lax.shift_right_arithmetic(s, 31) & 0x7FFFFFFF)
    return lax.bitcast_convert_type(bits, F32)


def _f32_to_key(x):
    bits = lax.bitcast_convert_type(x, I32)
    return bits ^ (lax.shift_right_arithmetic(bits, 31) & 0x7FFFFFFF) ^ INT_MIN


def _attn_kernel(qt_ref, qit_ref, wit_ref, zb_ref, k_ref, vt_ref, ki_ref, tri_ref,
                 o_ref, sc_ref, bot_ref, mma_ref, mmb_ref, *, n_tiles, topk, causal, stream_len,
                 n_valid):
    lane = lax.broadcasted_iota(I32, (1, TQ), 1)
    if causal:
        q0 = pl.program_id(1) * TQ
        n_t = lax.div(q0 + TQ + LT - 1, LT)
        key_lo = jnp.zeros((1, TQ), I32)
        key_hi = (lax.shift_right_logical(q0 + lane, 6) + 1) * CHUNK
    else:
        n_t = n_tiles
        slab = (n_tiles * LT) // (TQ // stream_len)
        key_lo = lax.div(lane, stream_len) * slab
        key_hi = key_lo + n_valid

    qit = qit_ref[0, 0]
    wit = wit_ref[0]
    wrows = [wit[hh:hh + 1, :] for hh in range(N_IDX_HEADS)]
    row0 = lax.broadcasted_iota(I32, (LT, 1), 0)

    fold = lambda x, op: op(x.reshape(LT // SUBLANES, SUBLANES, TQ), axis=0)

    last = n_t - 1
    n_pairs = lax.div(n_t + 1, 2)

    def idx_dot(t, buf):
        start = pl.multiple_of(t * LT, LT)
        buf[...] = jnp.dot(ki_ref[0, pl.ds(start, LT), :], qit, preferred_element_type=F32)

    def score_tile(t, buf, real, carry):
        mx, mn, n_nonneg, n_pos = carry
        sc = jnp.zeros((LT, TQ), F32)
        for hh in range(N_IDX_HEADS):
            sc = sc + jnp.maximum(buf[:, hh * TQ:(hh + 1) * TQ], 0.0) * wrows[hh]
        pos = row0 + t * LT
        ok = pos < key_hi if causal else (pos >= key_lo) & (pos < key_hi)
        mn = jnp.minimum(mn, fold(sc, jnp.min))
        sc = jnp.where(ok, sc, -jnp.inf)
        sc_ref[t] = sc
        ones = lambda m: jnp.where(real, fold(jnp.where(m, 1, 0).astype(I32), jnp.sum), 0)
        return (jnp.maximum(mx, fold(sc, jnp.max)), mn,
                n_nonneg + ones(sc >= 0.0), n_pos + ones(sc > 0.0))

    def score_pair(i, carry):
        t0 = 2 * i
        t1 = jnp.minimum(t0 + 1, last)
        idx_dot(t1, mmb_ref)
        carry = score_tile(t0, mma_ref, True, carry)
        idx_dot(jnp.minimum(t0 + 2, last), mma_ref)
        return score_tile(t1, mmb_ref, t0 + 1 <= last, carry)

    idx_dot(0, mma_ref)
    mx, mn, n_nonneg, n_pos = lax.fori_loop(
        0, n_pairs, score_pair,
        (jnp.full((SUBLANES, TQ), -jnp.inf, F32), jnp.full((SUBLANES, TQ), jnp.inf, F32),
         jnp.zeros((SUBLANES, TQ), I32), jnp.zeros((SUBLANES, TQ), I32)))
    mx = jnp.max(mx, axis=0, keepdims=True)
    mn = jnp.min(mn, axis=0, keepdims=True)
    n_nonneg = jnp.sum(n_nonneg, axis=0, keepdims=True).astype(F32)
    n_pos = jnp.sum(n_pos, axis=0, keepdims=True).astype(F32)

    def count(pred):
        def body(t, acc):
            return acc + fold(jnp.where(pred(sc_ref[t]), 1, 0).astype(I32), jnp.sum)
        acc = lax.fori_loop(0, n_t, body, jnp.zeros((SUBLANES, TQ), I32))
        return jnp.sum(acc, axis=0, keepdims=True)

    k_f = float(topk)
    n_adm = (key_hi - key_lo).astype(F32)
    short = n_adm < k_f

    def bisect_bits():
        def bit_pass(i, carry):
            thr_u, n_ge = carry
            cand_u = thr_u | lax.shift_left(jnp.int32(1), 31 - i)
            cand = _key_to_f32(cand_u)
            cnt = count(lambda sc: sc >= cand)
            take = cnt >= topk
            return jnp.where(take, cand_u, thr_u), jnp.where(take, cnt, n_ge)

        thr_u, n_ge = lax.fori_loop(
            0, 32, bit_pass, (jnp.zeros((1, TQ), I32), jnp.zeros((1, TQ), I32)))
        thr = jnp.where(short, -jnp.inf, _key_to_f32(thr_u))
        n_gt = count(lambda sc: sc > thr)
        need = jnp.where(short, 0, topk - n_gt).astype(F32)
        tie = jnp.logical_and(jnp.logical_not(short), n_ge > topk)
        return thr, need, jnp.max(jnp.where(tie, 1, 0))

    count_f = lambda pred: count(pred).astype(F32)
    zero_thr = jnp.logical_and(n_pos < k_f, n_nonneg >= k_f)
    pos_side = n_pos >= k_f
    lo0 = jnp.where(pos_side, 0.0, mn)
    hi0 = jnp.where(pos_side, mx, 0.0)
    clo0 = jnp.where(pos_side, n_nonneg, n_adm)
    chi0 = jnp.where(pos_side, 0.0, n_nonneg)
    logk = jnp.log(k_f)
    f_of = lambda c: jnp.log(jnp.maximum(c, 0.5)) - logk
    done0 = jnp.where(short | zero_thr | (clo0 == k_f), 1, 0)

    def search_pass(st):
        lo, hi, clo, chi, flo, fhi, side, done = st
        frac = jnp.clip(flo / (flo - fhi), SEARCH_CLIP, 1.0 - SEARCH_CLIP)
        cand = lo + (hi - lo) * frac
        cnt = count_f(lambda sc: sc >= cand)
        ge = cnt >= k_f
        up_lo = jnp.logical_and(done == 0, ge)
        up_hi = jnp.logical_and(done == 0, jnp.logical_not(ge))
        fhi = jnp.where(up_lo & (side == 1), fhi * 0.5, fhi)
        flo = jnp.where(up_hi & (side == -1), flo * 0.5, flo)
        lo = jnp.where(up_lo, cand, lo)
        clo = jnp.where(up_lo, cnt, clo)
        flo = jnp.where(up_lo, f_of(cnt), flo)
        hi = jnp.where(up_hi, cand, hi)
        chi = jnp.where(up_hi, cnt, chi)
        fhi = jnp.where(up_hi, f_of(cnt), fhi)
        side = jnp.where(up_lo, 1, jnp.where(up_hi, -1, side))
        done = jnp.where(clo == k_f, 1, done)
        return lo, hi, clo, chi, flo, fhi, side, done

    def walk_plan(st):
        lo, _, clo, chi = st[:4]
        from_hi = k_f - chi
        from_lo = clo - k_f + 1.0
        down = jnp.logical_or(from_hi <= from_lo, lo == 0.0)
        return down, jnp.where(st[7] == 1, 0.0, jnp.where(down, from_hi, from_lo))

    most_steps = lambda st: jnp.max(walk_plan(st)[1]).astype(I32)

    st = lax.fori_loop(0, UNCHECKED_SEARCH_PASSES, lambda i, s: search_pass(s),
                       (lo0, hi0, clo0, chi0, f_of(clo0), f_of(chi0), jnp.zeros((1, TQ), I32), done0))

    def checked_pass(c):
        s = search_pass(c[2])
        return c[0] + 1, most_steps(s), s

    _, n_steps, st = lax.while_loop(
        lambda c: jnp.logical_and(c[0] < MAX_SEARCH_PASSES, c[1] > MAX_WALK_STEPS), checked_pass,
        (jnp.int32(UNCHECKED_SEARCH_PASSES), most_steps(st), st))

    def walked():
        lo, hi = st[0], st[1]
        down, steps = walk_plan(st)
        sign = jnp.where(down, 1.0, -1.0)
        bound0 = jnp.where(down, hi, _key_to_f32(_f32_to_key(-lo) + 1))

        def step(i, bound):
            def body(t, acc):
                y = sc_ref[t] * sign
                return jnp.maximum(acc, fold(jnp.where(y < bound, y, -jnp.inf), jnp.max))
            nxt = lax.fori_loop(0, n_t, body, jnp.full((SUBLANES, TQ), -jnp.inf, F32))
            return jnp.where(steps > i.astype(F32), jnp.max(nxt, axis=0, keepdims=True), bound)

        found = lax.fori_loop(0, n_steps, step, bound0) * sign
        thr = jnp.where(short, -jnp.inf,
                        jnp.where(zero_thr, 0.0, jnp.where(st[7] == 1, lo, found)))
        tie = jnp.logical_and(zero_thr, n_nonneg > k_f)
        need = jnp.where(short, 0.0, jnp.where(tie, k_f - n_pos, ALL_TIES))
        return thr, need, jnp.max(jnp.where(tie, 1, 0))

    use_bits = n_steps > MAX_WALK_STEPS
    thr, need, n_tie = lax.cond(use_bits, bisect_bits, walked)

    gw = GROUP * TQ

    def qk_dots(t, buf):
        start = pl.multiple_of(t * LT, LT)
        for kv in range(N_KV_HEADS):
            buf[:, kv * gw:(kv + 1) * gw] = jnp.dot(
                k_ref[0, kv, pl.ds(start, LT), :], qt_ref[0, 0, kv], preferred_element_type=F32)

    def attend(with_ties, thr, need):
        def attend_tile(t, buf, real, carry):
            n_eq, n_sel, ms, accs = carry
            sc = sc_ref[t]
            thr_t = jnp.where(real, thr, jnp.inf)
            if with_ties:
                eq = sc == thr_t
                pref = jnp.dot(tri_ref[...], jnp.where(eq, 1.0, 0.0).astype(BF),
                               preferred_element_type=F32)
                sel = (sc > thr_t) | (eq & (n_eq + pref - 1.0 < need))
                n_eq = n_eq + pref[LT - 1:LT, :]
            else:
                sel = (sc >= thr_t) & (sc > -jnp.inf)
            n_sel = n_sel + fold(jnp.where(sel, 1, 0).astype(I32), jnp.sum)
            bias = jnp.where(sel, 0.0, MASKED_LOGIT)
            bias = jnp.concatenate([bias] * GROUP, axis=1)
            new_m, ps, alphas = [], [], []
            for kv in range(N_KV_HEADS):
                lg = buf[:, kv * gw:(kv + 1) * gw] + bias
                m_new = jnp.maximum(ms[kv], jnp.max(lg, axis=0, keepdims=True))
                ps.append(jnp.exp2(lg - m_new).astype(BF))
                alphas.append(jnp.exp2(ms[kv] - m_new))
                new_m.append(m_new)
            new_acc = []
            for kv in range(N_KV_HEADS):
                pv = jnp.dot(vt_ref[0, t, kv * V_ROWS:(kv + 1) * V_ROWS, :], ps[kv],
                             preferred_element_type=F32)
                new_acc.append(alphas[kv] * accs[kv] + pv)
            return n_eq, n_sel, tuple(new_m), tuple(new_acc)

        def attend_pair(i, carry):
            t0 = 2 * i
            t1 = jnp.minimum(t0 + 1, last)
            qk_dots(t1, mmb_ref)
            carry = attend_tile(t0, mma_ref, True, carry)
            qk_dots(jnp.minimum(t0 + 2, last), mma_ref)
            return attend_tile(t1, mmb_ref, t0 + 1 <= last, carry)

        init = (jnp.zeros((1, TQ), F32), jnp.zeros((SUBLANES, TQ), I32),
                tuple(jnp.full((1, gw), MASKED_LOGIT, F32) for _ in range(N_KV_HEADS)),
                tuple(jnp.zeros((V_ROWS, gw), F32) for _ in range(N_KV_HEADS)))
        qk_dots(0, mma_ref)
        _, n_sel, _, accs = lax.fori_loop(0, n_pairs, attend_pair, init)
        return accs, jnp.sum(n_sel, axis=0, keepdims=True)

    accs, n_sel = lax.cond(n_tie > 0, lambda: attend(True, thr, need),
                           lambda: attend(False, thr, need))

    n_want = jnp.where(short, n_adm, k_f)
    n_miss = jnp.sum(jnp.where(n_sel.astype(F32) != n_want, 1, 0))

    def redo():
        thr_b, need_b, _ = bisect_bits()
        return attend(True, thr_b, need_b)[0]

    accs = lax.cond(jnp.logical_and(n_miss > 0, jnp.logical_not(use_bits)), redo, lambda: accs)

    for kv in range(N_KV_HEADS):
        o_t = accs[kv][:HEAD_DIM] / accs[kv][HEAD_DIM:HEAD_DIM + 1]
        for gi in range(GROUP):
            hh = GROUP * kv + gi
            bot_ref[hh * HEAD_DIM:(hh + 1) * HEAD_DIM, :] = o_t[:, gi * TQ:(gi + 1) * TQ]
    o_ref[0] = (bot_ref[...].T * jax.nn.silu(zb_ref[0])).astype(BF)


def _attn_call(qt, qit, wit, zb, k_hm, vt, ki_b, *, topk, causal, stream_len=0, n_valid=0):
    b, nq = qt.shape[0], qt.shape[1]
    s = nq * TQ
    n_tiles = vt.shape[1]
    l_all = k_hm.shape[2]
    assert l_all == n_tiles * LT
    tri = (lax.broadcasted_iota(I32, (LT, LT), 0) >= lax.broadcasted_iota(I32, (LT, LT), 1)).astype(BF)
    whole = lambda shape: pl.BlockSpec((1,) + shape, lambda i, j: (i,) + (0,) * len(shape),
                                       pipeline_mode=pl.Buffered(1))
    return pl.pallas_call(
        functools.partial(_attn_kernel, n_tiles=n_tiles, topk=topk, causal=causal,
                          stream_len=stream_len, n_valid=n_valid),
        grid=(b, nq),
        in_specs=[
            pl.BlockSpec((1, 1, N_KV_HEADS, HEAD_DIM, GROUP * TQ), lambda i, j: (i, j, 0, 0, 0)),
            pl.BlockSpec((1, 1, IDX_DIM, N_IDX_HEADS * TQ), lambda i, j: (i, j, 0, 0)),
            pl.BlockSpec((1, N_IDX_HEADS, TQ), lambda i, j: (i, 0, j)),
            pl.BlockSpec((1, TQ, ATT_WIDTH), lambda i, j: (i, j, 0)),
            whole((N_KV_HEADS, l_all, HEAD_DIM)),
            whole((n_tiles, N_KV_HEADS * V_ROWS, LT)),
            whole((l_all, IDX_DIM)),
            _const_spec((LT, LT)),
        ],
        out_specs=pl.BlockSpec((1, TQ, ATT_WIDTH), lambda i, j: (i, j, 0)),
        out_shape=jax.ShapeDtypeStruct((b, s, ATT_WIDTH), BF),
        scratch_shapes=[pltpu.VMEM((n_tiles, LT, TQ), F32), pltpu.VMEM((ATT_WIDTH, TQ), F32),
                        pltpu.VMEM((LT, N_IDX_HEADS * TQ), F32), pltpu.VMEM((LT, N_IDX_HEADS * TQ), F32)],
        compiler_params=pltpu.CompilerParams(
            dimension_semantics=("arbitrary", "arbitrary"), vmem_limit_bytes=VMEM_LIMIT),
        name="attn_prompt" if causal else "attn_sample",
    )(qt, qit, wit, zb, k_hm, vt, ki_b, tri)


def _merge_kernel(x_ref, ma_ref, gb_ref, bg_ref, wpb_ref, wout_ref, fg_ref, y_ref):
    pb = jnp.dot(bg_ref[0], wpb_ref[...], preferred_element_type=F32)
    m = ma_ref[0].astype(F32) + jax.nn.sigmoid(gb_ref[0]) * pb
    xo = x_ref[0] + jnp.dot(m.astype(BF), wout_ref[...], preferred_element_type=F32)
    ms = jnp.mean(xo * xo, axis=-1, keepdims=True)
    y_ref[0] = xo * lax.rsqrt(ms + RMS_EPS) * fg_ref[...]


def _merge_call(x, ma, gb, bg, wts, *, tt):
    b, s, _ = x.shape
    tok = lambda w: pl.BlockSpec((1, tt, w), lambda i, j: (i, j, 0))
    consts = [wts["w_pb"], wts["w_out"], wts["final_g"]]
    return pl.pallas_call(
        _merge_kernel,
        grid=(b, s // tt),
        in_specs=[tok(D_MODEL), tok(D_MODEL), tok(D_MODEL), tok(ATT_WIDTH)]
                 + [_const_spec(c.shape) for c in consts],
        out_specs=tok(D_MODEL),
        out_shape=jax.ShapeDtypeStruct((b, s, D_MODEL), F32),
        compiler_params=pltpu.CompilerParams(
            dimension_semantics=("arbitrary", "arbitrary"), vmem_limit_bytes=VMEM_LIMIT),
        name="merge",
    )(x, ma, gb, bg, *consts)


def _prep_weights(norm_g, w_in, ln_g, ln_b, w_pa, w_pb, w_out, final_g):
    o = 0
    cols = {}
    for name, width in (("u", SGU_WIDTH), ("v", SGU_WIDTH), ("za", SGU_WIDTH), ("q", ATT_WIDTH),
                        ("k", KV_WIDTH), ("vv", KV_WIDTH), ("zb", ATT_WIDTH),
                        ("qi", N_IDX_HEADS * IDX_DIM), ("ki", IDX_DIM), ("wi", N_IDX_HEADS),
                        ("ga", D_MODEL), ("gb", D_MODEL)):
        cols[name] = (o, o + width)
        o += width
    wb = w_in.astype(BF)
    sl = lambda a, z: wb[:, cols[a][0]:cols[z][1]]
    return {
        "norm_g": norm_g.reshape(1, D_MODEL), "w_uvz": sl("u", "za"),
        "w_nat": jnp.concatenate([sl("k", "zb"), sl("ki", "ki")], axis=1),
        "w_g": sl("ga", "gb"),
        "w_t": jnp.concatenate([sl("q", "q"), sl("qi", "qi"), sl("vv", "vv"), sl("wi", "wi")], axis=1).T,
        "ln_g": ln_g.reshape(1, SGU_WIDTH), "ln_b": ln_b.reshape(1, SGU_WIDTH),
        "w_pa": w_pa.astype(BF), "w_pb": w_pb.astype(BF), "w_out": w_out.astype(BF),
        "final_g": final_g.reshape(1, D_MODEL),
    }


def _mix_operands(w_mix, b_pos):
    pairs = w_mix.reshape(SGU_GROUPS // 2, 2, SGU_LEN, SGU_LEN)
    wmix = jnp.concatenate([pairs[:, 0], pairs[:, 1]], axis=-1).astype(BF)
    bmix = jnp.repeat(b_pos, SGU_GDIM, axis=1)
    return wmix, bmix


def kernel(x_prompt, x_sample, cache_k, cache_v, cache_kidx, norm_g, w_in, sgu_ln_g, sgu_ln_b,
           sgu_w, sgu_b, w_pa, w_pb, w_out, final_g):
    b, s, _ = x_prompt.shape
    db, ds, _ = x_sample.shape
    past = cache_k.shape[2]
    assert norm_g.shape[0] == 1 and db * ds == TQ and s % LT == 0
    wts = _prep_weights(norm_g[0], w_in[0], sgu_ln_g[0], sgu_ln_b[0], w_pa[0], w_pb[0], w_out[0],
                        final_g)

    pos = jnp.arange(SGU_LEN)
    chunk_ok = (pos[None, :] // CHUNK) <= (pos[:, None] // CHUNK)
    wmix_p, bmix_p = _mix_operands(jnp.where(chunk_ok[None], sgu_w[0], 0.0), sgu_b[0].T)
    (ma, gb, zb, k, v, ki, k_hm, ki_b, qt, qit, wit, vt) = _proj_call(
        x_prompt, wts, wmix_p, bmix_p, tt=4 * TQ, lt=LT, emit_vn=False)
    bg = _attn_call(qt, qit, wit, zb, k_hm, vt, ki_b, topk=min(TOPK_MAX, s // 4), causal=True)
    y_prompt = _merge_call(x_prompt, ma, gb, bg, wts, tt=512)

    assert past % CHUNK == 0 and ds <= CHUNK
    w_blk = jnp.einsum("ab,gij->gaibj", jnp.eye(db, dtype=F32), sgu_w[0][:, :ds, :ds])
    wmix_s, bmix_s = _mix_operands(w_blk.reshape(SGU_GROUPS, SGU_LEN, SGU_LEN),
                                   jnp.tile(sgu_b[0][:, :ds].T, (db, 1)))
    n_tok = db * ds
    (ma_s, gb_s, zb_s, k_s, v_s, ki_s, _, _, qt_s, qit_s, wit_s, _, vn_s) = _proj_call(
        x_sample.reshape(1, n_tok, D_MODEL), wts, wmix_s, bmix_s, tt=n_tok, lt=n_tok, emit_vn=True)
    n_keys = past + ds
    slab = -(-n_keys * db // LT) * LT // db
    assert (slab * db) % LT == 0 and slab >= n_keys
    cat = lambda c, n, w: jnp.pad(
        jnp.concatenate([c.reshape(db, past, w), n.reshape(db, ds, w)], axis=1),
        ((0, 0), (0, slab - n_keys), (0, 0))).reshape(db * slab, w).astype(BF)
    k_all, v_all, ki_all = (cat(cache_k[0], k_s, KV_WIDTH), cat(cache_v[0], v_s, KV_WIDTH),
                            cat(cache_kidx[0], ki_s, IDX_DIM))
    n_t_s = db * slab // LT
    k_hm_s = k_all.reshape(1, db * slab, N_KV_HEADS, HEAD_DIM).transpose(0, 2, 1, 3)
    vt_s = v_all.reshape(n_t_s, LT, N_KV_HEADS, HEAD_DIM).transpose(0, 2, 3, 1)
    vt_s = jnp.concatenate([vt_s, jnp.ones((n_t_s, N_KV_HEADS, BF16_ROWS, LT), BF)], axis=2)
    vt_s = vt_s.reshape(1, n_t_s, N_KV_HEADS * V_ROWS, LT)
    bg_s = _attn_call(qt_s, qit_s, wit_s, zb_s, k_hm_s, vt_s, ki_all[None],
                      topk=min(TOPK_MAX, n_keys // 4), causal=False, stream_len=ds, n_valid=n_keys)
    y_sample = _merge_call(x_sample.reshape(1, n_tok, D_MODEL), ma_s, gb_s, bg_s, wts, tt=n_tok)

    return (y_prompt, y_sample.reshape(db, ds, D_MODEL),
            k.reshape(1, b, s, N_KV_HEADS, HEAD_DIM), v.reshape(1, b, s, N_KV_HEADS, HEAD_DIM),
            ki.reshape(1, b, s, IDX_DIM),
            k_s.reshape(1, db, ds, N_KV_HEADS, HEAD_DIM), v_s.reshape(1, db, ds, N_KV_HEADS, HEAD_DIM),
            ki_s.reshape(1, db, ds, IDX_DIM), vn_s.reshape(1, db, ds, SGU_WIDTH))
```

```python
import functools

import jax
import jax.numpy as jnp
from jax import lax
from jax.experimental import pallas as pl
from jax.experimental.pallas import tpu as pltpu

D_MODEL = 1024
CHUNK = 64
SGU_LEN = 128
SGU_GROUPS = 8
SGU_WIDTH = 512
SGU_GDIM = SGU_WIDTH // SGU_GROUPS
N_HEADS = 8
N_KV_HEADS = 4
HEAD_DIM = 64
GROUP = N_HEADS // N_KV_HEADS
ATT_WIDTH = N_HEADS * HEAD_DIM
KV_WIDTH = N_KV_HEADS * HEAD_DIM
N_IDX_HEADS = 8
IDX_DIM = 64
TOPK_MAX = 256
RMS_EPS = 1e-6
LN_EPS = 1e-5

BF = jnp.bfloat16
F32 = jnp.float32
I32 = jnp.int32

LANES = 128
SUBLANES = 8
BF16_ROWS = 16
TQ = LANES
LT = 512
V_ROWS = HEAD_DIM + BF16_ROWS
VMEM_LIMIT = 52 * 1024 * 1024
MASKED_LOGIT = -1e30
LOWEST = -3.4028235e38
ALL_TIES = 3e38
UNCHECKED_SEARCH_PASSES = 8
MAX_SEARCH_PASSES = 28
MAX_WALK_STEPS = 8
SEARCH_CLIP = 0.05
LOG2_E = 1.4426950408889634
INT_MIN = -2147483648


def _const_spec(shape):
    nd = len(shape)
    return pl.BlockSpec(shape, lambda *_: (0,) * nd, pipeline_mode=pl.Buffered(1))


def _proj_kernel(x_ref, ng_ref, wuvz_ref, wnat_ref, wg_ref, wt_ref,
                 lng_ref, lnb_ref, wmix_ref, bmix_ref, wpa_ref,
                 ma_ref, gb_ref, zb_ref, k_ref, v_ref, ki_ref, khm_ref, kib_ref,
                 qt_ref, qit_ref, wit_ref, vt_ref, *vn_refs, n_sub):
    x = x_ref[0]
    ms = jnp.mean(x * x, axis=-1, keepdims=True)
    h = (x * lax.rsqrt(ms + RMS_EPS) * ng_ref[...]).astype(BF)

    uvz = jnp.dot(h, wuvz_ref[...], preferred_element_type=F32)
    u = jax.nn.gelu(uvz[:, :SGU_WIDTH])
    v = jax.nn.gelu(uvz[:, SGU_WIDTH:2 * SGU_WIDTH])
    za = uvz[:, 2 * SGU_WIDTH:]
    mu = jnp.mean(v, axis=-1, keepdims=True)
    var = jnp.mean(jnp.square(v - mu), axis=-1, keepdims=True)
    vn = (v - mu) * lax.rsqrt(var + LN_EPS) * lng_ref[...] + lnb_ref[...]
    if vn_refs:
        vn_refs[0][0] = vn
    lane = lax.broadcasted_iota(I32, (SGU_LEN, LANES), 1)
    cols = []
    for c in range(SGU_WIDTH // LANES):
        rows = []
        for s in range(n_sub):
            blk = vn[s * SGU_LEN:(s + 1) * SGU_LEN, c * LANES:(c + 1) * LANES]
            lo = jnp.where(lane < SGU_GDIM, blk, 0.0).astype(BF)
            hi = jnp.where(lane >= SGU_GDIM, blk, 0.0).astype(BF)
            rhs = jnp.concatenate([lo, hi], axis=0)
            rows.append(jnp.dot(wmix_ref[c], rhs, preferred_element_type=F32))
        cols.append(jnp.concatenate(rows, axis=0) if n_sub > 1 else rows[0])
    mixed = jnp.concatenate(cols, axis=1)
    bias = bmix_ref[...]
    if n_sub > 1:
        bias = jnp.concatenate([bias] * n_sub, axis=0)
    a_out = u * (mixed + bias) * jax.nn.silu(za)
    pa = jnp.dot(a_out.astype(BF), wpa_ref[...], preferred_element_type=F32)
    g = jnp.dot(h, wg_ref[...], preferred_element_type=F32)
    ma_ref[0] = (jax.nn.sigmoid(g[:, :D_MODEL]) * pa).astype(BF)
    gb_ref[0] = g[:, D_MODEL:]

    nat = jnp.dot(h, wnat_ref[...], preferred_element_type=F32)
    k = nat[:, :KV_WIDTH]
    k_ref[0] = k
    v_ref[0] = nat[:, KV_WIDTH:2 * KV_WIDTH]
    zb_ref[0] = nat[:, 2 * KV_WIDTH:2 * KV_WIDTH + ATT_WIDTH]
    ki = nat[:, 2 * KV_WIDTH + ATT_WIDTH:]
    ki_ref[0] = ki
    kib_ref[0] = ki.astype(BF)
    for kv in range(N_KV_HEADS):
        khm_ref[0, kv] = k[:, kv * HEAD_DIM:(kv + 1) * HEAD_DIM].astype(BF)

    tm = lax.dot_general(wt_ref[...], h, (((1,), (1,)), ((), ())),
                         preferred_element_type=F32)
    o_qi, o_v, o_wi = ATT_WIDTH, 2 * ATT_WIDTH, 2 * ATT_WIDTH + KV_WIDTH
    for jb in range(n_sub):
        tok = slice(jb * TQ, (jb + 1) * TQ)
        for hh in range(N_HEADS):
            kv, gi = divmod(hh, GROUP)
            qt_ref[0, jb, kv, :, gi * TQ:(gi + 1) * TQ] = (
                tm[hh * HEAD_DIM:(hh + 1) * HEAD_DIM, tok] * (HEAD_DIM ** -0.5 * LOG2_E)).astype(BF)
        for hh in range(N_IDX_HEADS):
            qit_ref[0, jb, :, hh * TQ:(hh + 1) * TQ] = (
                tm[o_qi + hh * IDX_DIM:o_qi + (hh + 1) * IDX_DIM, tok] * (IDX_DIM ** -0.5)).astype(BF)
    for kv in range(N_KV_HEADS):
        vt_ref[0, 0, kv * V_ROWS:kv * V_ROWS + HEAD_DIM, :] = (
            tm[o_v + kv * HEAD_DIM:o_v + (kv + 1) * HEAD_DIM, :].astype(BF))
        vt_ref[0, 0, kv * V_ROWS + HEAD_DIM:(kv + 1) * V_ROWS, :] = jnp.ones(
            (BF16_ROWS, tm.shape[1]), BF)
    wit_ref[0] = tm[o_wi:o_wi + N_IDX_HEADS, :] * (N_IDX_HEADS ** -0.5)


def _proj_call(x, wts, wmix, bmix, *, tt, lt, emit_vn):
    b, s, _ = x.shape
    n_sub = tt // SGU_LEN
    r = lt // tt
    grid = (b, s // tt)
    tok = lambda w, dt=F32: (jax.ShapeDtypeStruct((b, s, w), dt),
                             pl.BlockSpec((1, tt, w), lambda i, j: (i, j, 0)))
    outs = [
        tok(D_MODEL, BF),
        tok(D_MODEL),
        tok(ATT_WIDTH),
        tok(KV_WIDTH), tok(KV_WIDTH), tok(IDX_DIM),
        (jax.ShapeDtypeStruct((b, N_KV_HEADS, s, HEAD_DIM), BF),
         pl.BlockSpec((1, N_KV_HEADS, tt, HEAD_DIM), lambda i, j: (i, 0, j, 0))),
        tok(IDX_DIM, BF),
        (jax.ShapeDtypeStruct((b, s // TQ, N_KV_HEADS, HEAD_DIM, GROUP * TQ), BF),
         pl.BlockSpec((1, n_sub, N_KV_HEADS, HEAD_DIM, GROUP * TQ), lambda i, j: (i, j, 0, 0, 0))),
        (jax.ShapeDtypeStruct((b, s // TQ, IDX_DIM, N_IDX_HEADS * TQ), BF),
         pl.BlockSpec((1, n_sub, IDX_DIM, N_IDX_HEADS * TQ), lambda i, j: (i, j, 0, 0))),
        (jax.ShapeDtypeStruct((b, N_IDX_HEADS, s), F32),
         pl.BlockSpec((1, N_IDX_HEADS, tt), lambda i, j: (i, 0, j))),
        (jax.ShapeDtypeStruct((b, s // lt, N_KV_HEADS * V_ROWS, lt), BF),
         pl.BlockSpec((1, 1, N_KV_HEADS * V_ROWS, tt), lambda i, j: (i, j // r, 0, j % r))),
    ]
    if emit_vn:
        outs.append(tok(SGU_WIDTH))
    consts = [wts["norm_g"], wts["w_uvz"], wts["w_nat"], wts["w_g"], wts["w_t"],
              wts["ln_g"], wts["ln_b"], wmix, bmix, wts["w_pa"]]
    return pl.pallas_call(
        functools.partial(_proj_kernel, n_sub=n_sub),
        grid=grid,
        in_specs=[pl.BlockSpec((1, tt, D_MODEL), lambda i, j: (i, j, 0))]
                 + [_const_spec(c.shape) for c in consts],
        out_specs=[o[1] for o in outs],
        out_shape=[o[0] for o in outs],
        compiler_params=pltpu.CompilerParams(
            dimension_semantics=("arbitrary", "arbitrary"), vmem_limit_bytes=VMEM_LIMIT),
        name="proj_vn" if emit_vn else "proj",
    )(x, *consts)


def _key_to_f32(key_u):
    s = key_u ^ INT_MIN
    bits = s ^ (lax.shift_right_arithmetic(s, 31) & 0x7FFFFFFF)
    return lax.bitcast_convert_type(bits, F32)


def _f32_to_key(x):
    bits = lax.bitcast_convert_type(x, I32)
    return bits ^ (lax.shift_right_arithmetic(bits, 31) & 0x7FFFFFFF) ^ INT_MIN


def _attn_kernel(qt_ref, qit_ref, wit_ref, zb_ref, k_ref, vt_ref, ki_ref, tri_ref,
                 o_ref, sc_ref, bot_ref, mma_ref, mmb_ref, *, n_tiles, topk, causal, stream_len,
                 n_valid):
    lane = lax.broadcasted_iota(I32, (1, TQ), 1)
    if causal:
        q0 = pl.program_id(1) * TQ
        n_t = lax.div(q0 + TQ + LT - 1, LT)
        key_lo = jnp.zeros((1, TQ), I32)
        key_hi = (lax.shift_right_logical(q0 + lane, 6) + 1) * CHUNK
    else:
        n_t = n_tiles
        slab = (n_tiles * LT) // (TQ // stream_len)
        key_lo = lax.div(lane, stream_len) * slab
        key_hi = key_lo + n_valid

    qit = qit_ref[0, 0]
    wit = wit_ref[0]
    wrows = [wit[hh:hh + 1, :] for hh in range(N_IDX_HEADS)]
    row0 = lax.broadcasted_iota(I32, (LT, 1), 0)

    fold = lambda x, op: op(x.reshape(LT // SUBLANES, SUBLANES, TQ), axis=0)

    last = n_t - 1
    n_pairs = lax.div(n_t + 1, 2)

    def idx_dot(t, buf):
        start = pl.multiple_of(t * LT, LT)
        buf[...] = jnp.dot(ki_ref[0, pl.ds(start, LT), :], qit, preferred_element_type=F32)

    def score_tile(t, buf, real, carry):
        mx, mn, n_nonneg, n_pos = carry
        sc = jnp.zeros((LT, TQ), F32)
        for hh in range(N_IDX_HEADS):
            sc = sc + jnp.maximum(buf[:, hh * TQ:(hh + 1) * TQ], 0.0) * wrows[hh]
        pos = row0 + t * LT
        ok = pos < key_hi if causal else (pos >= key_lo) & (pos < key_hi)
        mn = jnp.minimum(mn, fold(sc, jnp.min))
        sc = jnp.where(ok, sc, -jnp.inf)
        sc_ref[t] = sc
        ones = lambda m: jnp.where(real, fold(jnp.where(m, 1, 0).astype(I32), jnp.sum), 0)
        return (jnp.maximum(mx, fold(sc, jnp.max)), mn,
                n_nonneg + ones(sc >= 0.0), n_pos + ones(sc > 0.0))

    def score_pair(i, carry):
        t0 = 2 * i
        t1 = jnp.minimum(t0 + 1, last)
        idx_dot(t1, mmb_ref)
        carry = score_tile(t0, mma_ref, True, carry)
        idx_dot(jnp.minimum(t0 + 2, last), mma_ref)
        return score_tile(t1, mmb_ref, t0 + 1 <= last, carry)

    idx_dot(0, mma_ref)
    mx, mn, n_nonneg, n_pos = lax.fori_loop(
        0, n_pairs, score_pair,
        (jnp.full((SUBLANES, TQ), -jnp.inf, F32), jnp.full((SUBLANES, TQ), jnp.inf, F32),
         jnp.zeros((SUBLANES, TQ), I32), jnp.zeros((SUBLANES, TQ), I32)))
    mx = jnp.max(mx, axis=0, keepdims=True)
    mn = jnp.min(mn, axis=0, keepdims=True)
    n_nonneg = jnp.sum(n_nonneg, axis=0, keepdims=True).astype(F32)
    n_pos = jnp.sum(n_pos, axis=0, keepdims=True).astype(F32)

    def count(pred):
        ones = lambda t: fold(jnp.where(pred(sc_ref[t]), 1, 0).astype(I32), jnp.sum)

        def body(i, acc):
            t0 = 2 * i
            return acc + ones(t0) + jnp.where(t0 + 1 <= last, ones(jnp.minimum(t0 + 1, last)), 0)
        acc = lax.fori_loop(0, n_pairs, body, jnp.zeros((SUBLANES, TQ), I32))
        return jnp.sum(acc, axis=0, keepdims=True)

    k_f = float(topk)
    n_adm = (key_hi - key_lo).astype(F32)
    short = n_adm < k_f

    def bisect_bits():
        def bit_pass(i, carry):
            thr_u, n_ge = carry
            cand_u = thr_u | lax.shift_left(jnp.int32(1), 31 - i)
            cand = _key_to_f32(cand_u)
            cnt = count(lambda sc: sc >= cand)
            take = cnt >= topk
            return jnp.where(take, cand_u, thr_u), jnp.where(take, cnt, n_ge)

        thr_u, n_ge = lax.fori_loop(
            0, 32, bit_pass, (jnp.zeros((1, TQ), I32), jnp.zeros((1, TQ), I32)))
        thr = jnp.where(short, LOWEST, _key_to_f32(thr_u))
        n_gt = count(lambda sc: sc > thr)
        need = jnp.where(short, ALL_TIES, (topk - n_gt).astype(F32))
        tie = jnp.logical_and(jnp.logical_not(short), n_ge > topk)
        return thr, need, jnp.max(jnp.where(tie, 1, 0))

    count_f = lambda pred: count(pred).astype(F32)
    zero_thr = jnp.logical_and(n_pos < k_f, n_nonneg >= k_f)
    pos_side = n_pos >= k_f
    lo0 = jnp.where(pos_side, 0.0, mn)
    hi0 = jnp.where(pos_side, mx, 0.0)
    clo0 = jnp.where(pos_side, n_nonneg, n_adm)
    chi0 = jnp.where(pos_side, 0.0, n_nonneg)
    logk = jnp.log(k_f)
    f_of = lambda c: jnp.log(jnp.maximum(c, 0.5)) - logk
    done0 = jnp.where(short | zero_thr | (clo0 == k_f), 1, 0)

    def search_pass(st):
        lo, hi, clo, chi, flo, fhi, side, done = st
        frac = jnp.clip(flo / (flo - fhi), SEARCH_CLIP, 1.0 - SEARCH_CLIP)
        cand = lo + (hi - lo) * frac
        cnt = count_f(lambda sc: sc >= cand)
        ge = cnt >= k_f
        up_lo = jnp.logical_and(done == 0, ge)
        up_hi = jnp.logical_and(done == 0, jnp.logical_not(ge))
        fhi = jnp.where(up_lo & (side == 1), fhi * 0.5, fhi)
        flo = jnp.where(up_hi & (side == -1), flo * 0.5, flo)
        lo = jnp.where(up_lo, cand, lo)
        clo = jnp.where(up_lo, cnt, clo)
        flo = jnp.where(up_lo, f_of(cnt), flo)
        hi = jnp.where(up_hi, cand, hi)
        chi = jnp.where(up_hi, cnt, chi)
        fhi = jnp.where(up_hi, f_of(cnt), fhi)
        side = jnp.where(up_lo, 1, jnp.where(up_hi, -1, side))
        done = jnp.where(clo == k_f, 1, done)
        return lo, hi, clo, chi, flo, fhi, side, done

    def walk_plan(st):
        lo, _, clo, chi = st[:4]
        from_hi = k_f - chi
        from_lo = clo - k_f + 1.0
        down = jnp.logical_or(from_hi <= from_lo, lo == 0.0)
        return down, jnp.where(st[7] == 1, 0.0, jnp.where(down, from_hi, from_lo))

    most_steps = lambda st: jnp.max(walk_plan(st)[1]).astype(I32)

    st = lax.fori_loop(0, UNCHECKED_SEARCH_PASSES, lambda i, s: search_pass(s),
                       (lo0, hi0, clo0, chi0, f_of(clo0), f_of(chi0), jnp.zeros((1, TQ), I32), done0))

    def checked_pass(c):
        s = search_pass(c[2])
        return c[0] + 1, most_steps(s), s

    _, n_steps, st = lax.while_loop(
        lambda c: jnp.logical_and(c[0] < MAX_SEARCH_PASSES, c[1] > MAX_WALK_STEPS), checked_pass,
        (jnp.int32(UNCHECKED_SEARCH_PASSES), most_steps(st), st))

    def walked():
        lo, hi = st[0], st[1]
        down, steps = walk_plan(st)
        sign = jnp.where(down, 1.0, -1.0)
        bound0 = jnp.where(down, hi, _key_to_f32(_f32_to_key(-lo) + 1))

        def step(i, bound):
            def below(t):
                y = sc_ref[t] * sign
                return fold(jnp.where(y < bound, y, -jnp.inf), jnp.max)

            def body(j, acc):
                return jnp.maximum(acc, jnp.maximum(below(2 * j), below(jnp.minimum(2 * j + 1, last))))
            nxt = lax.fori_loop(0, n_pairs, body, jnp.full((SUBLANES, TQ), -jnp.inf, F32))
            return jnp.where(steps > i.astype(F32), jnp.max(nxt, axis=0, keepdims=True), bound)

        found = lax.fori_loop(0, n_steps, step, bound0) * sign
        thr = jnp.where(short, LOWEST,
                        jnp.where(zero_thr, 0.0, jnp.where(st[7] == 1, lo, found)))
        tie = jnp.logical_and(zero_thr, n_nonneg > k_f)
        need = jnp.where(tie, k_f - n_pos, ALL_TIES)
        return thr, need, jnp.max(jnp.where(tie, 1, 0))

    use_bits = n_steps > MAX_WALK_STEPS
    thr, need, n_tie = lax.cond(use_bits, bisect_bits, walked)

    gw = GROUP * TQ

    def qk_dots(t, buf):
        start = pl.multiple_of(t * LT, LT)
        for kv in range(N_KV_HEADS):
            buf[:, kv * gw:(kv + 1) * gw] = jnp.dot(
                k_ref[0, kv, pl.ds(start, LT), :], qt_ref[0, 0, kv], preferred_element_type=F32)

    def attend(with_ties, thr, need):
        def attend_tile(t, buf, real, carry):
            n_eq, n_sel, ms, accs = carry
            sc = sc_ref[t]
            thr_t = jnp.where(real, thr, jnp.inf)
            if with_ties:
                eq = sc == thr_t
                pref = jnp.dot(tri_ref[...], jnp.where(eq, 1.0, 0.0).astype(BF),
                               preferred_element_type=F32)
                sel = (sc > thr_t) | (eq & (n_eq + pref - 1.0 < need))
                n_eq = n_eq + pref[LT - 1:LT, :]
            else:
                sel = sc >= thr_t
            n_sel = n_sel + fold(jnp.where(sel, 1, 0).astype(I32), jnp.sum)
            bias = jnp.where(sel, 0.0, MASKED_LOGIT)
            bias = jnp.concatenate([bias] * GROUP, axis=1)
            new_m, ps, alphas = [], [], []
            for kv in range(N_KV_HEADS):
                lg = buf[:, kv * gw:(kv + 1) * gw] + bias
                m_new = jnp.maximum(ms[kv], jnp.max(lg, axis=0, keepdims=True))
                ps.append(jnp.exp2(lg - m_new).astype(BF))
                alphas.append(jnp.exp2(ms[kv] - m_new))
                new_m.append(m_new)
            new_acc = []
            for kv in range(N_KV_HEADS):
                pv = jnp.dot(vt_ref[0, t, kv * V_ROWS:(kv + 1) * V_ROWS, :], ps[kv],
                             preferred_element_type=F32)
                new_acc.append(alphas[kv] * accs[kv] + pv)
            return n_eq, n_sel, tuple(new_m), tuple(new_acc)

        def attend_pair(i, carry):
            t0 = 2 * i
            t1 = jnp.minimum(t0 + 1, last)
            qk_dots(t1, mmb_ref)
            carry = attend_tile(t0, mma_ref, True, carry)
            qk_dots(jnp.minimum(t0 + 2, last), mma_ref)
            return attend_tile(t1, mmb_ref, t0 + 1 <= last, carry)

        init = (jnp.zeros((1, TQ), F32), jnp.zeros((SUBLANES, TQ), I32),
                tuple(jnp.full((1, gw), MASKED_LOGIT, F32) for _ in range(N_KV_HEADS)),
                tuple(jnp.zeros((V_ROWS, gw), F32) for _ in range(N_KV_HEADS)))
        qk_dots(0, mma_ref)
        _, n_sel, _, accs = lax.fori_loop(0, n_pairs, attend_pair, init)
        return accs, jnp.sum(n_sel, axis=0, keepdims=True)

    accs, n_sel = lax.cond(n_tie > 0, lambda: attend(True, thr, need),
                           lambda: attend(False, thr, need))

    n_want = jnp.where(short, n_adm, k_f)
    n_miss = jnp.sum(jnp.where(n_sel.astype(F32) != n_want, 1, 0))

    def redo():
        thr_b, need_b, _ = bisect_bits()
        return attend(True, thr_b, need_b)[0]

    accs = lax.cond(jnp.logical_and(n_miss > 0, jnp.logical_not(use_bits)), redo, lambda: accs)

    for kv in range(N_KV_HEADS):
        o_t = accs[kv][:HEAD_DIM] / accs[kv][HEAD_DIM:HEAD_DIM + 1]
        for gi in range(GROUP):
            hh = GROUP * kv + gi
            bot_ref[hh * HEAD_DIM:(hh + 1) * HEAD_DIM, :] = o_t[:, gi * TQ:(gi + 1) * TQ]
    o_ref[0] = (bot_ref[...].T * jax.nn.silu(zb_ref[0])).astype(BF)


def _attn_call(qt, qit, wit, zb, k_hm, vt, ki_b, *, topk, causal, stream_len=0, n_valid=0):
    b, nq = qt.shape[0], qt.shape[1]
    s = nq * TQ
    n_tiles = vt.shape[1]
    l_all = k_hm.shape[2]
    assert l_all == n_tiles * LT
    tri = (lax.broadcasted_iota(I32, (LT, LT), 0) >= lax.broadcasted_iota(I32, (LT, LT), 1)).astype(BF)
    whole = lambda shape: pl.BlockSpec((1,) + shape, lambda i, j: (i,) + (0,) * len(shape),
                                       pipeline_mode=pl.Buffered(1))
    return pl.pallas_call(
        functools.partial(_attn_kernel, n_tiles=n_tiles, topk=topk, causal=causal,
                          stream_len=stream_len, n_valid=n_valid),
        grid=(b, nq),
        in_specs=[
            pl.BlockSpec((1, 1, N_KV_HEADS, HEAD_DIM, GROUP * TQ), lambda i, j: (i, j, 0, 0, 0)),
            pl.BlockSpec((1, 1, IDX_DIM, N_IDX_HEADS * TQ), lambda i, j: (i, j, 0, 0)),
            pl.BlockSpec((1, N_IDX_HEADS, TQ), lambda i, j: (i, 0, j)),
            pl.BlockSpec((1, TQ, ATT_WIDTH), lambda i, j: (i, j, 0)),
            whole((N_KV_HEADS, l_all, HEAD_DIM)),
            whole((n_tiles, N_KV_HEADS * V_ROWS, LT)),
            whole((l_all, IDX_DIM)),
            _const_spec((LT, LT)),
        ],
        out_specs=pl.BlockSpec((1, TQ, ATT_WIDTH), lambda i, j: (i, j, 0)),
        out_shape=jax.ShapeDtypeStruct((b, s, ATT_WIDTH), BF),
        scratch_shapes=[pltpu.VMEM((n_tiles, LT, TQ), F32), pltpu.VMEM((ATT_WIDTH, TQ), F32),
                        pltpu.VMEM((LT, N_IDX_HEADS * TQ), F32), pltpu.VMEM((LT, N_IDX_HEADS * TQ), F32)],
        compiler_params=pltpu.CompilerParams(
            dimension_semantics=("arbitrary", "arbitrary"), vmem_limit_bytes=VMEM_LIMIT),
        name="attn_prompt" if causal else "attn_sample",
    )(qt, qit, wit, zb, k_hm, vt, ki_b, tri)


def _merge_kernel(x_ref, ma_ref, gb_ref, bg_ref, wpb_ref, wout_ref, fg_ref, y_ref):
    pb = jnp.dot(bg_ref[0], wpb_ref[...], preferred_element_type=F32)
    m = ma_ref[0].astype(F32) + jax.nn.sigmoid(gb_ref[0]) * pb
    xo = x_ref[0] + jnp.dot(m.astype(BF), wout_ref[...], preferred_element_type=F32)
    ms = jnp.mean(xo * xo, axis=-1, keepdims=True)
    y_ref[0] = xo * lax.rsqrt(ms + RMS_EPS) * fg_ref[...]


def _merge_call(x, ma, gb, bg, wts, *, tt):
    b, s, _ = x.shape
    tok = lambda w: pl.BlockSpec((1, tt, w), lambda i, j: (i, j, 0))
    consts = [wts["w_pb"], wts["w_out"], wts["final_g"]]
    return pl.pallas_call(
        _merge_kernel,
        grid=(b, s // tt),
        in_specs=[tok(D_MODEL), tok(D_MODEL), tok(D_MODEL), tok(ATT_WIDTH)]
                 + [_const_spec(c.shape) for c in consts],
        out_specs=tok(D_MODEL),
        out_shape=jax.ShapeDtypeStruct((b, s, D_MODEL), F32),
        compiler_params=pltpu.CompilerParams(
            dimension_semantics=("arbitrary", "arbitrary"), vmem_limit_bytes=VMEM_LIMIT),
        name="merge",
    )(x, ma, gb, bg, *consts)


def _prep_weights(norm_g, w_in, ln_g, ln_b, w_pa, w_pb, w_out, final_g):
    o = 0
    cols = {}
    for name, width in (("u", SGU_WIDTH), ("v", SGU_WIDTH), ("za", SGU_WIDTH), ("q", ATT_WIDTH),
                        ("k", KV_WIDTH), ("vv", KV_WIDTH), ("zb", ATT_WIDTH),
                        ("qi", N_IDX_HEADS * IDX_DIM), ("ki", IDX_DIM), ("wi", N_IDX_HEADS),
                        ("ga", D_MODEL), ("gb", D_MODEL)):
        cols[name] = (o, o + width)
        o += width
    wb = w_in.astype(BF)
    sl = lambda a, z: wb[:, cols[a][0]:cols[z][1]]
    return {
        "norm_g": norm_g.reshape(1, D_MODEL), "w_uvz": sl("u", "za"),
        "w_nat": jnp.concatenate([sl("k", "zb"), sl("ki", "ki")], axis=1),
        "w_g": sl("ga", "gb"),
        "w_t": jnp.concatenate([sl("q", "q"), sl("qi", "qi"), sl("vv", "vv"), sl("wi", "wi")], axis=1).T,
        "ln_g": ln_g.reshape(1, SGU_WIDTH), "ln_b": ln_b.reshape(1, SGU_WIDTH),
        "w_pa": w_pa.astype(BF), "w_pb": w_pb.astype(BF), "w_out": w_out.astype(BF),
        "final_g": final_g.reshape(1, D_MODEL),
    }


def _mix_operands(w_mix, b_pos):
    pairs = w_mix.reshape(SGU_GROUPS // 2, 2, SGU_LEN, SGU_LEN)
    wmix = jnp.concatenate([pairs[:, 0], pairs[:, 1]], axis=-1).astype(BF)
    bmix = jnp.repeat(b_pos, SGU_GDIM, axis=1)
    return wmix, bmix


def kernel(x_prompt, x_sample, cache_k, cache_v, cache_kidx, norm_g, w_in, sgu_ln_g, sgu_ln_b,
           sgu_w, sgu_b, w_pa, w_pb, w_out, final_g):
    b, s, _ = x_prompt.shape
    db, ds, _ = x_sample.shape
    past = cache_k.shape[2]
    assert norm_g.shape[0] == 1 and db * ds == TQ and s % LT == 0
    wts = _prep_weights(norm_g[0], w_in[0], sgu_ln_g[0], sgu_ln_b[0], w_pa[0], w_pb[0], w_out[0],
                        final_g)

    pos = jnp.arange(SGU_LEN)
    chunk_ok = (pos[None, :] // CHUNK) <= (pos[:, None] // CHUNK)
    wmix_p, bmix_p = _mix_operands(jnp.where(chunk_ok[None], sgu_w[0], 0.0), sgu_b[0].T)
    (ma, gb, zb, k, v, ki, k_hm, ki_b, qt, qit, wit, vt) = _proj_call(
        x_prompt, wts, wmix_p, bmix_p, tt=4 * TQ, lt=LT, emit_vn=False)
    bg = _attn_call(qt, qit, wit, zb, k_hm, vt, ki_b, topk=min(TOPK_MAX, s // 4), causal=True)
    y_prompt = _merge_call(x_prompt, ma, gb, bg, wts, tt=512)

    assert past % CHUNK == 0 and ds <= CHUNK
    w_blk = jnp.einsum("ab,gij->gaibj", jnp.eye(db, dtype=F32), sgu_w[0][:, :ds, :ds])
    wmix_s, bmix_s = _mix_operands(w_blk.reshape(SGU_GROUPS, SGU_LEN, SGU_LEN),
                                   jnp.tile(sgu_b[0][:, :ds].T, (db, 1)))
    n_tok = db * ds
    (ma_s, gb_s, zb_s, k_s, v_s, ki_s, _, _, qt_s, qit_s, wit_s, _, vn_s) = _proj_call(
        x_sample.reshape(1, n_tok, D_MODEL), wts, wmix_s, bmix_s, tt=n_tok, lt=n_tok, emit_vn=True)
    n_keys = past + ds
    slab = -(-n_keys * db // LT) * LT // db
    assert (slab * db) % LT == 0 and slab >= n_keys
    cat = lambda c, n, w: jnp.pad(
        jnp.concatenate([c.reshape(db, past, w), n.reshape(db, ds, w)], axis=1),
        ((0, 0), (0, slab - n_keys), (0, 0))).reshape(db * slab, w).astype(BF)
    k_all, v_all, ki_all = (cat(cache_k[0], k_s, KV_WIDTH), cat(cache_v[0], v_s, KV_WIDTH),
                            cat(cache_kidx[0], ki_s, IDX_DIM))
    n_t_s = db * slab // LT
    k_hm_s = k_all.reshape(1, db * slab, N_KV_HEADS, HEAD_DIM).transpose(0, 2, 1, 3)
    vt_s = v_all.reshape(n_t_s, LT, N_KV_HEADS, HEAD_DIM).transpose(0, 2, 3, 1)
    vt_s = jnp.concatenate([vt_s, jnp.ones((n_t_s, N_KV_HEADS, BF16_ROWS, LT), BF)], axis=2)
    vt_s = vt_s.reshape(1, n_t_s, N_KV_HEADS * V_ROWS, LT)
    bg_s = _attn_call(qt_s, qit_s, wit_s, zb_s, k_hm_s, vt_s, ki_all[None],
                      topk=min(TOPK_MAX, n_keys // 4), causal=False, stream_len=ds, n_valid=n_keys)
    y_sample = _merge_call(x_sample.reshape(1, n_tok, D_MODEL), ma_s, gb_s, bg_s, wts, tt=n_tok)

    return (y_prompt, y_sample.reshape(db, ds, D_MODEL),
            k.reshape(1, b, s, N_KV_HEADS, HEAD_DIM), v.reshape(1, b, s, N_KV_HEADS, HEAD_DIM),
            ki.reshape(1, b, s, IDX_DIM),
            k_s.reshape(1, db, ds, N_KV_HEADS, HEAD_DIM), v_s.reshape(1, db, ds, N_KV_HEADS, HEAD_DIM),
            ki_s.reshape(1, db, ds, IDX_DIM), vn_s.reshape(1, db, ds, SGU_WIDTH))
```

```python
import functools

import jax
import jax.numpy as jnp
from jax import lax
from jax.experimental import pallas as pl
from jax.experimental.pallas import tpu as pltpu

D_MODEL = 1024
CHUNK = 64
SGU_LEN = 128
SGU_GROUPS = 8
SGU_WIDTH = 512
SGU_GDIM = SGU_WIDTH // SGU_GROUPS
N_HEADS = 8
N_KV_HEADS = 4
HEAD_DIM = 64
GROUP = N_HEADS // N_KV_HEADS
ATT_WIDTH = N_HEADS * HEAD_DIM
KV_WIDTH = N_KV_HEADS * HEAD_DIM
N_IDX_HEADS = 8
IDX_DIM = 64
TOPK_MAX = 256
RMS_EPS = 1e-6
LN_EPS = 1e-5

BF = jnp.bfloat16
F32 = jnp.float32
I32 = jnp.int32

LANES = 128
SUBLANES = 8
BF16_ROWS = 16
TQ = LANES
LT = 512
V_ROWS = HEAD_DIM + BF16_ROWS
VMEM_LIMIT = 52 * 1024 * 1024
MASKED_LOGIT = -1e30
LOWEST = -3.4028235e38
ALL_TIES = 3e38
UNCHECKED_SEARCH_PASSES = 8
MAX_SEARCH_PASSES = 28
MAX_WALK_STEPS = 8
SEARCH_CLIP = 0.05
LOG2_E = 1.4426950408889634
INT_MIN = -2147483648


def _const_spec(shape):
    nd = len(shape)
    return pl.BlockSpec(shape, lambda *_: (0,) * nd, pipeline_mode=pl.Buffered(1))


def _proj_kernel(x_ref, ng_ref, wuvz_ref, wnat_ref, wg_ref, wt_ref,
                 lng_ref, lnb_ref, wmix_ref, bmix_ref, wpa_ref,
                 ma_ref, gb_ref, zb_ref, k_ref, v_ref, ki_ref, khm_ref, kib_ref,
                 qt_ref, qit_ref, wit_ref, vt_ref, *vn_refs, n_sub):
    x = x_ref[0]
    ms = jnp.mean(x * x, axis=-1, keepdims=True)
    h = (x * lax.rsqrt(ms + RMS_EPS) * ng_ref[...]).astype(BF)

    uvz = jnp.dot(h, wuvz_ref[...], preferred_element_type=F32)
    u = jax.nn.gelu(uvz[:, :SGU_WIDTH])
    v = jax.nn.gelu(uvz[:, SGU_WIDTH:2 * SGU_WIDTH])
    za = uvz[:, 2 * SGU_WIDTH:]
    mu = jnp.mean(v, axis=-1, keepdims=True)
    var = jnp.mean(jnp.square(v - mu), axis=-1, keepdims=True)
    vn = (v - mu) * lax.rsqrt(var + LN_EPS) * lng_ref[...] + lnb_ref[...]
    if vn_refs:
        vn_refs[0][0] = vn
    lane = lax.broadcasted_iota(I32, (SGU_LEN, LANES), 1)
    cols = []
    for c in range(SGU_WIDTH // LANES):
        rows = []
        for s in range(n_sub):
            blk = vn[s * SGU_LEN:(s + 1) * SGU_LEN, c * LANES:(c + 1) * LANES]
            lo = jnp.where(lane < SGU_GDIM, blk, 0.0).astype(BF)
            hi = jnp.where(lane >= SGU_GDIM, blk, 0.0).astype(BF)
            rhs = jnp.concatenate([lo, hi], axis=0)
            rows.append(jnp.dot(wmix_ref[c], rhs, preferred_element_type=F32))
        cols.append(jnp.concatenate(rows, axis=0) if n_sub > 1 else rows[0])
    mixed = jnp.concatenate(cols, axis=1)
    bias = bmix_ref[...]
    if n_sub > 1:
        bias = jnp.concatenate([bias] * n_sub, axis=0)
    a_out = u * (mixed + bias) * jax.nn.silu(za)
    pa = jnp.dot(a_out.astype(BF), wpa_ref[...], preferred_element_type=F32)
    g = jnp.dot(h, wg_ref[...], preferred_element_type=F32)
    ma_ref[0] = (jax.nn.sigmoid(g[:, :D_MODEL]) * pa).astype(BF)
    gb_ref[0] = g[:, D_MODEL:]

    nat = jnp.dot(h, wnat_ref[...], preferred_element_type=F32)
    k = nat[:, :KV_WIDTH]
    k_ref[0] = k
    v_ref[0] = nat[:, KV_WIDTH:2 * KV_WIDTH]
    zb_ref[0] = nat[:, 2 * KV_WIDTH:2 * KV_WIDTH + ATT_WIDTH]
    ki = nat[:, 2 * KV_WIDTH + ATT_WIDTH:]
    ki_ref[0] = ki
    kib_ref[0] = ki.astype(BF)
    for kv in range(N_KV_HEADS):
        khm_ref[0, kv] = k[:, kv * HEAD_DIM:(kv + 1) * HEAD_DIM].astype(BF)

    tm = lax.dot_general(wt_ref[...], h, (((1,), (1,)), ((), ())),
                         preferred_element_type=F32)
    o_qi, o_v, o_wi = ATT_WIDTH, 2 * ATT_WIDTH, 2 * ATT_WIDTH + KV_WIDTH
    for jb in range(n_sub):
        tok = slice(jb * TQ, (jb + 1) * TQ)
        for hh in range(N_HEADS):
            kv, gi = divmod(hh, GROUP)
            qt_ref[0, jb, kv, :, gi * TQ:(gi + 1) * TQ] = (
                tm[hh * HEAD_DIM:(hh + 1) * HEAD_DIM, tok] * (HEAD_DIM ** -0.5 * LOG2_E)).astype(BF)
        for hh in range(N_IDX_HEADS):
            qit_ref[0, jb, :, hh * TQ:(hh + 1) * TQ] = (
                tm[o_qi + hh * IDX_DIM:o_qi + (hh + 1) * IDX_DIM, tok] * (IDX_DIM ** -0.5)).astype(BF)
    for kv in range(N_KV_HEADS):
        vt_ref[0, 0, kv * V_ROWS:kv * V_ROWS + HEAD_DIM, :] = (
            tm[o_v + kv * HEAD_DIM:o_v + (kv + 1) * HEAD_DIM, :].astype(BF))
        vt_ref[0, 0, kv * V_ROWS + HEAD_DIM:(kv + 1) * V_ROWS, :] = jnp.ones(
            (BF16_ROWS, tm.shape[1]), BF)
    wit_ref[0] = tm[o_wi:o_wi + N_IDX_HEADS, :] * (N_IDX_HEADS ** -0.5)


def _proj_call(x, wts, wmix, bmix, *, tt, lt, emit_vn):
    b, s, _ = x.shape
    n_sub = tt // SGU_LEN
    r = lt // tt
    grid = (b, s // tt)
    tok = lambda w, dt=F32: (jax.ShapeDtypeStruct((b, s, w), dt),
                             pl.BlockSpec((1, tt, w), lambda i, j: (i, j, 0)))
    outs = [
        tok(D_MODEL, BF),
        tok(D_MODEL),
        tok(ATT_WIDTH),
        tok(KV_WIDTH), tok(KV_WIDTH), tok(IDX_DIM),
        (jax.ShapeDtypeStruct((b, N_KV_HEADS, s, HEAD_DIM), BF),
         pl.BlockSpec((1, N_KV_HEADS, tt, HEAD_DIM), lambda i, j: (i, 0, j, 0))),
        tok(IDX_DIM, BF),
        (jax.ShapeDtypeStruct((b, s // TQ, N_KV_HEADS, HEAD_DIM, GROUP * TQ), BF),
         pl.BlockSpec((1, n_sub, N_KV_HEADS, HEAD_DIM, GROUP * TQ), lambda i, j: (i, j, 0, 0, 0))),
        (jax.ShapeDtypeStruct((b, s // TQ, IDX_DIM, N_IDX_HEADS * TQ), BF),
         pl.BlockSpec((1, n_sub, IDX_DIM, N_IDX_HEADS * TQ), lambda i, j: (i, j, 0, 0))),
        (jax.ShapeDtypeStruct((b, N_IDX_HEADS, s), F32),
         pl.BlockSpec((1, N_IDX_HEADS, tt), lambda i, j: (i, 0, j))),
        (jax.ShapeDtypeStruct((b, s // lt, N_KV_HEADS * V_ROWS, lt), BF),
         pl.BlockSpec((1, 1, N_KV_HEADS * V_ROWS, tt), lambda i, j: (i, j // r, 0, j % r))),
    ]
    if emit_vn:
        outs.append(tok(SGU_WIDTH))
    consts = [wts["norm_g"], wts["w_uvz"], wts["w_nat"], wts["w_g"], wts["w_t"],
              wts["ln_g"], wts["ln_b"], wmix, bmix, wts["w_pa"]]
    return pl.pallas_call(
        functools.partial(_proj_kernel, n_sub=n_sub),
        grid=grid,
        in_specs=[pl.BlockSpec((1, tt, D_MODEL), lambda i, j: (i, j, 0))]
                 + [_const_spec(c.shape) for c in consts],
        out_specs=[o[1] for o in outs],
        out_shape=[o[0] for o in outs],
        compiler_params=pltpu.CompilerParams(
            dimension_semantics=("arbitrary", "arbitrary"), vmem_limit_bytes=VMEM_LIMIT),
        name="proj_vn" if emit_vn else "proj",
    )(x, *consts)


def _key_to_f32(key_u):
    s = key_u ^ INT_MIN
    bits = s ^ (lax.shift_right_arithmetic(s, 31) & 0x7FFFFFFF)
    return lax.bitcast_convert_type(bits, F32)


def _f32_to_key(x):
    bits = lax.bitcast_convert_type(x, I32)
    return bits ^ (lax.shift_right_arithmetic(bits, 31) & 0x7FFFFFFF) ^ INT_MIN


def _attn_kernel(qt_ref, qit_ref, wit_ref, zb_ref, k_ref, vt_ref, ki_ref, tri_ref,
                 o_ref, sc_ref, bot_ref, mma_ref, mmb_ref, *, n_tiles, topk, causal, stream_len,
                 n_valid):
    lane = lax.broadcasted_iota(I32, (1, TQ), 1)
    if causal:
        q0 = pl.program_id(1) * TQ
        n_t = lax.div(q0 + TQ + LT - 1, LT)
        key_lo = jnp.zeros((1, TQ), I32)
        key_hi = (lax.shift_right_logical(q0 + lane, 6) + 1) * CHUNK
    else:
        n_t = n_tiles
        slab = (n_tiles * LT) // (TQ // stream_len)
        key_lo = lax.div(lane, stream_len) * slab
        key_hi = key_lo + n_valid

    qit = qit_ref[0, 0]
    wit = wit_ref[0]
    wrows = [wit[hh:hh + 1, :] for hh in range(N_IDX_HEADS)]
    row0 = lax.broadcasted_iota(I32, (LT, 1), 0)

    fold = lambda x, op: op(x.reshape(LT // SUBLANES, SUBLANES, TQ), axis=0)

    last = n_t - 1
    n_pairs = lax.div(n_t, 2)
    odd = n_t - 2 * n_pairs == 1
    keep = lambda c: c

    def idx_dot(t, buf):
        start = pl.multiple_of(t * LT, LT)
        buf[...] = jnp.dot(ki_ref[0, pl.ds(start, LT), :], qit, preferred_element_type=F32)

    def score_tile(t, buf, carry):
        mx, mn, n_nonneg, n_pos = carry
        sc = jnp.zeros((LT, TQ), F32)
        for hh in range(N_IDX_HEADS):
            sc = sc + jnp.maximum(buf[:, hh * TQ:(hh + 1) * TQ], 0.0) * wrows[hh]
        pos = row0 + t * LT
        ok = pos < key_hi if causal else (pos >= key_lo) & (pos < key_hi)
        mn = jnp.minimum(mn, fold(sc, jnp.min))
        sc = jnp.where(ok, sc, -jnp.inf)
        sc_ref[t] = sc
        ones = lambda m: fold(jnp.where(m, 1, 0).astype(I32), jnp.sum)
        return (jnp.maximum(mx, fold(sc, jnp.max)), mn,
                n_nonneg + ones(sc >= 0.0), n_pos + ones(sc > 0.0))

    def score_pair(i, carry):
        t0 = 2 * i
        idx_dot(t0 + 1, mmb_ref)
        carry = score_tile(t0, mma_ref, carry)
        idx_dot(jnp.minimum(t0 + 2, last), mma_ref)
        return score_tile(t0 + 1, mmb_ref, carry)

    idx_dot(0, mma_ref)
    stats = lax.fori_loop(
        0, n_pairs, score_pair,
        (jnp.full((SUBLANES, TQ), -jnp.inf, F32), jnp.full((SUBLANES, TQ), jnp.inf, F32),
         jnp.zeros((SUBLANES, TQ), I32), jnp.zeros((SUBLANES, TQ), I32)))
    mx, mn, n_nonneg, n_pos = lax.cond(
        odd, lambda c: score_tile(last, mma_ref, c), keep, stats)
    mx = jnp.max(mx, axis=0, keepdims=True)
    mn = jnp.min(mn, axis=0, keepdims=True)
    n_nonneg = jnp.sum(n_nonneg, axis=0, keepdims=True).astype(F32)
    n_pos = jnp.sum(n_pos, axis=0, keepdims=True).astype(F32)

    def count(pred):
        ones = lambda t: fold(jnp.where(pred(sc_ref[t]), 1, 0).astype(I32), jnp.sum)

        acc = lax.fori_loop(0, n_pairs, lambda i, a: a + ones(2 * i) + ones(2 * i + 1),
                            jnp.zeros((SUBLANES, TQ), I32))
        acc = lax.cond(odd, lambda a: a + ones(last), keep, acc)
        return jnp.sum(acc, axis=0, keepdims=True)

    k_f = float(topk)
    n_adm = (key_hi - key_lo).astype(F32)
    short = n_adm < k_f

    def bisect_bits():
        def bit_pass(i, carry):
            thr_u, n_ge = carry
            cand_u = thr_u | lax.shift_left(jnp.int32(1), 31 - i)
            cand = _key_to_f32(cand_u)
            cnt = count(lambda sc: sc >= cand)
            take = cnt >= topk
            return jnp.where(take, cand_u, thr_u), jnp.where(take, cnt, n_ge)

        thr_u, n_ge = lax.fori_loop(
            0, 32, bit_pass, (jnp.zeros((1, TQ), I32), jnp.zeros((1, TQ), I32)))
        thr = jnp.where(short, LOWEST, _key_to_f32(thr_u))
        n_gt = count(lambda sc: sc > thr)
        need = jnp.where(short, ALL_TIES, (topk - n_gt).astype(F32))
        tie = jnp.logical_and(jnp.logical_not(short), n_ge > topk)
        return thr, need, jnp.max(jnp.where(tie, 1, 0))

    count_f = lambda pred: count(pred).astype(F32)
    zero_thr = jnp.logical_and(n_pos < k_f, n_nonneg >= k_f)
    pos_side = n_pos >= k_f
    lo0 = jnp.where(pos_side, 0.0, mn)
    hi0 = jnp.where(pos_side, mx, 0.0)
    clo0 = jnp.where(pos_side, n_nonneg, n_adm)
    chi0 = jnp.where(pos_side, 0.0, n_nonneg)
    logk = jnp.log(k_f)
    f_of = lambda c: jnp.log(jnp.maximum(c, 0.5)) - logk
    done0 = jnp.where(short | zero_thr | (clo0 == k_f), 1, 0)

    def search_pass(st):
        lo, hi, clo, chi, flo, fhi, side, done = st
        frac = jnp.clip(flo / (flo - fhi), SEARCH_CLIP, 1.0 - SEARCH_CLIP)
        cand = lo + (hi - lo) * frac
        cnt = count_f(lambda sc: sc >= cand)
        ge = cnt >= k_f
        up_lo = jnp.logical_and(done == 0, ge)
        up_hi = jnp.logical_and(done == 0, jnp.logical_not(ge))
        fhi = jnp.where(up_lo & (side == 1), fhi * 0.5, fhi)
        flo = jnp.where(up_hi & (side == -1), flo * 0.5, flo)
        lo = jnp.where(up_lo, cand, lo)
        clo = jnp.where(up_lo, cnt, clo)
        flo = jnp.where(up_lo, f_of(cnt), flo)
        hi = jnp.where(up_hi, cand, hi)
        chi = jnp.where(up_hi, cnt, chi)
        fhi = jnp.where(up_hi, f_of(cnt), fhi)
        side = jnp.where(up_lo, 1, jnp.where(up_hi, -1, side))
        done = jnp.where(clo == k_f, 1, done)
        return lo, hi, clo, chi, flo, fhi, side, done

    def walk_plan(st):
        lo, _, clo, chi = st[:4]
        from_hi = k_f - chi
        from_lo = clo - k_f + 1.0
        down = jnp.logical_or(from_hi <= from_lo, lo == 0.0)
        return down, jnp.where(st[7] == 1, 0.0, jnp.where(down, from_hi, from_lo))

    most_steps = lambda st: jnp.max(walk_plan(st)[1]).astype(I32)

    st = lax.fori_loop(0, UNCHECKED_SEARCH_PASSES, lambda i, s: search_pass(s),
                       (lo0, hi0, clo0, chi0, f_of(clo0), f_of(chi0), jnp.zeros((1, TQ), I32), done0))

    def checked_pass(c):
        s = search_pass(c[2])
        return c[0] + 1, most_steps(s), s

    _, n_steps, st = lax.while_loop(
        lambda c: jnp.logical_and(c[0] < MAX_SEARCH_PASSES, c[1] > MAX_WALK_STEPS), checked_pass,
        (jnp.int32(UNCHECKED_SEARCH_PASSES), most_steps(st), st))

    def walked():
        lo, hi = st[0], st[1]
        down, steps = walk_plan(st)
        sign = jnp.where(down, 1.0, -1.0)
        bound0 = jnp.where(down, hi, _key_to_f32(_f32_to_key(-lo) + 1))

        def step(i, bound):
            def below(t):
                y = sc_ref[t] * sign
                return fold(jnp.where(y < bound, y, -jnp.inf), jnp.max)

            nxt = lax.fori_loop(
                0, n_pairs, lambda j, a: jnp.maximum(a, jnp.maximum(below(2 * j), below(2 * j + 1))),
                jnp.full((SUBLANES, TQ), -jnp.inf, F32))
            nxt = lax.cond(odd, lambda a: jnp.maximum(a, below(last)), keep, nxt)
            return jnp.where(steps > i.astype(F32), jnp.max(nxt, axis=0, keepdims=True), bound)

        found = lax.fori_loop(0, n_steps, step, bound0) * sign
        thr = jnp.where(short, LOWEST,
                        jnp.where(zero_thr, 0.0, jnp.where(st[7] == 1, lo, found)))
        tie = jnp.logical_and(zero_thr, n_nonneg > k_f)
        need = jnp.where(tie, k_f - n_pos, ALL_TIES)
        return thr, need, jnp.max(jnp.where(tie, 1, 0))

    use_bits = n_steps > MAX_WALK_STEPS
    thr, need, n_tie = lax.cond(use_bits, bisect_bits, walked)

    gw = GROUP * TQ

    def qk_dots(t, buf):
        start = pl.multiple_of(t * LT, LT)
        for kv in range(N_KV_HEADS):
            buf[:, kv * gw:(kv + 1) * gw] = jnp.dot(
                k_ref[0, kv, pl.ds(start, LT), :], qt_ref[0, 0, kv], preferred_element_type=F32)

    def attend(with_ties, thr, need):
        def attend_tile(t, buf, carry):
            n_eq, n_sel, ms, accs = carry
            sc = sc_ref[t]
            if with_ties:
                eq = sc == thr
                pref = jnp.dot(tri_ref[...], jnp.where(eq, 1.0, 0.0).astype(BF),
                               preferred_element_type=F32)
                sel = (sc > thr) | (eq & (n_eq + pref - 1.0 < need))
                n_eq = n_eq + pref[LT - 1:LT, :]
            else:
                sel = sc >= thr
            n_sel = n_sel + fold(jnp.where(sel, 1, 0).astype(I32), jnp.sum)
            bias = jnp.where(sel, 0.0, MASKED_LOGIT)
            bias = jnp.concatenate([bias] * GROUP, axis=1)
            new_m, ps, alphas = [], [], []
            for kv in range(N_KV_HEADS):
                lg = buf[:, kv * gw:(kv + 1) * gw] + bias
                m_new = jnp.maximum(ms[kv], jnp.max(lg, axis=0, keepdims=True))
                ps.append(jnp.exp2(lg - m_new).astype(BF))
                alphas.append(jnp.exp2(ms[kv] - m_new))
                new_m.append(m_new)
            new_acc = []
            for kv in range(N_KV_HEADS):
                pv = jnp.dot(vt_ref[0, t, kv * V_ROWS:(kv + 1) * V_ROWS, :], ps[kv],
                             preferred_element_type=F32)
                new_acc.append(alphas[kv] * accs[kv] + pv)
            return n_eq, n_sel, tuple(new_m), tuple(new_acc)

        def attend_pair(i, carry):
            t0 = 2 * i
            qk_dots(t0 + 1, mmb_ref)
            carry = attend_tile(t0, mma_ref, carry)
            qk_dots(jnp.minimum(t0 + 2, last), mma_ref)
            return attend_tile(t0 + 1, mmb_ref, carry)

        init = (jnp.zeros((1, TQ), F32), jnp.zeros((SUBLANES, TQ), I32),
                tuple(jnp.full((1, gw), MASKED_LOGIT, F32) for _ in range(N_KV_HEADS)),
                tuple(jnp.zeros((V_ROWS, gw), F32) for _ in range(N_KV_HEADS)))
        qk_dots(0, mma_ref)
        carry = lax.fori_loop(0, n_pairs, attend_pair, init)
        _, n_sel, _, accs = lax.cond(odd, lambda c: attend_tile(last, mma_ref, c), keep, carry)
        return accs, jnp.sum(n_sel, axis=0, keepdims=True)

    accs, n_sel = lax.cond(n_tie > 0, lambda: attend(True, thr, need),
                           lambda: attend(False, thr, need))

    n_want = jnp.where(short, n_adm, k_f)
    n_miss = jnp.sum(jnp.where(n_sel.astype(F32) != n_want, 1, 0))

    def redo():
        thr_b, need_b, _ = bisect_bits()
        return attend(True, thr_b, need_b)[0]

    accs = lax.cond(jnp.logical_and(n_miss > 0, jnp.logical_not(use_bits)), redo, lambda: accs)

    for kv in range(N_KV_HEADS):
        o_t = accs[kv][:HEAD_DIM] / accs[kv][HEAD_DIM:HEAD_DIM + 1]
        for gi in range(GROUP):
            hh = GROUP * kv + gi
            bot_ref[hh * HEAD_DIM:(hh + 1) * HEAD_DIM, :] = o_t[:, gi * TQ:(gi + 1) * TQ]
    o_ref[0] = (bot_ref[...].T * jax.nn.silu(zb_ref[0])).astype(BF)


def _attn_call(qt, qit, wit, zb, k_hm, vt, ki_b, *, topk, causal, stream_len=0, n_valid=0):
    b, nq = qt.shape[0], qt.shape[1]
    s = nq * TQ
    n_tiles = vt.shape[1]
    l_all = k_hm.shape[2]
    assert l_all == n_tiles * LT
    tri = (lax.broadcasted_iota(I32, (LT, LT), 0) >= lax.broadcasted_iota(I32, (LT, LT), 1)).astype(BF)
    whole = lambda shape: pl.BlockSpec((1,) + shape, lambda i, j: (i,) + (0,) * len(shape),
                                       pipeline_mode=pl.Buffered(1))
    return pl.pallas_call(
        functools.partial(_attn_kernel, n_tiles=n_tiles, topk=topk, causal=causal,
                          stream_len=stream_len, n_valid=n_valid),
        grid=(b, nq),
        in_specs=[
            pl.BlockSpec((1, 1, N_KV_HEADS, HEAD_DIM, GROUP * TQ), lambda i, j: (i, j, 0, 0, 0)),
            pl.BlockSpec((1, 1, IDX_DIM, N_IDX_HEADS * TQ), lambda i, j: (i, j, 0, 0)),
            pl.BlockSpec((1, N_IDX_HEADS, TQ), lambda i, j: (i, 0, j)),
            pl.BlockSpec((1, TQ, ATT_WIDTH), lambda i, j: (i, j, 0)),
            whole((N_KV_HEADS, l_all, HEAD_DIM)),
            whole((n_tiles, N_KV_HEADS * V_ROWS, LT)),
            whole((l_all, IDX_DIM)),
            _const_spec((LT, LT)),
        ],
        out_specs=pl.BlockSpec((1, TQ, ATT_WIDTH), lambda i, j: (i, j, 0)),
        out_shape=jax.ShapeDtypeStruct((b, s, ATT_WIDTH), BF),
        scratch_shapes=[pltpu.VMEM((n_tiles, LT, TQ), F32), pltpu.VMEM((ATT_WIDTH, TQ), F32),
                        pltpu.VMEM((LT, N_IDX_HEADS * TQ), F32), pltpu.VMEM((LT, N_IDX_HEADS * TQ), F32)],
        compiler_params=pltpu.CompilerParams(
            dimension_semantics=("arbitrary", "arbitrary"), vmem_limit_bytes=VMEM_LIMIT),
        name="attn_prompt" if causal else "attn_sample",
    )(qt, qit, wit, zb, k_hm, vt, ki_b, tri)


def _merge_kernel(x_ref, ma_ref, gb_ref, bg_ref, wpb_ref, wout_ref, fg_ref, y_ref):
    pb = jnp.dot(bg_ref[0], wpb_ref[...], preferred_element_type=F32)
    m = ma_ref[0].astype(F32) + jax.nn.sigmoid(gb_ref[0]) * pb
    xo = x_ref[0] + jnp.dot(m.astype(BF), wout_ref[...], preferred_element_type=F32)
    ms = jnp.mean(xo * xo, axis=-1, keepdims=True)
    y_ref[0] = xo * lax.rsqrt(ms + RMS_EPS) * fg_ref[...]


def _merge_call(x, ma, gb, bg, wts, *, tt):
    b, s, _ = x.shape
    tok = lambda w: pl.BlockSpec((1, tt, w), lambda i, j: (i, j, 0))
    consts = [wts["w_pb"], wts["w_out"], wts["final_g"]]
    return pl.pallas_call(
        _merge_kernel,
        grid=(b, s // tt),
        in_specs=[tok(D_MODEL), tok(D_MODEL), tok(D_MODEL), tok(ATT_WIDTH)]
                 + [_const_spec(c.shape) for c in consts],
        out_specs=tok(D_MODEL),
        out_shape=jax.ShapeDtypeStruct((b, s, D_MODEL), F32),
        compiler_params=pltpu.CompilerParams(
            dimension_semantics=("arbitrary", "arbitrary"), vmem_limit_bytes=VMEM_LIMIT),
        name="merge",
    )(x, ma, gb, bg, *consts)


def _prep_weights(norm_g, w_in, ln_g, ln_b, w_pa, w_pb, w_out, final_g):
    o = 0
    cols = {}
    for name, width in (("u", SGU_WIDTH), ("v", SGU_WIDTH), ("za", SGU_WIDTH), ("q", ATT_WIDTH),
                        ("k", KV_WIDTH), ("vv", KV_WIDTH), ("zb", ATT_WIDTH),
                        ("qi", N_IDX_HEADS * IDX_DIM), ("ki", IDX_DIM), ("wi", N_IDX_HEADS),
                        ("ga", D_MODEL), ("gb", D_MODEL)):
        cols[name] = (o, o + width)
        o += width
    wb = w_in.astype(BF)
    sl = lambda a, z: wb[:, cols[a][0]:cols[z][1]]
    return {
        "norm_g": norm_g.reshape(1, D_MODEL), "w_uvz": sl("u", "za"),
        "w_nat": jnp.concatenate([sl("k", "zb"), sl("ki", "ki")], axis=1),
        "w_g": sl("ga", "gb"),
        "w_t": jnp.concatenate([sl("q", "q"), sl("qi", "qi"), sl("vv", "vv"), sl("wi", "wi")], axis=1).T,
        "ln_g": ln_g.reshape(1, SGU_WIDTH), "ln_b": ln_b.reshape(1, SGU_WIDTH),
        "w_pa": w_pa.astype(BF), "w_pb": w_pb.astype(BF), "w_out": w_out.astype(BF),
        "final_g": final_g.reshape(1, D_MODEL),
    }


def _mix_operands(w_mix, b_pos):
    pairs = w_mix.reshape(SGU_GROUPS // 2, 2, SGU_LEN, SGU_LEN)
    wmix = jnp.concatenate([pairs[:, 0], pairs[:, 1]], axis=-1).astype(BF)
    bmix = jnp.repeat(b_pos, SGU_GDIM, axis=1)
    return wmix, bmix


def kernel(x_prompt, x_sample, cache_k, cache_v, cache_kidx, norm_g, w_in, sgu_ln_g, sgu_ln_b,
           sgu_w, sgu_b, w_pa, w_pb, w_out, final_g):
    b, s, _ = x_prompt.shape
    db, ds, _ = x_sample.shape
    past = cache_k.shape[2]
    assert norm_g.shape[0] == 1 and db * ds == TQ and s % LT == 0
    wts = _prep_weights(norm_g[0], w_in[0], sgu_ln_g[0], sgu_ln_b[0], w_pa[0], w_pb[0], w_out[0],
                        final_g)

    pos = jnp.arange(SGU_LEN)
    chunk_ok = (pos[None, :] // CHUNK) <= (pos[:, None] // CHUNK)
    wmix_p, bmix_p = _mix_operands(jnp.where(chunk_ok[None], sgu_w[0], 0.0), sgu_b[0].T)
    (ma, gb, zb, k, v, ki, k_hm, ki_b, qt, qit, wit, vt) = _proj_call(
        x_prompt, wts, wmix_p, bmix_p, tt=4 * TQ, lt=LT, emit_vn=False)
    bg = _attn_call(qt, qit, wit, zb, k_hm, vt, ki_b, topk=min(TOPK_MAX, s // 4), causal=True)
    y_prompt = _merge_call(x_prompt, ma, gb, bg, wts, tt=512)

    assert past % CHUNK == 0 and ds <= CHUNK
    w_blk = jnp.einsum("ab,gij->gaibj", jnp.eye(db, dtype=F32), sgu_w[0][:, :ds, :ds])
    wmix_s, bmix_s = _mix_operands(w_blk.reshape(SGU_GROUPS, SGU_LEN, SGU_LEN),
                                   jnp.tile(sgu_b[0][:, :ds].T, (db, 1)))
    n_tok = db * ds
    (ma_s, gb_s, zb_s, k_s, v_s, ki_s, _, _, qt_s, qit_s, wit_s, _, vn_s) = _proj_call(
        x_sample.reshape(1, n_tok, D_MODEL), wts, wmix_s, bmix_s, tt=n_tok, lt=n_tok, emit_vn=True)
    n_keys = past + ds
    slab = -(-n_keys * db // LT) * LT // db
    assert (slab * db) % LT == 0 and slab >= n_keys
    cat = lambda c, n, w: jnp.pad(
        jnp.concatenate([c.reshape(db, past, w), n.reshape(db, ds, w)], axis=1),
        ((0, 0), (0, slab - n_keys), (0, 0))).reshape(db * slab, w).astype(BF)
    k_all, v_all, ki_all = (cat(cache_k[0], k_s, KV_WIDTH), cat(cache_v[0], v_s, KV_WIDTH),
                            cat(cache_kidx[0], ki_s, IDX_DIM))
    n_t_s = db * slab // LT
    k_hm_s = k_all.reshape(1, db * slab, N_KV_HEADS, HEAD_DIM).transpose(0, 2, 1, 3)
    vt_s = v_all.reshape(n_t_s, LT, N_KV_HEADS, HEAD_DIM).transpose(0, 2, 3, 1)
    vt_s = jnp.concatenate([vt_s, jnp.ones((n_t_s, N_KV_HEADS, BF16_ROWS, LT), BF)], axis=2)
    vt_s = vt_s.reshape(1, n_t_s, N_KV_HEADS * V_ROWS, LT)
    bg_s = _attn_call(qt_s, qit_s, wit_s, zb_s, k_hm_s, vt_s, ki_all[None],
                      topk=min(TOPK_MAX, n_keys // 4), causal=False, stream_len=ds, n_valid=n_keys)
    y_sample = _merge_call(x_sample.reshape(1, n_tok, D_MODEL), ma_s, gb_s, bg_s, wts, tt=n_tok)

    return (y_prompt, y_sample.reshape(db, ds, D_MODEL),
            k.reshape(1, b, s, N_KV_HEADS, HEAD_DIM), v.reshape(1, b, s, N_KV_HEADS, HEAD_DIM),
            ki.reshape(1, b, s, IDX_DIM),
            k_s.reshape(1, db, ds, N_KV_HEADS, HEAD_DIM), v_s.reshape(1, db, ds, N_KV_HEADS, HEAD_DIM),
            ki_s.reshape(1, db, ds, IDX_DIM), vn_s.reshape(1, db, ds, SGU_WIDTH))
```

```python
import functools

import jax
import jax.numpy as jnp
from jax import lax
from jax.experimental import pallas as pl
from jax.experimental.pallas import tpu as pltpu

D_MODEL = 1024
CHUNK = 64
SGU_LEN = 128
SGU_GROUPS = 8
SGU_WIDTH = 512
SGU_GDIM = SGU_WIDTH // SGU_GROUPS
N_HEADS = 8
N_KV_HEADS = 4
HEAD_DIM = 64
GROUP = N_HEADS // N_KV_HEADS
ATT_WIDTH = N_HEADS * HEAD_DIM
KV_WIDTH = N_KV_HEADS * HEAD_DIM
N_IDX_HEADS = 8
IDX_DIM = 64
TOPK_MAX = 256
RMS_EPS = 1e-6
LN_EPS = 1e-5

BF = jnp.bfloat16
F32 = jnp.float32
I32 = jnp.int32

LANES = 128
SUBLANES = 8
BF16_ROWS = 16
TQ = LANES
LT = 512
V_ROWS = HEAD_DIM + BF16_ROWS
VMEM_LIMIT = 52 * 1024 * 1024
MASKED_LOGIT = -1e30
LOWEST = -3.4028235e38
MIN_SOFTMAX_SUM = 1e-30
ALL_TIES = 3e38
UNCHECKED_SEARCH_PASSES = 8
MAX_SEARCH_PASSES = 28
MAX_WALK_STEPS = 8
SEARCH_CLIP = 0.05
LOG2_E = 1.4426950408889634
INT_MIN = -2147483648


def _const_spec(shape):
    nd = len(shape)
    return pl.BlockSpec(shape, lambda *_: (0,) * nd, pipeline_mode=pl.Buffered(1))


def _proj_kernel(x_ref, ng_ref, wuvz_ref, wnat_ref, wg_ref, wt_ref,
                 lng_ref, lnb_ref, wmix_ref, bmix_ref, wpa_ref,
                 ma_ref, gb_ref, zb_ref, k_ref, v_ref, ki_ref, khm_ref, kib_ref,
                 qt_ref, qit_ref, wit_ref, vt_ref, *vn_refs, n_sub):
    x = x_ref[0]
    ms = jnp.mean(x * x, axis=-1, keepdims=True)
    h = (x * lax.rsqrt(ms + RMS_EPS) * ng_ref[...]).astype(BF)

    uvz = jnp.dot(h, wuvz_ref[...], preferred_element_type=F32)
    u = jax.nn.gelu(uvz[:, :SGU_WIDTH])
    v = jax.nn.gelu(uvz[:, SGU_WIDTH:2 * SGU_WIDTH])
    za = uvz[:, 2 * SGU_WIDTH:]
    mu = jnp.mean(v, axis=-1, keepdims=True)
    var = jnp.mean(jnp.square(v - mu), axis=-1, keepdims=True)
    vn = (v - mu) * lax.rsqrt(var + LN_EPS) * lng_ref[...] + lnb_ref[...]
    if vn_refs:
        vn_refs[0][0] = vn
    lane = lax.broadcasted_iota(I32, (SGU_LEN, LANES), 1)
    cols = []
    for c in range(SGU_WIDTH // LANES):
        rows = []
        for s in range(n_sub):
            blk = vn[s * SGU_LEN:(s + 1) * SGU_LEN, c * LANES:(c + 1) * LANES]
            lo = jnp.where(lane < SGU_GDIM, blk, 0.0).astype(BF)
            hi = jnp.where(lane >= SGU_GDIM, blk, 0.0).astype(BF)
            rhs = jnp.concatenate([lo, hi], axis=0)
            rows.append(jnp.dot(wmix_ref[c], rhs, preferred_element_type=F32))
        cols.append(jnp.concatenate(rows, axis=0) if n_sub > 1 else rows[0])
    mixed = jnp.concatenate(cols, axis=1)
    bias = bmix_ref[...]
    if n_sub > 1:
        bias = jnp.concatenate([bias] * n_sub, axis=0)
    a_out = u * (mixed + bias) * jax.nn.silu(za)
    pa = jnp.dot(a_out.astype(BF), wpa_ref[...], preferred_element_type=F32)
    g = jnp.dot(h, wg_ref[...], preferred_element_type=F32)
    ma_ref[0] = (jax.nn.sigmoid(g[:, :D_MODEL]) * pa).astype(BF)
    gb_ref[0] = g[:, D_MODEL:]

    nat = jnp.dot(h, wnat_ref[...], preferred_element_type=F32)
    k = nat[:, :KV_WIDTH]
    k_ref[0] = k
    v_ref[0] = nat[:, KV_WIDTH:2 * KV_WIDTH]
    zb_ref[0] = nat[:, 2 * KV_WIDTH:2 * KV_WIDTH + ATT_WIDTH]
    ki = nat[:, 2 * KV_WIDTH + ATT_WIDTH:]
    ki_ref[0] = ki
    kib_ref[0] = ki.astype(BF)
    for kv in range(N_KV_HEADS):
        khm_ref[0, kv] = k[:, kv * HEAD_DIM:(kv + 1) * HEAD_DIM].astype(BF)

    tm = lax.dot_general(wt_ref[...], h, (((1,), (1,)), ((), ())),
                         preferred_element_type=F32)
    o_qi, o_v, o_wi = ATT_WIDTH, 2 * ATT_WIDTH, 2 * ATT_WIDTH + KV_WIDTH
    for jb in range(n_sub):
        tok = slice(jb * TQ, (jb + 1) * TQ)
        for hh in range(N_HEADS):
            kv, gi = divmod(hh, GROUP)
            qt_ref[0, jb, kv, :, gi * TQ:(gi + 1) * TQ] = (
                tm[hh * HEAD_DIM:(hh + 1) * HEAD_DIM, tok] * (HEAD_DIM ** -0.5 * LOG2_E)).astype(BF)
        for hh in range(N_IDX_HEADS):
            qit_ref[0, jb, :, hh * TQ:(hh + 1) * TQ] = (
                tm[o_qi + hh * IDX_DIM:o_qi + (hh + 1) * IDX_DIM, tok] * (IDX_DIM ** -0.5)).astype(BF)
    for kv in range(N_KV_HEADS):
        vt_ref[0, 0, kv * V_ROWS:kv * V_ROWS + HEAD_DIM, :] = (
            tm[o_v + kv * HEAD_DIM:o_v + (kv + 1) * HEAD_DIM, :].astype(BF))
        vt_ref[0, 0, kv * V_ROWS + HEAD_DIM:(kv + 1) * V_ROWS, :] = jnp.ones(
            (BF16_ROWS, tm.shape[1]), BF)
    wit_ref[0] = tm[o_wi:o_wi + N_IDX_HEADS, :] * (N_IDX_HEADS ** -0.5)


def _proj_call(x, wts, wmix, bmix, *, tt, lt, emit_vn):
    b, s, _ = x.shape
    n_sub = tt // SGU_LEN
    r = lt // tt
    grid = (b, s // tt)
    tok = lambda w, dt=F32: (jax.ShapeDtypeStruct((b, s, w), dt),
                             pl.BlockSpec((1, tt, w), lambda i, j: (i, j, 0)))
    outs = [
        tok(D_MODEL, BF),
        tok(D_MODEL),
        tok(ATT_WIDTH),
        tok(KV_WIDTH), tok(KV_WIDTH), tok(IDX_DIM),
        (jax.ShapeDtypeStruct((b, N_KV_HEADS, s, HEAD_DIM), BF),
         pl.BlockSpec((1, N_KV_HEADS, tt, HEAD_DIM), lambda i, j: (i, 0, j, 0))),
        tok(IDX_DIM, BF),
        (jax.ShapeDtypeStruct((b, s // TQ, N_KV_HEADS, HEAD_DIM, GROUP * TQ), BF),
         pl.BlockSpec((1, n_sub, N_KV_HEADS, HEAD_DIM, GROUP * TQ), lambda i, j: (i, j, 0, 0, 0))),
        (jax.ShapeDtypeStruct((b, s // TQ, IDX_DIM, N_IDX_HEADS * TQ), BF),
         pl.BlockSpec((1, n_sub, IDX_DIM, N_IDX_HEADS * TQ), lambda i, j: (i, j, 0, 0))),
        (jax.ShapeDtypeStruct((b, N_IDX_HEADS, s), F32),
         pl.BlockSpec((1, N_IDX_HEADS, tt), lambda i, j: (i, 0, j))),
        (jax.ShapeDtypeStruct((b, s // lt, N_KV_HEADS * V_ROWS, lt), BF),
         pl.BlockSpec((1, 1, N_KV_HEADS * V_ROWS, tt), lambda i, j: (i, j // r, 0, j % r))),
    ]
    if emit_vn:
        outs.append(tok(SGU_WIDTH))
    consts = [wts["norm_g"], wts["w_uvz"], wts["w_nat"], wts["w_g"], wts["w_t"],
              wts["ln_g"], wts["ln_b"], wmix, bmix, wts["w_pa"]]
    return pl.pallas_call(
        functools.partial(_proj_kernel, n_sub=n_sub),
        grid=grid,
        in_specs=[pl.BlockSpec((1, tt, D_MODEL), lambda i, j: (i, j, 0))]
                 + [_const_spec(c.shape) for c in consts],
        out_specs=[o[1] for o in outs],
        out_shape=[o[0] for o in outs],
        compiler_params=pltpu.CompilerParams(
            dimension_semantics=("arbitrary", "arbitrary"), vmem_limit_bytes=VMEM_LIMIT),
        name="proj_vn" if emit_vn else "proj",
    )(x, *consts)


def _key_to_f32(key_u):
    s = key_u ^ INT_MIN
    bits = s ^ (lax.shift_right_arithmetic(s, 31) & 0x7FFFFFFF)
    return lax.bitcast_convert_type(bits, F32)


def _f32_to_key(x):
    bits = lax.bitcast_convert_type(x, I32)
    return bits ^ (lax.shift_right_arithmetic(bits, 31) & 0x7FFFFFFF) ^ INT_MIN


def _attn_kernel(qt_ref, qit_ref, wit_ref, zb_ref, k_ref, vt_ref, ki_ref, tri_ref,
                 o_ref, sc_ref, bot_ref, mma_ref, mmb_ref, *, n_tiles, topk, causal, stream_len,
                 n_valid):
    lane = lax.broadcasted_iota(I32, (1, TQ), 1)
    if causal:
        q0 = pl.program_id(1) * TQ
        n_t = lax.div(q0 + TQ + LT - 1, LT)
        key_lo = jnp.zeros((1, TQ), I32)
        key_hi = (lax.shift_right_logical(q0 + lane, 6) + 1) * CHUNK
    else:
        n_t = n_tiles
        slab = (n_tiles * LT) // (TQ // stream_len)
        key_lo = lax.div(lane, stream_len) * slab
        key_hi = key_lo + n_valid

    qit = qit_ref[0, 0]
    wit = wit_ref[0]
    wrows = [wit[hh:hh + 1, :] for hh in range(N_IDX_HEADS)]
    row0 = lax.broadcasted_iota(I32, (LT, 1), 0)

    fold = lambda x, op: op(x.reshape(LT // SUBLANES, SUBLANES, TQ), axis=0)

    last = n_t - 1
    n_pairs = lax.div(n_t, 2)
    odd = n_t - 2 * n_pairs == 1
    keep = lambda c: c

    def idx_dot(t, buf):
        start = pl.multiple_of(t * LT, LT)
        buf[...] = jnp.dot(ki_ref[0, pl.ds(start, LT), :], qit, preferred_element_type=F32)

    def score_tile(t, buf, carry):
        mx, mn, n_nonneg, n_pos = carry
        sc = jnp.zeros((LT, TQ), F32)
        for hh in range(N_IDX_HEADS):
            sc = sc + jnp.maximum(buf[:, hh * TQ:(hh + 1) * TQ], 0.0) * wrows[hh]
        pos = row0 + t * LT
        ok = pos < key_hi if causal else (pos >= key_lo) & (pos < key_hi)
        mn = jnp.minimum(mn, fold(sc, jnp.min))
        sc = jnp.where(ok, sc, -jnp.inf)
        sc_ref[t] = sc
        ones = lambda m: fold(jnp.where(m, 1, 0).astype(I32), jnp.sum)
        return (jnp.maximum(mx, fold(sc, jnp.max)), mn,
                n_nonneg + ones(sc >= 0.0), n_pos + ones(sc > 0.0))

    def score_pair(i, carry):
        t0 = 2 * i
        idx_dot(t0 + 1, mmb_ref)
        carry = score_tile(t0, mma_ref, carry)
        idx_dot(jnp.minimum(t0 + 2, last), mma_ref)
        return score_tile(t0 + 1, mmb_ref, carry)

    idx_dot(0, mma_ref)
    stats = lax.fori_loop(
        0, n_pairs, score_pair,
        (jnp.full((SUBLANES, TQ), -jnp.inf, F32), jnp.full((SUBLANES, TQ), jnp.inf, F32),
         jnp.zeros((SUBLANES, TQ), I32), jnp.zeros((SUBLANES, TQ), I32)))
    mx, mn, n_nonneg, n_pos = lax.cond(
        odd, lambda c: score_tile(last, mma_ref, c), keep, stats)
    mx = jnp.max(mx, axis=0, keepdims=True)
    mn = jnp.min(mn, axis=0, keepdims=True)
    n_nonneg = jnp.sum(n_nonneg, axis=0, keepdims=True).astype(F32)
    n_pos = jnp.sum(n_pos, axis=0, keepdims=True).astype(F32)

    def count(pred):
        ones = lambda t: fold(jnp.where(pred(sc_ref[t]), 1, 0).astype(I32), jnp.sum)

        acc = lax.fori_loop(0, n_pairs, lambda i, a: a + ones(2 * i) + ones(2 * i + 1),
                            jnp.zeros((SUBLANES, TQ), I32))
        acc = lax.cond(odd, lambda a: a + ones(last), keep, acc)
        return jnp.sum(acc, axis=0, keepdims=True)

    k_f = float(topk)
    n_adm = (key_hi - key_lo).astype(F32)
    short = n_adm < k_f

    def bisect_bits():
        def bit_pass(i, carry):
            thr_u, n_ge = carry
            cand_u = thr_u | lax.shift_left(jnp.int32(1), 31 - i)
            cand = _key_to_f32(cand_u)
            cnt = count(lambda sc: sc >= cand)
            take = cnt >= topk
            return jnp.where(take, cand_u, thr_u), jnp.where(take, cnt, n_ge)

        thr_u, n_ge = lax.fori_loop(
            0, 32, bit_pass, (jnp.zeros((1, TQ), I32), jnp.zeros((1, TQ), I32)))
        thr = jnp.where(short, LOWEST, _key_to_f32(thr_u))
        n_gt = count(lambda sc: sc > thr)
        need = jnp.where(short, ALL_TIES, (topk - n_gt).astype(F32))
        tie = jnp.logical_and(jnp.logical_not(short), n_ge > topk)
        return thr, need, jnp.max(jnp.where(tie, 1, 0))

    count_f = lambda pred: count(pred).astype(F32)
    zero_thr = jnp.logical_and(n_pos < k_f, n_nonneg >= k_f)
    pos_side = n_pos >= k_f
    lo0 = jnp.where(pos_side, 0.0, mn)
    hi0 = jnp.where(pos_side, mx, 0.0)
    clo0 = jnp.where(pos_side, n_nonneg, n_adm)
    chi0 = jnp.where(pos_side, 0.0, n_nonneg)
    logk = jnp.log(k_f)
    f_of = lambda c: jnp.log(jnp.maximum(c, 0.5)) - logk
    done0 = jnp.where(short | zero_thr | (clo0 == k_f), 1, 0)

    def search_pass(st):
        lo, hi, clo, chi, flo, fhi, side, done = st
        frac = jnp.clip(flo / (flo - fhi), SEARCH_CLIP, 1.0 - SEARCH_CLIP)
        cand = lo + (hi - lo) * frac
        cnt = count_f(lambda sc: sc >= cand)
        ge = cnt >= k_f
        up_lo = jnp.logical_and(done == 0, ge)
        up_hi = jnp.logical_and(done == 0, jnp.logical_not(ge))
        fhi = jnp.where(up_lo & (side == 1), fhi * 0.5, fhi)
        flo = jnp.where(up_hi & (side == -1), flo * 0.5, flo)
        lo = jnp.where(up_lo, cand, lo)
        clo = jnp.where(up_lo, cnt, clo)
        flo = jnp.where(up_lo, f_of(cnt), flo)
        hi = jnp.where(up_hi, cand, hi)
        chi = jnp.where(up_hi, cnt, chi)
        fhi = jnp.where(up_hi, f_of(cnt), fhi)
        side = jnp.where(up_lo, 1, jnp.where(up_hi, -1, side))
        done = jnp.where(clo == k_f, 1, done)
        return lo, hi, clo, chi, flo, fhi, side, done

    def walk_plan(st):
        lo, _, clo, chi = st[:4]
        from_hi = k_f - chi
        from_lo = clo - k_f + 1.0
        down = jnp.logical_or(from_hi <= from_lo, lo == 0.0)
        return down, jnp.where(st[7] == 1, 0.0, jnp.where(down, from_hi, from_lo))

    most_steps = lambda st: jnp.max(walk_plan(st)[1]).astype(I32)

    st = lax.fori_loop(0, UNCHECKED_SEARCH_PASSES, lambda i, s: search_pass(s),
                       (lo0, hi0, clo0, chi0, f_of(clo0), f_of(chi0), jnp.zeros((1, TQ), I32), done0))

    def checked_pass(c):
        s = search_pass(c[2])
        return c[0] + 1, most_steps(s), s

    _, n_steps, st = lax.while_loop(
        lambda c: jnp.logical_and(c[0] < MAX_SEARCH_PASSES, c[1] > MAX_WALK_STEPS), checked_pass,
        (jnp.int32(UNCHECKED_SEARCH_PASSES), most_steps(st), st))

    def walked():
        lo, hi = st[0], st[1]
        down, steps = walk_plan(st)
        sign = jnp.where(down, 1.0, -1.0)
        bound0 = jnp.where(down, hi, _key_to_f32(_f32_to_key(-lo) + 1))

        def step(i, bound):
            def below(t):
                y = sc_ref[t] * sign
                return fold(jnp.where(y < bound, y, -jnp.inf), jnp.max)

            nxt = lax.fori_loop(
                0, n_pairs, lambda j, a: jnp.maximum(a, jnp.maximum(below(2 * j), below(2 * j + 1))),
                jnp.full((SUBLANES, TQ), -jnp.inf, F32))
            nxt = lax.cond(odd, lambda a: jnp.maximum(a, below(last)), keep, nxt)
            return jnp.where(steps > i.astype(F32), jnp.max(nxt, axis=0, keepdims=True), bound)

        found = lax.fori_loop(0, n_steps, step, bound0) * sign
        thr = jnp.where(short, LOWEST,
                        jnp.where(zero_thr, 0.0, jnp.where(st[7] == 1, lo, found)))
        tie = jnp.logical_and(zero_thr, n_nonneg > k_f)
        need = jnp.where(tie, k_f - n_pos, ALL_TIES)
        return thr, need, jnp.max(jnp.where(tie, 1, 0))

    use_bits = n_steps > MAX_WALK_STEPS
    thr, need, n_tie = lax.cond(use_bits, bisect_bits, walked)

    gw = GROUP * TQ

    def qk_dots(t, buf):
        start = pl.multiple_of(t * LT, LT)
        for kv in range(N_KV_HEADS):
            buf[:, kv * gw:(kv + 1) * gw] = jnp.dot(
                k_ref[0, kv, pl.ds(start, LT), :], qt_ref[0, 0, kv], preferred_element_type=F32)

    def attend(with_ties, thr, need):
        def attend_tile(t, buf, carry):
            n_eq, n_sel, ms, accs = carry
            sc = sc_ref[t]
            if with_ties:
                eq = sc == thr
                pref = jnp.dot(tri_ref[...], jnp.where(eq, 1.0, 0.0).astype(BF),
                               preferred_element_type=F32)
                sel = (sc > thr) | (eq & (n_eq + pref - 1.0 < need))
                n_eq = n_eq + pref[LT - 1:LT, :]
            else:
                sel = sc >= thr
            n_sel = n_sel + fold(jnp.where(sel, 1, 0).astype(I32), jnp.sum)
            keep01 = jnp.where(sel, 1.0, 0.0).astype(BF)
            keep01 = jnp.concatenate([keep01] * GROUP, axis=1)
            new_m, ps, alphas = [], [], []
            for kv in range(N_KV_HEADS):
                lg = buf[:, kv * gw:(kv + 1) * gw]
                m_new = jnp.maximum(ms[kv], jnp.max(lg, axis=0, keepdims=True))
                ps.append(jnp.exp2(lg - m_new).astype(BF) * keep01)
                alphas.append(jnp.exp2(ms[kv] - m_new))
                new_m.append(m_new)
            new_acc = []
            for kv in range(N_KV_HEADS):
                pv = jnp.dot(vt_ref[0, t, kv * V_ROWS:(kv + 1) * V_ROWS, :], ps[kv],
                             preferred_element_type=F32)
                new_acc.append(alphas[kv] * accs[kv] + pv)
            return n_eq, n_sel, tuple(new_m), tuple(new_acc)

        def attend_pair(i, carry):
            t0 = 2 * i
            qk_dots(t0 + 1, mmb_ref)
            carry = attend_tile(t0, mma_ref, carry)
            qk_dots(jnp.minimum(t0 + 2, last), mma_ref)
            return attend_tile(t0 + 1, mmb_ref, carry)

        init = (jnp.zeros((1, TQ), F32), jnp.zeros((SUBLANES, TQ), I32),
                tuple(jnp.full((1, gw), MASKED_LOGIT, F32) for _ in range(N_KV_HEADS)),
                tuple(jnp.zeros((V_ROWS, gw), F32) for _ in range(N_KV_HEADS)))
        qk_dots(0, mma_ref)
        carry = lax.fori_loop(0, n_pairs, attend_pair, init)
        _, n_sel, _, accs = lax.cond(odd, lambda c: attend_tile(last, mma_ref, c), keep, carry)
        return accs, jnp.sum(n_sel, axis=0, keepdims=True)

    accs, n_sel = lax.cond(n_tie > 0, lambda: attend(True, thr, need),
                           lambda: attend(False, thr, need))

    def attend_exact(thr, need):
        def tile(t, carry):
            n_eq, ms, accs = carry
            sc = sc_ref[t]
            eq = sc == thr
            pref = jnp.dot(tri_ref[...], jnp.where(eq, 1.0, 0.0).astype(BF),
                           preferred_element_type=F32)
            sel = (sc > thr) | (eq & (n_eq + pref - 1.0 < need))
            n_eq = n_eq + pref[LT - 1:LT, :]
            bias = jnp.concatenate([jnp.where(sel, 0.0, MASKED_LOGIT)] * GROUP, axis=1)
            start = pl.multiple_of(t * LT, LT)
            new_m, new_acc = [], []
            for kv in range(N_KV_HEADS):
                lg = jnp.dot(k_ref[0, kv, pl.ds(start, LT), :], qt_ref[0, 0, kv],
                             preferred_element_type=F32) + bias
                m_new = jnp.maximum(ms[kv], jnp.max(lg, axis=0, keepdims=True))
                p = jnp.exp2(lg - m_new).astype(BF)
                pv = jnp.dot(vt_ref[0, t, kv * V_ROWS:(kv + 1) * V_ROWS, :], p,
                             preferred_element_type=F32)
                new_acc.append(jnp.exp2(ms[kv] - m_new) * accs[kv] + pv)
                new_m.append(m_new)
            return n_eq, tuple(new_m), tuple(new_acc)

        init = (jnp.zeros((1, TQ), F32),
                tuple(jnp.full((1, gw), MASKED_LOGIT, F32) for _ in range(N_KV_HEADS)),
                tuple(jnp.zeros((V_ROWS, gw), F32) for _ in range(N_KV_HEADS)))
        return lax.fori_loop(0, n_t, tile, init)[2]

    n_want = jnp.where(short, n_adm, k_f)
    n_miss = jnp.sum(jnp.where(n_sel.astype(F32) != n_want, 1, 0))
    n_tiny = sum(jnp.sum(jnp.where(a[HEAD_DIM:HEAD_DIM + 1] >= MIN_SOFTMAX_SUM, 0, 1)) for a in accs)

    def redo():
        thr_b, need_b, _ = bisect_bits()
        return attend_exact(thr_b, need_b)

    miss = jnp.logical_and(n_miss > 0, jnp.logical_not(use_bits))
    accs = lax.cond(jnp.logical_or(miss, n_tiny > 0), redo, lambda: accs)

    for kv in range(N_KV_HEADS):
        o_t = accs[kv][:HEAD_DIM] / accs[kv][HEAD_DIM:HEAD_DIM + 1]
        for gi in range(GROUP):
            hh = GROUP * kv + gi
            bot_ref[hh * HEAD_DIM:(hh + 1) * HEAD_DIM, :] = o_t[:, gi * TQ:(gi + 1) * TQ]
    o_ref[0] = (bot_ref[...].T * jax.nn.silu(zb_ref[0])).astype(BF)


def _attn_call(qt, qit, wit, zb, k_hm, vt, ki_b, *, topk, causal, stream_len=0, n_valid=0):
    b, nq = qt.shape[0], qt.shape[1]
    s = nq * TQ
    n_tiles = vt.shape[1]
    l_all = k_hm.shape[2]
    assert l_all == n_tiles * LT
    tri = (lax.broadcasted_iota(I32, (LT, LT), 0) >= lax.broadcasted_iota(I32, (LT, LT), 1)).astype(BF)
    whole = lambda shape: pl.BlockSpec((1,) + shape, lambda i, j: (i,) + (0,) * len(shape),
                                       pipeline_mode=pl.Buffered(1))
    return pl.pallas_call(
        functools.partial(_attn_kernel, n_tiles=n_tiles, topk=topk, causal=causal,
                          stream_len=stream_len, n_valid=n_valid),
        grid=(b, nq),
        in_specs=[
            pl.BlockSpec((1, 1, N_KV_HEADS, HEAD_DIM, GROUP * TQ), lambda i, j: (i, j, 0, 0, 0)),
            pl.BlockSpec((1, 1, IDX_DIM, N_IDX_HEADS * TQ), lambda i, j: (i, j, 0, 0)),
            pl.BlockSpec((1, N_IDX_HEADS, TQ), lambda i, j: (i, 0, j)),
            pl.BlockSpec((1, TQ, ATT_WIDTH), lambda i, j: (i, j, 0)),
            whole((N_KV_HEADS, l_all, HEAD_DIM)),
            whole((n_tiles, N_KV_HEADS * V_ROWS, LT)),
            whole((l_all, IDX_DIM)),
            _const_spec((LT, LT)),
        ],
        out_specs=pl.BlockSpec((1, TQ, ATT_WIDTH), lambda i, j: (i, j, 0)),
        out_shape=jax.ShapeDtypeStruct((b, s, ATT_WIDTH), BF),
        scratch_shapes=[pltpu.VMEM((n_tiles, LT, TQ), F32), pltpu.VMEM((ATT_WIDTH, TQ), F32),
                        pltpu.VMEM((LT, N_IDX_HEADS * TQ), F32), pltpu.VMEM((LT, N_IDX_HEADS * TQ), F32)],
        compiler_params=pltpu.CompilerParams(
            dimension_semantics=("arbitrary", "arbitrary"), vmem_limit_bytes=VMEM_LIMIT),
        name="attn_prompt" if causal else "attn_sample",
    )(qt, qit, wit, zb, k_hm, vt, ki_b, tri)


def _merge_kernel(x_ref, ma_ref, gb_ref, bg_ref, wpb_ref, wout_ref, fg_ref, y_ref):
    pb = jnp.dot(bg_ref[0], wpb_ref[...], preferred_element_type=F32)
    m = ma_ref[0].astype(F32) + jax.nn.sigmoid(gb_ref[0]) * pb
    xo = x_ref[0] + jnp.dot(m.astype(BF), wout_ref[...], preferred_element_type=F32)
    ms = jnp.mean(xo * xo, axis=-1, keepdims=True)
    y_ref[0] = xo * lax.rsqrt(ms + RMS_EPS) * fg_ref[...]


def _merge_call(x, ma, gb, bg, wts, *, tt):
    b, s, _ = x.shape
    tok = lambda w: pl.BlockSpec((1, tt, w), lambda i, j: (i, j, 0))
    consts = [wts["w_pb"], wts["w_out"], wts["final_g"]]
    return pl.pallas_call(
        _merge_kernel,
        grid=(b, s // tt),
        in_specs=[tok(D_MODEL), tok(D_MODEL), tok(D_MODEL), tok(ATT_WIDTH)]
                 + [_const_spec(c.shape) for c in consts],
        out_specs=tok(D_MODEL),
        out_shape=jax.ShapeDtypeStruct((b, s, D_MODEL), F32),
        compiler_params=pltpu.CompilerParams(
            dimension_semantics=("arbitrary", "arbitrary"), vmem_limit_bytes=VMEM_LIMIT),
        name="merge",
    )(x, ma, gb, bg, *consts)


def _prep_weights(norm_g, w_in, ln_g, ln_b, w_pa, w_pb, w_out, final_g):
    o = 0
    cols = {}
    for name, width in (("u", SGU_WIDTH), ("v", SGU_WIDTH), ("za", SGU_WIDTH), ("q", ATT_WIDTH),
                        ("k", KV_WIDTH), ("vv", KV_WIDTH), ("zb", ATT_WIDTH),
                        ("qi", N_IDX_HEADS * IDX_DIM), ("ki", IDX_DIM), ("wi", N_IDX_HEADS),
                        ("ga", D_MODEL), ("gb", D_MODEL)):
        cols[name] = (o, o + width)
        o += width
    wb = w_in.astype(BF)
    sl = lambda a, z: wb[:, cols[a][0]:cols[z][1]]
    return {
        "norm_g": norm_g.reshape(1, D_MODEL), "w_uvz": sl("u", "za"),
        "w_nat": jnp.concatenate([sl("k", "zb"), sl("ki", "ki")], axis=1),
        "w_g": sl("ga", "gb"),
        "w_t": jnp.concatenate([sl("q", "q"), sl("qi", "qi"), sl("vv", "vv"), sl("wi", "wi")], axis=1).T,
        "ln_g": ln_g.reshape(1, SGU_WIDTH), "ln_b": ln_b.reshape(1, SGU_WIDTH),
        "w_pa": w_pa.astype(BF), "w_pb": w_pb.astype(BF), "w_out": w_out.astype(BF),
        "final_g": final_g.reshape(1, D_MODEL),
    }


def _mix_operands(w_mix, b_pos):
    pairs = w_mix.reshape(SGU_GROUPS // 2, 2, SGU_LEN, SGU_LEN)
    wmix = jnp.concatenate([pairs[:, 0], pairs[:, 1]], axis=-1).astype(BF)
    bmix = jnp.repeat(b_pos, SGU_GDIM, axis=1)
    return wmix, bmix


def kernel(x_prompt, x_sample, cache_k, cache_v, cache_kidx, norm_g, w_in, sgu_ln_g, sgu_ln_b,
           sgu_w, sgu_b, w_pa, w_pb, w_out, final_g):
    b, s, _ = x_prompt.shape
    db, ds, _ = x_sample.shape
    past = cache_k.shape[2]
    assert norm_g.shape[0] == 1 and db * ds == TQ and s % LT == 0
    wts = _prep_weights(norm_g[0], w_in[0], sgu_ln_g[0], sgu_ln_b[0], w_pa[0], w_pb[0], w_out[0],
                        final_g)

    pos = jnp.arange(SGU_LEN)
    chunk_ok = (pos[None, :] // CHUNK) <= (pos[:, None] // CHUNK)
    wmix_p, bmix_p = _mix_operands(jnp.where(chunk_ok[None], sgu_w[0], 0.0), sgu_b[0].T)
    (ma, gb, zb, k, v, ki, k_hm, ki_b, qt, qit, wit, vt) = _proj_call(
        x_prompt, wts, wmix_p, bmix_p, tt=4 * TQ, lt=LT, emit_vn=False)
    bg = _attn_call(qt, qit, wit, zb, k_hm, vt, ki_b, topk=min(TOPK_MAX, s // 4), causal=True)
    y_prompt = _merge_call(x_prompt, ma, gb, bg, wts, tt=512)

    assert past % CHUNK == 0 and ds <= CHUNK
    w_blk = jnp.einsum("ab,gij->gaibj", jnp.eye(db, dtype=F32), sgu_w[0][:, :ds, :ds])
    wmix_s, bmix_s = _mix_operands(w_blk.reshape(SGU_GROUPS, SGU_LEN, SGU_LEN),
                                   jnp.tile(sgu_b[0][:, :ds].T, (db, 1)))
    n_tok = db * ds
    (ma_s, gb_s, zb_s, k_s, v_s, ki_s, _, _, qt_s, qit_s, wit_s, _, vn_s) = _proj_call(
        x_sample.reshape(1, n_tok, D_MODEL), wts, wmix_s, bmix_s, tt=n_tok, lt=n_tok, emit_vn=True)
    n_keys = past + ds
    slab = -(-n_keys * db // LT) * LT // db
    assert (slab * db) % LT == 0 and slab >= n_keys
    cat = lambda c, n, w: jnp.pad(
        jnp.concatenate([c.reshape(db, past, w), n.reshape(db, ds, w)], axis=1),
        ((0, 0), (0, slab - n_keys), (0, 0))).reshape(db * slab, w).astype(BF)
    k_all, v_all, ki_all = (cat(cache_k[0], k_s, KV_WIDTH), cat(cache_v[0], v_s, KV_WIDTH),
                            cat(cache_kidx[0], ki_s, IDX_DIM))
    n_t_s = db * slab // LT
    k_hm_s = k_all.reshape(1, db * slab, N_KV_HEADS, HEAD_DIM).transpose(0, 2, 1, 3)
    vt_s = v_all.reshape(n_t_s, LT, N_KV_HEADS, HEAD_DIM).transpose(0, 2, 3, 1)
    vt_s = jnp.concatenate([vt_s, jnp.ones((n_t_s, N_KV_HEADS, BF16_ROWS, LT), BF)], axis=2)
    vt_s = vt_s.reshape(1, n_t_s, N_KV_HEADS * V_ROWS, LT)
    bg_s = _attn_call(qt_s, qit_s, wit_s, zb_s, k_hm_s, vt_s, ki_all[None],
                      topk=min(TOPK_MAX, n_keys // 4), causal=False, stream_len=ds, n_valid=n_keys)
    y_sample = _merge_call(x_sample.reshape(1, n_tok, D_MODEL), ma_s, gb_s, bg_s, wts, tt=n_tok)

    return (y_prompt, y_sample.reshape(db, ds, D_MODEL),
            k.reshape(1, b, s, N_KV_HEADS, HEAD_DIM), v.reshape(1, b, s, N_KV_HEADS, HEAD_DIM),
            ki.reshape(1, b, s, IDX_DIM),
            k_s.reshape(1, db, ds, N_KV_HEADS, HEAD_DIM), v_s.reshape(1, db, ds, N_KV_HEADS, HEAD_DIM),
            ki_s.reshape(1, db, ds, IDX_DIM), vn_s.reshape(1, db, ds, SGU_WIDTH))
```

```python
import functools

import jax
import jax.numpy as jnp
from jax import lax
from jax.experimental import pallas as pl
from jax.experimental.pallas import tpu as pltpu

D_MODEL = 1024
CHUNK = 64
SGU_LEN = 128
SGU_GROUPS = 8
SGU_WIDTH = 512
SGU_GDIM = SGU_WIDTH // SGU_GROUPS
N_HEADS = 8
N_KV_HEADS = 4
HEAD_DIM = 64
GROUP = N_HEADS // N_KV_HEADS
ATT_WIDTH = N_HEADS * HEAD_DIM
KV_WIDTH = N_KV_HEADS * HEAD_DIM
N_IDX_HEADS = 8
IDX_DIM = 64
TOPK_MAX = 256
RMS_EPS = 1e-6
LN_EPS = 1e-5

BF = jnp.bfloat16
F32 = jnp.float32
I32 = jnp.int32

LANES = 128
SUBLANES = 8
BF16_ROWS = 16
TQ = LANES
LT = 512
V_ROWS = HEAD_DIM + BF16_ROWS
VMEM_LIMIT = 52 * 1024 * 1024
MASKED_LOGIT = -1e30
LOWEST = -3.4028235e38
MIN_SOFTMAX_SUM = 1e-30
ALL_TIES = 3e38
UNCHECKED_SEARCH_PASSES = 8
MAX_SEARCH_PASSES = 28
MAX_WALK_STEPS = 8
SEARCH_CLIP = 0.05
LOG2_E = 1.4426950408889634
INT_MIN = -2147483648


def _const_spec(shape):
    nd = len(shape)
    return pl.BlockSpec(shape, lambda *_: (0,) * nd, pipeline_mode=pl.Buffered(1))


def _proj_kernel(x_ref, ng_ref, wuvz_ref, wnat_ref, wg_ref, wt_ref,
                 lng_ref, lnb_ref, wmix_ref, bmix_ref, wpa_ref,
                 ma_ref, gb_ref, zb_ref, k_ref, v_ref, ki_ref, khm_ref, kib_ref,
                 qt_ref, qit_ref, wit_ref, vt_ref, *vn_refs, n_sub):
    x = x_ref[0]
    ms = jnp.mean(x * x, axis=-1, keepdims=True)
    h = (x * lax.rsqrt(ms + RMS_EPS) * ng_ref[...]).astype(BF)

    uvz = jnp.dot(h, wuvz_ref[...], preferred_element_type=F32)
    u = jax.nn.gelu(uvz[:, :SGU_WIDTH])
    v = jax.nn.gelu(uvz[:, SGU_WIDTH:2 * SGU_WIDTH])
    za = uvz[:, 2 * SGU_WIDTH:]
    mu = jnp.mean(v, axis=-1, keepdims=True)
    var = jnp.mean(jnp.square(v - mu), axis=-1, keepdims=True)
    vn = (v - mu) * lax.rsqrt(var + LN_EPS) * lng_ref[...] + lnb_ref[...]
    if vn_refs:
        vn_refs[0][0] = vn
    lane = lax.broadcasted_iota(I32, (SGU_LEN, LANES), 1)
    cols = []
    for c in range(SGU_WIDTH // LANES):
        rows = []
        for s in range(n_sub):
            blk = vn[s * SGU_LEN:(s + 1) * SGU_LEN, c * LANES:(c + 1) * LANES]
            lo = jnp.where(lane < SGU_GDIM, blk, 0.0).astype(BF)
            hi = jnp.where(lane >= SGU_GDIM, blk, 0.0).astype(BF)
            rhs = jnp.concatenate([lo, hi], axis=0)
            rows.append(jnp.dot(wmix_ref[c], rhs, preferred_element_type=F32))
        cols.append(jnp.concatenate(rows, axis=0) if n_sub > 1 else rows[0])
    mixed = jnp.concatenate(cols, axis=1)
    bias = bmix_ref[...]
    if n_sub > 1:
        bias = jnp.concatenate([bias] * n_sub, axis=0)
    a_out = u * (mixed + bias) * jax.nn.silu(za)
    pa = jnp.dot(a_out.astype(BF), wpa_ref[...], preferred_element_type=F32)
    g = jnp.dot(h, wg_ref[...], preferred_element_type=F32)
    ma_ref[0] = (jax.nn.sigmoid(g[:, :D_MODEL]) * pa).astype(BF)
    gb_ref[0] = g[:, D_MODEL:]

    nat = jnp.dot(h, wnat_ref[...], preferred_element_type=F32)
    k = nat[:, :KV_WIDTH]
    k_ref[0] = k
    v_ref[0] = nat[:, KV_WIDTH:2 * KV_WIDTH]
    zb_ref[0] = nat[:, 2 * KV_WIDTH:2 * KV_WIDTH + ATT_WIDTH]
    ki = nat[:, 2 * KV_WIDTH + ATT_WIDTH:]
    ki_ref[0] = ki
    kib_ref[0] = ki.astype(BF)
    for kv in range(N_KV_HEADS):
        khm_ref[0, kv] = k[:, kv * HEAD_DIM:(kv + 1) * HEAD_DIM].astype(BF)

    tm = lax.dot_general(wt_ref[...], h, (((1,), (1,)), ((), ())),
                         preferred_element_type=F32)
    o_qi, o_v, o_wi = ATT_WIDTH, 2 * ATT_WIDTH, 2 * ATT_WIDTH + KV_WIDTH
    for jb in range(n_sub):
        tok = slice(jb * TQ, (jb + 1) * TQ)
        for hh in range(N_HEADS):
            kv, gi = divmod(hh, GROUP)
            qt_ref[0, jb, kv, :, gi * TQ:(gi + 1) * TQ] = (
                tm[hh * HEAD_DIM:(hh + 1) * HEAD_DIM, tok] * (HEAD_DIM ** -0.5 * LOG2_E)).astype(BF)
        for hh in range(N_IDX_HEADS):
            qit_ref[0, jb, :, hh * TQ:(hh + 1) * TQ] = (
                tm[o_qi + hh * IDX_DIM:o_qi + (hh + 1) * IDX_DIM, tok] * (IDX_DIM ** -0.5)).astype(BF)
    for kv in range(N_KV_HEADS):
        vt_ref[0, 0, kv * V_ROWS:kv * V_ROWS + HEAD_DIM, :] = (
            tm[o_v + kv * HEAD_DIM:o_v + (kv + 1) * HEAD_DIM, :].astype(BF))
        vt_ref[0, 0, kv * V_ROWS + HEAD_DIM:(kv + 1) * V_ROWS, :] = jnp.ones(
            (BF16_ROWS, tm.shape[1]), BF)
    wit_ref[0] = tm[o_wi:o_wi + N_IDX_HEADS, :] * (N_IDX_HEADS ** -0.5)


def _proj_call(x, wts, wmix, bmix, *, tt, lt, emit_vn):
    b, s, _ = x.shape
    n_sub = tt // SGU_LEN
    r = lt // tt
    grid = (b, s // tt)
    tok = lambda w, dt=F32: (jax.ShapeDtypeStruct((b, s, w), dt),
                             pl.BlockSpec((1, tt, w), lambda i, j: (i, j, 0)))
    outs = [
        tok(D_MODEL, BF),
        tok(D_MODEL),
        tok(ATT_WIDTH),
        tok(KV_WIDTH), tok(KV_WIDTH), tok(IDX_DIM),
        (jax.ShapeDtypeStruct((b, N_KV_HEADS, s, HEAD_DIM), BF),
         pl.BlockSpec((1, N_KV_HEADS, tt, HEAD_DIM), lambda i, j: (i, 0, j, 0))),
        tok(IDX_DIM, BF),
        (jax.ShapeDtypeStruct((b, s // TQ, N_KV_HEADS, HEAD_DIM, GROUP * TQ), BF),
         pl.BlockSpec((1, n_sub, N_KV_HEADS, HEAD_DIM, GROUP * TQ), lambda i, j: (i, j, 0, 0, 0))),
        (jax.ShapeDtypeStruct((b, s // TQ, IDX_DIM, N_IDX_HEADS * TQ), BF),
         pl.BlockSpec((1, n_sub, IDX_DIM, N_IDX_HEADS * TQ), lambda i, j: (i, j, 0, 0))),
        (jax.ShapeDtypeStruct((b, N_IDX_HEADS, s), F32),
         pl.BlockSpec((1, N_IDX_HEADS, tt), lambda i, j: (i, 0, j))),
        (jax.ShapeDtypeStruct((b, s // lt, N_KV_HEADS * V_ROWS, lt), BF),
         pl.BlockSpec((1, 1, N_KV_HEADS * V_ROWS, tt), lambda i, j: (i, j // r, 0, j % r))),
    ]
    if emit_vn:
        outs.append(tok(SGU_WIDTH))
    consts = [wts["norm_g"], wts["w_uvz"], wts["w_nat"], wts["w_g"], wts["w_t"],
              wts["ln_g"], wts["ln_b"], wmix, bmix, wts["w_pa"]]
    return pl.pallas_call(
        functools.partial(_proj_kernel, n_sub=n_sub),
        grid=grid,
        in_specs=[pl.BlockSpec((1, tt, D_MODEL), lambda i, j: (i, j, 0))]
                 + [_const_spec(c.shape) for c in consts],
        out_specs=[o[1] for o in outs],
        out_shape=[o[0] for o in outs],
        compiler_params=pltpu.CompilerParams(
            dimension_semantics=("arbitrary", "arbitrary"), vmem_limit_bytes=VMEM_LIMIT),
        name="proj_vn" if emit_vn else "proj",
    )(x, *consts)


def _key_to_f32(key_u):
    s = key_u ^ INT_MIN
    bits = s ^ (lax.shift_right_arithmetic(s, 31) & 0x7FFFFFFF)
    return lax.bitcast_convert_type(bits, F32)


def _f32_to_key(x):
    bits = lax.bitcast_convert_type(x, I32)
    return bits ^ (lax.shift_right_arithmetic(bits, 31) & 0x7FFFFFFF) ^ INT_MIN


def _attn_kernel(qt_ref, qit_ref, wit_ref, zb_ref, k_ref, vt_ref, ki_ref, tri_ref,
                 o_ref, sc_ref, bot_ref, mma_ref, mmb_ref, *, n_tiles, topk, causal, stream_len,
                 n_valid):
    lane = lax.broadcasted_iota(I32, (1, TQ), 1)
    if causal:
        q0 = pl.program_id(1) * TQ
        n_t = lax.div(q0 + TQ + LT - 1, LT)
        key_lo = jnp.zeros((1, TQ), I32)
        key_hi = (lax.shift_right_logical(q0 + lane, 6) + 1) * CHUNK
    else:
        n_t = n_tiles
        slab = (n_tiles * LT) // (TQ // stream_len)
        key_lo = lax.div(lane, stream_len) * slab
        key_hi = key_lo + n_valid

    qit = qit_ref[0, 0]
    wit = wit_ref[0]
    wrows = [wit[hh:hh + 1, :] for hh in range(N_IDX_HEADS)]
    row0 = lax.broadcasted_iota(I32, (LT, 1), 0)

    fold = lambda x, op: op(x.reshape(LT // SUBLANES, SUBLANES, TQ), axis=0)

    last = n_t - 1
    n_pairs = lax.div(n_t, 2)
    odd = n_t - 2 * n_pairs == 1
    keep = lambda c: c

    def idx_dot(t, buf):
        start = pl.multiple_of(t * LT, LT)
        buf[...] = jnp.dot(ki_ref[0, pl.ds(start, LT), :], qit, preferred_element_type=F32)

    def score_tile(t, buf, carry):
        mx, mn, n_nonneg, n_pos = carry
        sc = jnp.zeros((LT, TQ), F32)
        for hh in range(N_IDX_HEADS):
            sc = sc + jnp.maximum(buf[:, hh * TQ:(hh + 1) * TQ], 0.0) * wrows[hh]
        pos = row0 + t * LT
        ok = pos < key_hi if causal else (pos >= key_lo) & (pos < key_hi)
        mn = jnp.minimum(mn, fold(sc, jnp.min))
        sc = jnp.where(ok, sc, -jnp.inf)
        sc_ref[t] = sc
        ones = lambda m: fold(jnp.where(m, 1, 0).astype(I32), jnp.sum)
        return (jnp.maximum(mx, fold(sc, jnp.max)), mn,
                n_nonneg + ones(sc >= 0.0), n_pos + ones(sc > 0.0))

    def score_pair(i, carry):
        t0 = 2 * i
        idx_dot(t0 + 1, mmb_ref)
        carry = score_tile(t0, mma_ref, carry)
        idx_dot(jnp.minimum(t0 + 2, last), mma_ref)
        return score_tile(t0 + 1, mmb_ref, carry)

    idx_dot(0, mma_ref)
    stats = lax.fori_loop(
        0, n_pairs, score_pair,
        (jnp.full((SUBLANES, TQ), -jnp.inf, F32), jnp.full((SUBLANES, TQ), jnp.inf, F32),
         jnp.zeros((SUBLANES, TQ), I32), jnp.zeros((SUBLANES, TQ), I32)))
    mx, mn, n_nonneg, n_pos = lax.cond(
        odd, lambda c: score_tile(last, mma_ref, c), keep, stats)
    mx = jnp.max(mx, axis=0, keepdims=True)
    mn = jnp.min(mn, axis=0, keepdims=True)
    n_nonneg = jnp.sum(n_nonneg, axis=0, keepdims=True).astype(F32)
    n_pos = jnp.sum(n_pos, axis=0, keepdims=True).astype(F32)

    def count(pred):
        ones = lambda t: fold(jnp.where(pred(sc_ref[t]), 1, 0).astype(I32), jnp.sum)

        acc = lax.fori_loop(0, n_pairs, lambda i, a: a + ones(2 * i) + ones(2 * i + 1),
                            jnp.zeros((SUBLANES, TQ), I32))
        acc = lax.cond(odd, lambda a: a + ones(last), keep, acc)
        return jnp.sum(acc, axis=0, keepdims=True)

    k_f = float(topk)
    n_adm = (key_hi - key_lo).astype(F32)
    short = n_adm < k_f

    def bisect_bits():
        def bit_pass(i, carry):
            thr_u, n_ge = carry
            cand_u = thr_u | lax.shift_left(jnp.int32(1), 31 - i)
            cand = _key_to_f32(cand_u)
            cnt = count(lambda sc: sc >= cand)
            take = cnt >= topk
            return jnp.where(take, cand_u, thr_u), jnp.where(take, cnt, n_ge)

        thr_u, n_ge = lax.fori_loop(
            0, 32, bit_pass, (jnp.zeros((1, TQ), I32), jnp.zeros((1, TQ), I32)))
        thr = jnp.where(short, LOWEST, _key_to_f32(thr_u))
        n_gt = count(lambda sc: sc > thr)
        need = jnp.where(short, ALL_TIES, (topk - n_gt).astype(F32))
        tie = jnp.logical_and(jnp.logical_not(short), n_ge > topk)
        return thr, need, jnp.max(jnp.where(tie, 1, 0))

    count_f = lambda pred: count(pred).astype(F32)
    zero_thr = jnp.logical_and(n_pos < k_f, n_nonneg >= k_f)
    zero_tie = jnp.logical_and(zero_thr, n_nonneg > k_f)
    n_zero_tie = jnp.max(jnp.where(zero_tie, 1, 0))
    pos_side = n_pos >= k_f
    lo0 = jnp.where(pos_side, 0.0, mn)
    hi0 = jnp.where(pos_side, mx, 0.0)
    clo0 = jnp.where(pos_side, n_nonneg, n_adm)
    chi0 = jnp.where(pos_side, 0.0, n_nonneg)
    logk = jnp.log(k_f)
    f_of = lambda c: jnp.log(jnp.maximum(c, 0.5)) - logk
    done0 = jnp.where(short | zero_thr | (clo0 == k_f), 1, 0)

    def search_pass(st):
        lo, hi, clo, chi, flo, fhi, side, done = st
        frac = jnp.clip(flo / (flo - fhi), SEARCH_CLIP, 1.0 - SEARCH_CLIP)
        cand = lo + (hi - lo) * frac
        cnt = count_f(lambda sc: sc >= cand)
        ge = cnt >= k_f
        up_lo = jnp.logical_and(done == 0, ge)
        up_hi = jnp.logical_and(done == 0, jnp.logical_not(ge))
        fhi = jnp.where(up_lo & (side == 1), fhi * 0.5, fhi)
        flo = jnp.where(up_hi & (side == -1), flo * 0.5, flo)
        lo = jnp.where(up_lo, cand, lo)
        clo = jnp.where(up_lo, cnt, clo)
        flo = jnp.where(up_lo, f_of(cnt), flo)
        hi = jnp.where(up_hi, cand, hi)
        chi = jnp.where(up_hi, cnt, chi)
        fhi = jnp.where(up_hi, f_of(cnt), fhi)
        side = jnp.where(up_lo, 1, jnp.where(up_hi, -1, side))
        done = jnp.where(clo == k_f, 1, done)
        return lo, hi, clo, chi, flo, fhi, side, done

    def walk_plan(st):
        lo, _, clo, chi = st[:4]
        from_hi = k_f - chi
        from_lo = clo - k_f + 1.0
        down = jnp.logical_or(from_hi <= from_lo, lo == 0.0)
        return down, jnp.where(st[7] == 1, 0.0, jnp.where(down, from_hi, from_lo))

    most_steps = lambda st: jnp.max(walk_plan(st)[1]).astype(I32)

    st = lax.fori_loop(0, UNCHECKED_SEARCH_PASSES, lambda i, s: search_pass(s),
                       (lo0, hi0, clo0, chi0, f_of(clo0), f_of(chi0), jnp.zeros((1, TQ), I32), done0))

    def checked_pass(c):
        s = search_pass(c[2])
        return c[0] + 1, most_steps(s), s

    _, n_steps, st = lax.while_loop(
        lambda c: jnp.logical_and(c[0] < MAX_SEARCH_PASSES, c[1] > MAX_WALK_STEPS), checked_pass,
        (jnp.int32(UNCHECKED_SEARCH_PASSES), most_steps(st), st))

    def walked():
        lo, hi = st[0], st[1]
        down, steps = walk_plan(st)
        sign = jnp.where(down, 1.0, -1.0)
        bound0 = jnp.where(down, hi, _key_to_f32(_f32_to_key(-lo) + 1))

        def step(i, bound):
            def below(t):
                y = sc_ref[t] * sign
                return fold(jnp.where(y < bound, y, -jnp.inf), jnp.max)

            nxt = lax.fori_loop(
                0, n_pairs, lambda j, a: jnp.maximum(a, jnp.maximum(below(2 * j), below(2 * j + 1))),
                jnp.full((SUBLANES, TQ), -jnp.inf, F32))
            nxt = lax.cond(odd, lambda a: jnp.maximum(a, below(last)), keep, nxt)
            return jnp.where(steps > i.astype(F32), jnp.max(nxt, axis=0, keepdims=True), bound)

        found = lax.fori_loop(0, n_steps, step, bound0) * sign
        thr = jnp.where(short, LOWEST,
                        jnp.where(zero_thr, 0.0, jnp.where(st[7] == 1, lo, found)))
        return thr, jnp.where(zero_tie, k_f - n_pos, ALL_TIES), n_zero_tie

    use_bits = n_steps > MAX_WALK_STEPS
    thr, need, n_tie = lax.cond(use_bits, bisect_bits, walked)

    gw = GROUP * TQ

    def qk_dots(t, buf):
        start = pl.multiple_of(t * LT, LT)
        for kv in range(N_KV_HEADS):
            buf[:, kv * gw:(kv + 1) * gw] = jnp.dot(
                k_ref[0, kv, pl.ds(start, LT), :], qt_ref[0, 0, kv], preferred_element_type=F32)

    def attend(with_ties, thr, need):
        def attend_tile(t, buf, carry):
            n_eq, n_sel, ms, accs = carry
            sc = sc_ref[t]
            if with_ties:
                eq = sc == thr
                pref = jnp.dot(tri_ref[...], jnp.where(eq, 1.0, 0.0).astype(BF),
                               preferred_element_type=F32)
                sel = (sc > thr) | (eq & (n_eq + pref - 1.0 < need))
                n_eq = n_eq + pref[LT - 1:LT, :]
            else:
                sel = sc >= thr
            n_sel = n_sel + fold(jnp.where(sel, 1, 0).astype(I32), jnp.sum)
            keep01 = jnp.where(sel, 1.0, 0.0).astype(BF)
            keep01 = jnp.concatenate([keep01] * GROUP, axis=1)
            new_m, ps, alphas = [], [], []
            for kv in range(N_KV_HEADS):
                lg = buf[:, kv * gw:(kv + 1) * gw]
                m_new = jnp.maximum(ms[kv], jnp.max(lg, axis=0, keepdims=True))
                ps.append(jnp.exp2(lg - m_new).astype(BF) * keep01)
                alphas.append(jnp.exp2(ms[kv] - m_new))
                new_m.append(m_new)
            new_acc = []
            for kv in range(N_KV_HEADS):
                pv = jnp.dot(vt_ref[0, t, kv * V_ROWS:(kv + 1) * V_ROWS, :], ps[kv],
                             preferred_element_type=F32)
                new_acc.append(alphas[kv] * accs[kv] + pv)
            return n_eq, n_sel, tuple(new_m), tuple(new_acc)

        def attend_pair(i, carry):
            t0 = 2 * i
            qk_dots(t0 + 1, mmb_ref)
            carry = attend_tile(t0, mma_ref, carry)
            qk_dots(jnp.minimum(t0 + 2, last), mma_ref)
            return attend_tile(t0 + 1, mmb_ref, carry)

        init = (jnp.zeros((1, TQ), F32), jnp.zeros((SUBLANES, TQ), I32),
                tuple(jnp.full((1, gw), MASKED_LOGIT, F32) for _ in range(N_KV_HEADS)),
                tuple(jnp.zeros((V_ROWS, gw), F32) for _ in range(N_KV_HEADS)))
        qk_dots(0, mma_ref)
        carry = lax.fori_loop(0, n_pairs, attend_pair, init)
        _, n_sel, _, accs = lax.cond(odd, lambda c: attend_tile(last, mma_ref, c), keep, carry)
        return accs, jnp.sum(n_sel, axis=0, keepdims=True)

    accs, n_sel = lax.cond(n_tie > 0, lambda: attend(True, thr, need),
                           lambda: attend(False, thr, need))

    def attend_exact(thr, need):
        def tile(t, carry):
            n_eq, ms, accs = carry
            sc = sc_ref[t]
            eq = sc == thr
            pref = jnp.dot(tri_ref[...], jnp.where(eq, 1.0, 0.0).astype(BF),
                           preferred_element_type=F32)
            sel = (sc > thr) | (eq & (n_eq + pref - 1.0 < need))
            n_eq = n_eq + pref[LT - 1:LT, :]
            bias = jnp.concatenate([jnp.where(sel, 0.0, MASKED_LOGIT)] * GROUP, axis=1)
            start = pl.multiple_of(t * LT, LT)
            new_m, new_acc = [], []
            for kv in range(N_KV_HEADS):
                lg = jnp.dot(k_ref[0, kv, pl.ds(start, LT), :], qt_ref[0, 0, kv],
                             preferred_element_type=F32) + bias
                m_new = jnp.maximum(ms[kv], jnp.max(lg, axis=0, keepdims=True))
                p = jnp.exp2(lg - m_new).astype(BF)
                pv = jnp.dot(vt_ref[0, t, kv * V_ROWS:(kv + 1) * V_ROWS, :], p,
                             preferred_element_type=F32)
                new_acc.append(jnp.exp2(ms[kv] - m_new) * accs[kv] + pv)
                new_m.append(m_new)
            return n_eq, tuple(new_m), tuple(new_acc)

        init = (jnp.zeros((1, TQ), F32),
                tuple(jnp.full((1, gw), MASKED_LOGIT, F32) for _ in range(N_KV_HEADS)),
                tuple(jnp.zeros((V_ROWS, gw), F32) for _ in range(N_KV_HEADS)))
        return lax.fori_loop(0, n_t, tile, init)[2]

    n_want = jnp.where(short, n_adm, k_f)
    bad = jnp.where(jnp.logical_or(n_sel.astype(F32) == n_want, use_bits), 0, 1)
    for a in accs:
        tiny = jnp.where(a[HEAD_DIM:HEAD_DIM + 1] >= MIN_SOFTMAX_SUM, 0, 1)
        bad = bad + sum(tiny[:, gi * TQ:(gi + 1) * TQ] for gi in range(GROUP))

    def redo():
        thr_b, need_b, _ = bisect_bits()
        return attend_exact(thr_b, need_b)

    accs = lax.cond(jnp.sum(bad) > 0, redo, lambda: accs)

    for kv in range(N_KV_HEADS):
        o_t = accs[kv][:HEAD_DIM] / accs[kv][HEAD_DIM:HEAD_DIM + 1]
        for gi in range(GROUP):
            hh = GROUP * kv + gi
            bot_ref[hh * HEAD_DIM:(hh + 1) * HEAD_DIM, :] = o_t[:, gi * TQ:(gi + 1) * TQ]
    o_ref[0] = (bot_ref[...].T * jax.nn.silu(zb_ref[0])).astype(BF)


def _attn_call(qt, qit, wit, zb, k_hm, vt, ki_b, *, topk, causal, stream_len=0, n_valid=0):
    b, nq = qt.shape[0], qt.shape[1]
    s = nq * TQ
    n_tiles = vt.shape[1]
    l_all = k_hm.shape[2]
    assert l_all == n_tiles * LT
    tri = (lax.broadcasted_iota(I32, (LT, LT), 0) >= lax.broadcasted_iota(I32, (LT, LT), 1)).astype(BF)
    whole = lambda shape: pl.BlockSpec((1,) + shape, lambda i, j: (i,) + (0,) * len(shape),
                                       pipeline_mode=pl.Buffered(1))
    return pl.pallas_call(
        functools.partial(_attn_kernel, n_tiles=n_tiles, topk=topk, causal=causal,
                          stream_len=stream_len, n_valid=n_valid),
        grid=(b, nq),
        in_specs=[
            pl.BlockSpec((1, 1, N_KV_HEADS, HEAD_DIM, GROUP * TQ), lambda i, j: (i, j, 0, 0, 0)),
            pl.BlockSpec((1, 1, IDX_DIM, N_IDX_HEADS * TQ), lambda i, j: (i, j, 0, 0)),
            pl.BlockSpec((1, N_IDX_HEADS, TQ), lambda i, j: (i, 0, j)),
            pl.BlockSpec((1, TQ, ATT_WIDTH), lambda i, j: (i, j, 0)),
            whole((N_KV_HEADS, l_all, HEAD_DIM)),
            whole((n_tiles, N_KV_HEADS * V_ROWS, LT)),
            whole((l_all, IDX_DIM)),
            _const_spec((LT, LT)),
        ],
        out_specs=pl.BlockSpec((1, TQ, ATT_WIDTH), lambda i, j: (i, j, 0)),
        out_shape=jax.ShapeDtypeStruct((b, s, ATT_WIDTH), BF),
        scratch_shapes=[pltpu.VMEM((n_tiles, LT, TQ), F32), pltpu.VMEM((ATT_WIDTH, TQ), F32),
                        pltpu.VMEM((LT, N_IDX_HEADS * TQ), F32), pltpu.VMEM((LT, N_IDX_HEADS * TQ), F32)],
        compiler_params=pltpu.CompilerParams(
            dimension_semantics=("arbitrary", "arbitrary"), vmem_limit_bytes=VMEM_LIMIT),
        name="attn_prompt" if causal else "attn_sample",
    )(qt, qit, wit, zb, k_hm, vt, ki_b, tri)


def _merge_kernel(x_ref, ma_ref, gb_ref, bg_ref, wpb_ref, wout_ref, fg_ref, y_ref):
    pb = jnp.dot(bg_ref[0], wpb_ref[...], preferred_element_type=F32)
    m = ma_ref[0].astype(F32) + jax.nn.sigmoid(gb_ref[0]) * pb
    xo = x_ref[0] + jnp.dot(m.astype(BF), wout_ref[...], preferred_element_type=F32)
    ms = jnp.mean(xo * xo, axis=-1, keepdims=True)
    y_ref[0] = xo * lax.rsqrt(ms + RMS_EPS) * fg_ref[...]


def _merge_call(x, ma, gb, bg, wts, *, tt):
    b, s, _ = x.shape
    tok = lambda w: pl.BlockSpec((1, tt, w), lambda i, j: (i, j, 0))
    consts = [wts["w_pb"], wts["w_out"], wts["final_g"]]
    return pl.pallas_call(
        _merge_kernel,
        grid=(b, s // tt),
        in_specs=[tok(D_MODEL), tok(D_MODEL), tok(D_MODEL), tok(ATT_WIDTH)]
                 + [_const_spec(c.shape) for c in consts],
        out_specs=tok(D_MODEL),
        out_shape=jax.ShapeDtypeStruct((b, s, D_MODEL), F32),
        compiler_params=pltpu.CompilerParams(
            dimension_semantics=("arbitrary", "arbitrary"), vmem_limit_bytes=VMEM_LIMIT),
        name="merge",
    )(x, ma, gb, bg, *consts)


def _prep_weights(norm_g, w_in, ln_g, ln_b, w_pa, w_pb, w_out, final_g):
    o = 0
    cols = {}
    for name, width in (("u", SGU_WIDTH), ("v", SGU_WIDTH), ("za", SGU_WIDTH), ("q", ATT_WIDTH),
                        ("k", KV_WIDTH), ("vv", KV_WIDTH), ("zb", ATT_WIDTH),
                        ("qi", N_IDX_HEADS * IDX_DIM), ("ki", IDX_DIM), ("wi", N_IDX_HEADS),
                        ("ga", D_MODEL), ("gb", D_MODEL)):
        cols[name] = (o, o + width)
        o += width
    wb = w_in.astype(BF)
    sl = lambda a, z: wb[:, cols[a][0]:cols[z][1]]
    return {
        "norm_g": norm_g.reshape(1, D_MODEL), "w_uvz": sl("u", "za"),
        "w_nat": jnp.concatenate([sl("k", "zb"), sl("ki", "ki")], axis=1),
        "w_g": sl("ga", "gb"),
        "w_t": jnp.concatenate([sl("q", "q"), sl("qi", "qi"), sl("vv", "vv"), sl("wi", "wi")], axis=1).T,
        "ln_g": ln_g.reshape(1, SGU_WIDTH), "ln_b": ln_b.reshape(1, SGU_WIDTH),
        "w_pa": w_pa.astype(BF), "w_pb": w_pb.astype(BF), "w_out": w_out.astype(BF),
        "final_g": final_g.reshape(1, D_MODEL),
    }


def _mix_operands(w_mix, b_pos):
    pairs = w_mix.reshape(SGU_GROUPS // 2, 2, SGU_LEN, SGU_LEN)
    wmix = jnp.concatenate([pairs[:, 0], pairs[:, 1]], axis=-1).astype(BF)
    bmix = jnp.repeat(b_pos, SGU_GDIM, axis=1)
    return wmix, bmix


def kernel(x_prompt, x_sample, cache_k, cache_v, cache_kidx, norm_g, w_in, sgu_ln_g, sgu_ln_b,
           sgu_w, sgu_b, w_pa, w_pb, w_out, final_g):
    b, s, _ = x_prompt.shape
    db, ds, _ = x_sample.shape
    past = cache_k.shape[2]
    assert norm_g.shape[0] == 1 and db * ds == TQ and s % LT == 0
    wts = _prep_weights(norm_g[0], w_in[0], sgu_ln_g[0], sgu_ln_b[0], w_pa[0], w_pb[0], w_out[0],
                        final_g)

    pos = jnp.arange(SGU_LEN)
    chunk_ok = (pos[None, :] // CHUNK) <= (pos[:, None] // CHUNK)
    wmix_p, bmix_p = _mix_operands(jnp.where(chunk_ok[None], sgu_w[0], 0.0), sgu_b[0].T)
    (ma, gb, zb, k, v, ki, k_hm, ki_b, qt, qit, wit, vt) = _proj_call(
        x_prompt, wts, wmix_p, bmix_p, tt=4 * TQ, lt=LT, emit_vn=False)
    bg = _attn_call(qt, qit, wit, zb, k_hm, vt, ki_b, topk=min(TOPK_MAX, s // 4), causal=True)
    y_prompt = _merge_call(x_prompt, ma, gb, bg, wts, tt=512)

    assert past % CHUNK == 0 and ds <= CHUNK
    w_blk = jnp.einsum("ab,gij->gaibj", jnp.eye(db, dtype=F32), sgu_w[0][:, :ds, :ds])
    wmix_s, bmix_s = _mix_operands(w_blk.reshape(SGU_GROUPS, SGU_LEN, SGU_LEN),
                                   jnp.tile(sgu_b[0][:, :ds].T, (db, 1)))
    n_tok = db * ds
    (ma_s, gb_s, zb_s, k_s, v_s, ki_s, _, _, qt_s, qit_s, wit_s, _, vn_s) = _proj_call(
        x_sample.reshape(1, n_tok, D_MODEL), wts, wmix_s, bmix_s, tt=n_tok, lt=n_tok, emit_vn=True)
    n_keys = past + ds
    slab = -(-n_keys * db // LT) * LT // db
    assert (slab * db) % LT == 0 and slab >= n_keys
    cat = lambda c, n, w: jnp.pad(
        jnp.concatenate([c.reshape(db, past, w), n.reshape(db, ds, w)], axis=1),
        ((0, 0), (0, slab - n_keys), (0, 0))).reshape(db * slab, w).astype(BF)
    k_all, v_all, ki_all = (cat(cache_k[0], k_s, KV_WIDTH), cat(cache_v[0], v_s, KV_WIDTH),
                            cat(cache_kidx[0], ki_s, IDX_DIM))
    n_t_s = db * slab // LT
    k_hm_s = k_all.reshape(1, db * slab, N_KV_HEADS, HEAD_DIM).transpose(0, 2, 1, 3)
    vt_s = v_all.reshape(n_t_s, LT, N_KV_HEADS, HEAD_DIM).transpose(0, 2, 3, 1)
    vt_s = jnp.concatenate([vt_s, jnp.ones((n_t_s, N_KV_HEADS, BF16_ROWS, LT), BF)], axis=2)
    vt_s = vt_s.reshape(1, n_t_s, N_KV_HEADS * V_ROWS, LT)
    bg_s = _attn_call(qt_s, qit_s, wit_s, zb_s, k_hm_s, vt_s, ki_all[None],
                      topk=min(TOPK_MAX, n_keys // 4), causal=False, stream_len=ds, n_valid=n_keys)
    y_sample = _merge_call(x_sample.reshape(1, n_tok, D_MODEL), ma_s, gb_s, bg_s, wts, tt=n_tok)

    return (y_prompt, y_sample.reshape(db, ds, D_MODEL),
            k.reshape(1, b, s, N_KV_HEADS, HEAD_DIM), v.reshape(1, b, s, N_KV_HEADS, HEAD_DIM),
            ki.reshape(1, b, s, IDX_DIM),
            k_s.reshape(1, db, ds, N_KV_HEADS, HEAD_DIM), v_s.reshape(1, db, ds, N_KV_HEADS, HEAD_DIM),
            ki_s.reshape(1, db, ds, IDX_DIM), vn_s.reshape(1, db, ds, SGU_WIDTH))
```

```python
import functools

import jax
import jax.numpy as jnp
from jax import lax
from jax.experimental import pallas as pl
from jax.experimental.pallas import tpu as pltpu

D_MODEL = 1024
CHUNK = 64
SGU_LEN = 128
SGU_GROUPS = 8
SGU_WIDTH = 512
SGU_GDIM = SGU_WIDTH // SGU_GROUPS
N_HEADS = 8
N_KV_HEADS = 4
HEAD_DIM = 64
GROUP = N_HEADS // N_KV_HEADS
ATT_WIDTH = N_HEADS * HEAD_DIM
KV_WIDTH = N_KV_HEADS * HEAD_DIM
N_IDX_HEADS = 8
IDX_DIM = 64
TOPK_MAX = 256
RMS_EPS = 1e-6
LN_EPS = 1e-5

BF = jnp.bfloat16
F32 = jnp.float32
I32 = jnp.int32

LANES = 128
SUBLANES = 8
BF16_ROWS = 16
TQ = LANES
LT = 512
V_ROWS = HEAD_DIM + BF16_ROWS
VMEM_LIMIT = 52 * 1024 * 1024
MASKED_LOGIT = -1e30
LOWEST = -3.4028235e38
MIN_SOFTMAX_SUM = 1e-30
ALL_TIES = 3e38
UNCHECKED_SEARCH_PASSES = 8
MAX_SEARCH_PASSES = 28
MAX_WALK_STEPS = 8
SEARCH_CLIP = 0.05
LOG2_E = 1.4426950408889634
INT_MIN = -2147483648


def _const_spec(shape):
    nd = len(shape)
    return pl.BlockSpec(shape, lambda *_: (0,) * nd, pipeline_mode=pl.Buffered(1))


def _proj_kernel(x_ref, ng_ref, wuvz_ref, wnat_ref, wg_ref, wt_ref,
                 lng_ref, lnb_ref, wmix_ref, bmix_ref, wpa_ref,
                 ma_ref, gb_ref, zb_ref, k_ref, v_ref, ki_ref, khm_ref, kib_ref,
                 qt_ref, qit_ref, wit_ref, vt_ref, *vn_refs, n_sub):
    x = x_ref[0]
    ms = jnp.mean(x * x, axis=-1, keepdims=True)
    h = (x * lax.rsqrt(ms + RMS_EPS) * ng_ref[...]).astype(BF)

    uvz = jnp.dot(h, wuvz_ref[...], preferred_element_type=F32)
    u = jax.nn.gelu(uvz[:, :SGU_WIDTH])
    v = jax.nn.gelu(uvz[:, SGU_WIDTH:2 * SGU_WIDTH])
    za = uvz[:, 2 * SGU_WIDTH:]
    mu = jnp.mean(v, axis=-1, keepdims=True)
    var = jnp.mean(jnp.square(v - mu), axis=-1, keepdims=True)
    vn = (v - mu) * lax.rsqrt(var + LN_EPS) * lng_ref[...] + lnb_ref[...]
    if vn_refs:
        vn_refs[0][0] = vn
    lane = lax.broadcasted_iota(I32, (SGU_LEN, LANES), 1)
    cols = []
    for c in range(SGU_WIDTH // LANES):
        rows = []
        for s in range(n_sub):
            blk = vn[s * SGU_LEN:(s + 1) * SGU_LEN, c * LANES:(c + 1) * LANES]
            lo = jnp.where(lane < SGU_GDIM, blk, 0.0).astype(BF)
            hi = jnp.where(lane >= SGU_GDIM, blk, 0.0).astype(BF)
            rhs = jnp.concatenate([lo, hi], axis=0)
            rows.append(jnp.dot(wmix_ref[c], rhs, preferred_element_type=F32))
        cols.append(jnp.concatenate(rows, axis=0) if n_sub > 1 else rows[0])
    mixed = jnp.concatenate(cols, axis=1)
    bias = bmix_ref[...]
    if n_sub > 1:
        bias = jnp.concatenate([bias] * n_sub, axis=0)
    a_out = u * (mixed + bias) * jax.nn.silu(za)
    pa = jnp.dot(a_out.astype(BF), wpa_ref[...], preferred_element_type=F32)
    g = jnp.dot(h, wg_ref[...], preferred_element_type=F32)
    ma_ref[0] = (jax.nn.sigmoid(g[:, :D_MODEL]) * pa).astype(BF)
    gb_ref[0] = g[:, D_MODEL:]

    nat = jnp.dot(h, wnat_ref[...], preferred_element_type=F32)
    k = nat[:, :KV_WIDTH]
    k_ref[0] = k
    v_ref[0] = nat[:, KV_WIDTH:2 * KV_WIDTH]
    zb_ref[0] = nat[:, 2 * KV_WIDTH:2 * KV_WIDTH + ATT_WIDTH]
    ki = nat[:, 2 * KV_WIDTH + ATT_WIDTH:]
    ki_ref[0] = ki
    kib_ref[0] = ki.astype(BF)
    for kv in range(N_KV_HEADS):
        khm_ref[0, kv] = k[:, kv * HEAD_DIM:(kv + 1) * HEAD_DIM].astype(BF)

    tm = lax.dot_general(wt_ref[...], h, (((1,), (1,)), ((), ())),
                         preferred_element_type=F32)
    o_qi, o_v, o_wi = ATT_WIDTH, 2 * ATT_WIDTH, 2 * ATT_WIDTH + KV_WIDTH
    for jb in range(n_sub):
        tok = slice(jb * TQ, (jb + 1) * TQ)
        for hh in range(N_HEADS):
            kv, gi = divmod(hh, GROUP)
            qt_ref[0, jb, kv, :, gi * TQ:(gi + 1) * TQ] = (
                tm[hh * HEAD_DIM:(hh + 1) * HEAD_DIM, tok] * (HEAD_DIM ** -0.5 * LOG2_E)).astype(BF)
        for hh in range(N_IDX_HEADS):
            qit_ref[0, jb, :, hh * TQ:(hh + 1) * TQ] = (
                tm[o_qi + hh * IDX_DIM:o_qi + (hh + 1) * IDX_DIM, tok] * (IDX_DIM ** -0.5)).astype(BF)
    for kv in range(N_KV_HEADS):
        vt_ref[0, 0, kv * V_ROWS:kv * V_ROWS + HEAD_DIM, :] = (
            tm[o_v + kv * HEAD_DIM:o_v + (kv + 1) * HEAD_DIM, :].astype(BF))
        vt_ref[0, 0, kv * V_ROWS + HEAD_DIM:(kv + 1) * V_ROWS, :] = jnp.ones(
            (BF16_ROWS, tm.shape[1]), BF)
    wit_ref[0] = tm[o_wi:o_wi + N_IDX_HEADS, :] * (N_IDX_HEADS ** -0.5)


def _proj_call(x, wts, wmix, bmix, *, tt, lt, emit_vn):
    b, s, _ = x.shape
    n_sub = tt // SGU_LEN
    r = lt // tt
    grid = (b, s // tt)
    tok = lambda w, dt=F32: (jax.ShapeDtypeStruct((b, s, w), dt),
                             pl.BlockSpec((1, tt, w), lambda i, j: (i, j, 0)))
    outs = [
        tok(D_MODEL, BF),
        tok(D_MODEL),
        tok(ATT_WIDTH),
        tok(KV_WIDTH), tok(KV_WIDTH), tok(IDX_DIM),
        (jax.ShapeDtypeStruct((b, N_KV_HEADS, s, HEAD_DIM), BF),
         pl.BlockSpec((1, N_KV_HEADS, tt, HEAD_DIM), lambda i, j: (i, 0, j, 0))),
        tok(IDX_DIM, BF),
        (jax.ShapeDtypeStruct((b, s // TQ, N_KV_HEADS, HEAD_DIM, GROUP * TQ), BF),
         pl.BlockSpec((1, n_sub, N_KV_HEADS, HEAD_DIM, GROUP * TQ), lambda i, j: (i, j, 0, 0, 0))),
        (jax.ShapeDtypeStruct((b, s // TQ, IDX_DIM, N_IDX_HEADS * TQ), BF),
         pl.BlockSpec((1, n_sub, IDX_DIM, N_IDX_HEADS * TQ), lambda i, j: (i, j, 0, 0))),
        (jax.ShapeDtypeStruct((b, N_IDX_HEADS, s), F32),
         pl.BlockSpec((1, N_IDX_HEADS, tt), lambda i, j: (i, 0, j))),
        (jax.ShapeDtypeStruct((b, s // lt, N_KV_HEADS * V_ROWS, lt), BF),
         pl.BlockSpec((1, 1, N_KV_HEADS * V_ROWS, tt), lambda i, j: (i, j // r, 0, j % r))),
    ]
    if emit_vn:
        outs.append(tok(SGU_WIDTH))
    consts = [wts["norm_g"], wts["w_uvz"], wts["w_nat"], wts["w_g"], wts["w_t"],
              wts["ln_g"], wts["ln_b"], wmix, bmix, wts["w_pa"]]
    return pl.pallas_call(
        functools.partial(_proj_kernel, n_sub=n_sub),
        grid=grid,
        in_specs=[pl.BlockSpec((1, tt, D_MODEL), lambda i, j: (i, j, 0))]
                 + [_const_spec(c.shape) for c in consts],
        out_specs=[o[1] for o in outs],
        out_shape=[o[0] for o in outs],
        compiler_params=pltpu.CompilerParams(
            dimension_semantics=("arbitrary", "arbitrary"), vmem_limit_bytes=VMEM_LIMIT),
        name="proj_vn" if emit_vn else "proj",
    )(x, *consts)


def _key_to_f32(key_u):
    s = key_u ^ INT_MIN
    bits = s ^ (lax.shift_right_arithmetic(s, 31) & 0x7FFFFFFF)
    return lax.bitcast_convert_type(bits, F32)


def _f32_to_key(x):
    bits = lax.bitcast_convert_type(x, I32)
    return bits ^ (lax.shift_right_arithmetic(bits, 31) & 0x7FFFFFFF) ^ INT_MIN


def _attn_kernel(qt_ref, qit_ref, wit_ref, zb_ref, k_ref, vt_ref, ki_ref, tri_ref,
                 o_ref, sc_ref, bot_ref, mma_ref, mmb_ref, *, n_tiles, topk, causal, stream_len,
                 n_valid):
    lane = lax.broadcasted_iota(I32, (1, TQ), 1)
    if causal:
        q0 = pl.program_id(1) * TQ
        n_t = lax.div(q0 + TQ + LT - 1, LT)
        key_lo = jnp.zeros((1, TQ), I32)
        key_hi = (lax.shift_right_logical(q0 + lane, 6) + 1) * CHUNK
    else:
        n_t = n_tiles
        slab = (n_tiles * LT) // (TQ // stream_len)
        key_lo = lax.div(lane, stream_len) * slab
        key_hi = key_lo + n_valid

    qit = qit_ref[0, 0]
    wit = wit_ref[0]
    wrows = [wit[hh:hh + 1, :] for hh in range(N_IDX_HEADS)]
    row0 = lax.broadcasted_iota(I32, (LT, 1), 0)

    fold = lambda x, op: op(x.reshape(LT // SUBLANES, SUBLANES, TQ), axis=0)

    last = n_t - 1
    n_pairs = lax.div(n_t, 2)
    odd = n_t - 2 * n_pairs == 1
    keep = lambda c: c

    def idx_dot(t, buf):
        start = pl.multiple_of(t * LT, LT)
        buf[...] = jnp.dot(ki_ref[0, pl.ds(start, LT), :], qit, preferred_element_type=F32)

    def score_tile(t, buf, carry):
        mx, mn, n_nonneg, n_pos = carry
        sc = jnp.zeros((LT, TQ), F32)
        for hh in range(N_IDX_HEADS):
            sc = sc + jnp.maximum(buf[:, hh * TQ:(hh + 1) * TQ], 0.0) * wrows[hh]
        pos = row0 + t * LT
        ok = pos < key_hi if causal else (pos >= key_lo) & (pos < key_hi)
        mn = jnp.minimum(mn, fold(sc, jnp.min))
        sc = jnp.where(ok, sc, -jnp.inf)
        sc_ref[t] = sc
        ones = lambda m: fold(jnp.where(m, 1, 0).astype(I32), jnp.sum)
        return (jnp.maximum(mx, fold(sc, jnp.max)), mn,
                n_nonneg + ones(sc >= 0.0), n_pos + ones(sc > 0.0))

    def score_pair(i, carry):
        t0 = 2 * i
        idx_dot(t0 + 1, mmb_ref)
        carry = score_tile(t0, mma_ref, carry)
        idx_dot(jnp.minimum(t0 + 2, last), mma_ref)
        return score_tile(t0 + 1, mmb_ref, carry)

    idx_dot(0, mma_ref)
    stats = lax.fori_loop(
        0, n_pairs, score_pair,
        (jnp.full((SUBLANES, TQ), -jnp.inf, F32), jnp.full((SUBLANES, TQ), jnp.inf, F32),
         jnp.zeros((SUBLANES, TQ), I32), jnp.zeros((SUBLANES, TQ), I32)))
    mx, mn, n_nonneg, n_pos = lax.cond(
        odd, lambda c: score_tile(last, mma_ref, c), keep, stats)
    mx = jnp.max(mx, axis=0, keepdims=True)
    mn = jnp.min(mn, axis=0, keepdims=True)
    n_nonneg = jnp.sum(n_nonneg, axis=0, keepdims=True).astype(F32)
    n_pos = jnp.sum(n_pos, axis=0, keepdims=True).astype(F32)

    def count(pred):
        ones = lambda t: fold(jnp.where(pred(sc_ref[t]), 1, 0).astype(I32), jnp.sum)

        acc = lax.fori_loop(0, n_pairs, lambda i, a: a + ones(2 * i) + ones(2 * i + 1),
                            jnp.zeros((SUBLANES, TQ), I32))
        acc = lax.cond(odd, lambda a: a + ones(last), keep, acc)
        return jnp.sum(acc, axis=0, keepdims=True)

    k_f = float(topk)
    n_adm = (key_hi - key_lo).astype(F32)
    short = n_adm < k_f

    def bisect_bits():
        def bit_pass(i, carry):
            thr_u, n_ge = carry
            cand_u = thr_u | lax.shift_left(jnp.int32(1), 31 - i)
            cand = _key_to_f32(cand_u)
            cnt = count(lambda sc: sc >= cand)
            take = cnt >= topk
            return jnp.where(take, cand_u, thr_u), jnp.where(take, cnt, n_ge)

        thr_u, n_ge = lax.fori_loop(
            0, 32, bit_pass, (jnp.zeros((1, TQ), I32), jnp.zeros((1, TQ), I32)))
        thr = jnp.where(short, LOWEST, _key_to_f32(thr_u))
        n_gt = count(lambda sc: sc > thr)
        need = jnp.where(short, ALL_TIES, (topk - n_gt).astype(F32))
        tie = jnp.logical_and(jnp.logical_not(short), n_ge > topk)
        return thr, need, jnp.max(jnp.where(tie, 1, 0))

    count_f = lambda pred: count(pred).astype(F32)
    zero_thr = jnp.logical_and(n_pos < k_f, n_nonneg >= k_f)
    pos_side = n_pos >= k_f
    lo0 = jnp.where(pos_side, 0.0, mn)
    hi0 = jnp.where(pos_side, mx, 0.0)
    clo0 = jnp.where(pos_side, n_nonneg, n_adm)
    chi0 = jnp.where(pos_side, 0.0, n_nonneg)
    logk = jnp.log(k_f)
    f_of = lambda c: jnp.log(jnp.maximum(c, 0.5)) - logk
    done0 = jnp.where(short | zero_thr | (clo0 == k_f), 1, 0)

    def search_pass(st):
        lo, hi, clo, chi, flo, fhi, side, done = st
        frac = jnp.clip(flo / (flo - fhi), SEARCH_CLIP, 1.0 - SEARCH_CLIP)
        cand = lo + (hi - lo) * frac
        cnt = count_f(lambda sc: sc >= cand)
        ge = cnt >= k_f
        up_lo = jnp.logical_and(done == 0, ge)
        up_hi = jnp.logical_and(done == 0, jnp.logical_not(ge))
        fhi = jnp.where(up_lo & (side == 1), fhi * 0.5, fhi)
        flo = jnp.where(up_hi & (side == -1), flo * 0.5, flo)
        lo = jnp.where(up_lo, cand, lo)
        clo = jnp.where(up_lo, cnt, clo)
        flo = jnp.where(up_lo, f_of(cnt), flo)
        hi = jnp.where(up_hi, cand, hi)
        chi = jnp.where(up_hi, cnt, chi)
        fhi = jnp.where(up_hi, f_of(cnt), fhi)
        side = jnp.where(up_lo, 1, jnp.where(up_hi, -1, side))
        done = jnp.where(clo == k_f, 1, done)
        return lo, hi, clo, chi, flo, fhi, side, done

    def walk_plan(st):
        lo, _, clo, chi = st[:4]
        from_hi = k_f - chi
        from_lo = clo - k_f + 1.0
        down = jnp.logical_or(from_hi <= from_lo, lo == 0.0)
        return down, jnp.where(st[7] == 1, 0.0, jnp.where(down, from_hi, from_lo))

    most_steps = lambda st: jnp.max(walk_plan(st)[1]).astype(I32)

    st = lax.fori_loop(0, UNCHECKED_SEARCH_PASSES, lambda i, s: search_pass(s),
                       (lo0, hi0, clo0, chi0, f_of(clo0), f_of(chi0), jnp.zeros((1, TQ), I32), done0))

    def checked_pass(c):
        s = search_pass(c[2])
        return c[0] + 1, most_steps(s), s

    _, n_steps, st = lax.while_loop(
        lambda c: jnp.logical_and(c[0] < MAX_SEARCH_PASSES, c[1] > MAX_WALK_STEPS), checked_pass,
        (jnp.int32(UNCHECKED_SEARCH_PASSES), most_steps(st), st))

    def walked():
        lo, hi = st[0], st[1]
        down, steps = walk_plan(st)
        sign = jnp.where(down, 1.0, -1.0)
        bound0 = jnp.where(down, hi, _key_to_f32(_f32_to_key(-lo) + 1))

        def step(i, bound):
            def below(t):
                y = sc_ref[t] * sign
                return fold(jnp.where(y < bound, y, -jnp.inf), jnp.max)

            nxt = lax.fori_loop(
                0, n_pairs, lambda j, a: jnp.maximum(a, jnp.maximum(below(2 * j), below(2 * j + 1))),
                jnp.full((SUBLANES, TQ), -jnp.inf, F32))
            nxt = lax.cond(odd, lambda a: jnp.maximum(a, below(last)), keep, nxt)
            return jnp.where(steps > i.astype(F32), jnp.max(nxt, axis=0, keepdims=True), bound)

        found = lax.fori_loop(0, n_steps, step, bound0) * sign
        thr = jnp.where(short, LOWEST,
                        jnp.where(zero_thr, 0.0, jnp.where(st[7] == 1, lo, found)))
        tie = jnp.logical_and(zero_thr, n_nonneg > k_f)
        need = jnp.where(tie, k_f - n_pos, ALL_TIES)
        return thr, need, jnp.max(jnp.where(tie, 1, 0))

    use_bits = n_steps > MAX_WALK_STEPS
    thr, need, n_tie = lax.cond(use_bits, bisect_bits, walked)

    gw = GROUP * TQ

    def qk_dots(t, buf, kvs=range(N_KV_HEADS)):
        start = pl.multiple_of(t * LT, LT)
        for kv in kvs:
            buf[:, kv * gw:(kv + 1) * gw] = jnp.dot(
                k_ref[0, kv, pl.ds(start, LT), :], qt_ref[0, 0, kv], preferred_element_type=F32)

    def attend(with_ties, thr, need):
        def attend_tile(t, buf, carry, nxt=None):
            n_eq, n_sel, ms, accs = carry
            sc = sc_ref[t]
            if with_ties:
                eq = sc == thr
                pref = jnp.dot(tri_ref[...], jnp.where(eq, 1.0, 0.0).astype(BF),
                               preferred_element_type=F32)
                sel = (sc > thr) | (eq & (n_eq + pref - 1.0 < need))
                n_eq = n_eq + pref[LT - 1:LT, :]
            else:
                sel = sc >= thr
            n_sel = n_sel + fold(jnp.where(sel, 1, 0).astype(I32), jnp.sum)
            keep01 = jnp.where(sel, 1.0, 0.0).astype(BF)
            keep01 = jnp.concatenate([keep01] * GROUP, axis=1)
            new_m, new_acc = [None] * N_KV_HEADS, [None] * N_KV_HEADS
            for half in (range(0, N_KV_HEADS // 2), range(N_KV_HEADS // 2, N_KV_HEADS)):
                if nxt is not None:
                    qk_dots(nxt[0], nxt[1], half)
                ps, alphas = {}, {}
                for kv in half:
                    lg = buf[:, kv * gw:(kv + 1) * gw]
                    m_new = jnp.maximum(ms[kv], jnp.max(lg, axis=0, keepdims=True))
                    ps[kv] = jnp.exp2(lg - m_new).astype(BF) * keep01
                    alphas[kv] = jnp.exp2(ms[kv] - m_new)
                    new_m[kv] = m_new
                for kv in half:
                    pv = jnp.dot(vt_ref[0, t, kv * V_ROWS:(kv + 1) * V_ROWS, :], ps[kv],
                                 preferred_element_type=F32)
                    new_acc[kv] = alphas[kv] * accs[kv] + pv
            return n_eq, n_sel, tuple(new_m), tuple(new_acc)

        def attend_pair(i, carry):
            t0 = 2 * i
            carry = attend_tile(t0, mma_ref, carry, (t0 + 1, mmb_ref))
            return attend_tile(t0 + 1, mmb_ref, carry, (jnp.minimum(t0 + 2, last), mma_ref))

        init = (jnp.zeros((1, TQ), F32), jnp.zeros((SUBLANES, TQ), I32),
                tuple(jnp.full((1, gw), MASKED_LOGIT, F32) for _ in range(N_KV_HEADS)),
                tuple(jnp.zeros((V_ROWS, gw), F32) for _ in range(N_KV_HEADS)))
        qk_dots(0, mma_ref)
        carry = lax.fori_loop(0, n_pairs, attend_pair, init)
        _, n_sel, _, accs = lax.cond(odd, lambda c: attend_tile(last, mma_ref, c), keep, carry)
        return accs, jnp.sum(n_sel, axis=0, keepdims=True)

    accs, n_sel = lax.cond(n_tie > 0, lambda: attend(True, thr, need),
                           lambda: attend(False, thr, need))

    def attend_exact(thr, need):
        def tile(t, carry):
            n_eq, ms, accs = carry
            sc = sc_ref[t]
            eq = sc == thr
            pref = jnp.dot(tri_ref[...], jnp.where(eq, 1.0, 0.0).astype(BF),
                           preferred_element_type=F32)
            sel = (sc > thr) | (eq & (n_eq + pref - 1.0 < need))
            n_eq = n_eq + pref[LT - 1:LT, :]
            bias = jnp.concatenate([jnp.where(sel, 0.0, MASKED_LOGIT)] * GROUP, axis=1)
            start = pl.multiple_of(t * LT, LT)
            new_m, new_acc = [], []
            for kv in range(N_KV_HEADS):
                lg = jnp.dot(k_ref[0, kv, pl.ds(start, LT), :], qt_ref[0, 0, kv],
                             preferred_element_type=F32) + bias
                m_new = jnp.maximum(ms[kv], jnp.max(lg, axis=0, keepdims=True))
                p = jnp.exp2(lg - m_new).astype(BF)
                pv = jnp.dot(vt_ref[0, t, kv * V_ROWS:(kv + 1) * V_ROWS, :], p,
                             preferred_element_type=F32)
                new_acc.append(jnp.exp2(ms[kv] - m_new) * accs[kv] + pv)
                new_m.append(m_new)
            return n_eq, tuple(new_m), tuple(new_acc)

        init = (jnp.zeros((1, TQ), F32),
                tuple(jnp.full((1, gw), MASKED_LOGIT, F32) for _ in range(N_KV_HEADS)),
                tuple(jnp.zeros((V_ROWS, gw), F32) for _ in range(N_KV_HEADS)))
        return lax.fori_loop(0, n_t, tile, init)[2]

    n_want = jnp.where(short, n_adm, k_f)
    n_miss = jnp.sum(jnp.where(n_sel.astype(F32) != n_want, 1, 0))
    n_tiny = sum(jnp.sum(jnp.where(a[HEAD_DIM:HEAD_DIM + 1] >= MIN_SOFTMAX_SUM, 0, 1)) for a in accs)

    def redo():
        thr_b, need_b, _ = bisect_bits()
        return attend_exact(thr_b, need_b)

    miss = jnp.logical_and(n_miss > 0, jnp.logical_not(use_bits))
    accs = lax.cond(jnp.logical_or(miss, n_tiny > 0), redo, lambda: accs)

    for kv in range(N_KV_HEADS):
        o_t = accs[kv][:HEAD_DIM] / accs[kv][HEAD_DIM:HEAD_DIM + 1]
        for gi in range(GROUP):
            hh = GROUP * kv + gi
            bot_ref[hh * HEAD_DIM:(hh + 1) * HEAD_DIM, :] = o_t[:, gi * TQ:(gi + 1) * TQ]
    o_ref[0] = (bot_ref[...].T * jax.nn.silu(zb_ref[0])).astype(BF)


def _attn_call(qt, qit, wit, zb, k_hm, vt, ki_b, *, topk, causal, stream_len=0, n_valid=0):
    b, nq = qt.shape[0], qt.shape[1]
    s = nq * TQ
    n_tiles = vt.shape[1]
    l_all = k_hm.shape[2]
    assert l_all == n_tiles * LT
    tri = (lax.broadcasted_iota(I32, (LT, LT), 0) >= lax.broadcasted_iota(I32, (LT, LT), 1)).astype(BF)
    whole = lambda shape: pl.BlockSpec((1,) + shape, lambda i, j: (i,) + (0,) * len(shape),
                                       pipeline_mode=pl.Buffered(1))
    return pl.pallas_call(
        functools.partial(_attn_kernel, n_tiles=n_tiles, topk=topk, causal=causal,
                          stream_len=stream_len, n_valid=n_valid),
        grid=(b, nq),
        in_specs=[
            pl.BlockSpec((1, 1, N_KV_HEADS, HEAD_DIM, GROUP * TQ), lambda i, j: (i, j, 0, 0, 0)),
            pl.BlockSpec((1, 1, IDX_DIM, N_IDX_HEADS * TQ), lambda i, j: (i, j, 0, 0)),
            pl.BlockSpec((1, N_IDX_HEADS, TQ), lambda i, j: (i, 0, j)),
            pl.BlockSpec((1, TQ, ATT_WIDTH), lambda i, j: (i, j, 0)),
            whole((N_KV_HEADS, l_all, HEAD_DIM)),
            whole((n_tiles, N_KV_HEADS * V_ROWS, LT)),
            whole((l_all, IDX_DIM)),
            _const_spec((LT, LT)),
        ],
        out_specs=pl.BlockSpec((1, TQ, ATT_WIDTH), lambda i, j: (i, j, 0)),
        out_shape=jax.ShapeDtypeStruct((b, s, ATT_WIDTH), BF),
        scratch_shapes=[pltpu.VMEM((n_tiles, LT, TQ), F32), pltpu.VMEM((ATT_WIDTH, TQ), F32),
                        pltpu.VMEM((LT, N_IDX_HEADS * TQ), F32), pltpu.VMEM((LT, N_IDX_HEADS * TQ), F32)],
        compiler_params=pltpu.CompilerParams(
            dimension_semantics=("arbitrary", "arbitrary"), vmem_limit_bytes=VMEM_LIMIT),
        name="attn_prompt" if causal else "attn_sample",
    )(qt, qit, wit, zb, k_hm, vt, ki_b, tri)


def _merge_kernel(x_ref, ma_ref, gb_ref, bg_ref, wpb_ref, wout_ref, fg_ref, y_ref):
    pb = jnp.dot(bg_ref[0], wpb_ref[...], preferred_element_type=F32)
    m = ma_ref[0].astype(F32) + jax.nn.sigmoid(gb_ref[0]) * pb
    xo = x_ref[0] + jnp.dot(m.astype(BF), wout_ref[...], preferred_element_type=F32)
    ms = jnp.mean(xo * xo, axis=-1, keepdims=True)
    y_ref[0] = xo * lax.rsqrt(ms + RMS_EPS) * fg_ref[...]


def _merge_call(x, ma, gb, bg, wts, *, tt):
    b, s, _ = x.shape
    tok = lambda w: pl.BlockSpec((1, tt, w), lambda i, j: (i, j, 0))
    consts = [wts["w_pb"], wts["w_out"], wts["final_g"]]
    return pl.pallas_call(
        _merge_kernel,
        grid=(b, s // tt),
        in_specs=[tok(D_MODEL), tok(D_MODEL), tok(D_MODEL), tok(ATT_WIDTH)]
                 + [_const_spec(c.shape) for c in consts],
        out_specs=tok(D_MODEL),
        out_shape=jax.ShapeDtypeStruct((b, s, D_MODEL), F32),
        compiler_params=pltpu.CompilerParams(
            dimension_semantics=("arbitrary", "arbitrary"), vmem_limit_bytes=VMEM_LIMIT),
        name="merge",
    )(x, ma, gb, bg, *consts)


def _prep_weights(norm_g, w_in, ln_g, ln_b, w_pa, w_pb, w_out, final_g):
    o = 0
    cols = {}
    for name, width in (("u", SGU_WIDTH), ("v", SGU_WIDTH), ("za", SGU_WIDTH), ("q", ATT_WIDTH),
                        ("k", KV_WIDTH), ("vv", KV_WIDTH), ("zb", ATT_WIDTH),
                        ("qi", N_IDX_HEADS * IDX_DIM), ("ki", IDX_DIM), ("wi", N_IDX_HEADS),
                        ("ga", D_MODEL), ("gb", D_MODEL)):
        cols[name] = (o, o + width)
        o += width
    wb = w_in.astype(BF)
    sl = lambda a, z: wb[:, cols[a][0]:cols[z][1]]
    return {
        "norm_g": norm_g.reshape(1, D_MODEL), "w_uvz": sl("u", "za"),
        "w_nat": jnp.concatenate([sl("k", "zb"), sl("ki", "ki")], axis=1),
        "w_g": sl("ga", "gb"),
        "w_t": jnp.concatenate([sl("q", "q"), sl("qi", "qi"), sl("vv", "vv"), sl("wi", "wi")], axis=1).T,
        "ln_g": ln_g.reshape(1, SGU_WIDTH), "ln_b": ln_b.reshape(1, SGU_WIDTH),
        "w_pa": w_pa.astype(BF), "w_pb": w_pb.astype(BF), "w_out": w_out.astype(BF),
        "final_g": final_g.reshape(1, D_MODEL),
    }


def _mix_operands(w_mix, b_pos):
    pairs = w_mix.reshape(SGU_GROUPS // 2, 2, SGU_LEN, SGU_LEN)
    wmix = jnp.concatenate([pairs[:, 0], pairs[:, 1]], axis=-1).astype(BF)
    bmix = jnp.repeat(b_pos, SGU_GDIM, axis=1)
    return wmix, bmix


def kernel(x_prompt, x_sample, cache_k, cache_v, cache_kidx, norm_g, w_in, sgu_ln_g, sgu_ln_b,
           sgu_w, sgu_b, w_pa, w_pb, w_out, final_g):
    b, s, _ = x_prompt.shape
    db, ds, _ = x_sample.shape
    past = cache_k.shape[2]
    assert norm_g.shape[0] == 1 and db * ds == TQ and s % LT == 0
    wts = _prep_weights(norm_g[0], w_in[0], sgu_ln_g[0], sgu_ln_b[0], w_pa[0], w_pb[0], w_out[0],
                        final_g)

    pos = jnp.arange(SGU_LEN)
    chunk_ok = (pos[None, :] // CHUNK) <= (pos[:, None] // CHUNK)
    wmix_p, bmix_p = _mix_operands(jnp.where(chunk_ok[None], sgu_w[0], 0.0), sgu_b[0].T)
    (ma, gb, zb, k, v, ki, k_hm, ki_b, qt, qit, wit, vt) = _proj_call(
        x_prompt, wts, wmix_p, bmix_p, tt=4 * TQ, lt=LT, emit_vn=False)
    bg = _attn_call(qt, qit, wit, zb, k_hm, vt, ki_b, topk=min(TOPK_MAX, s // 4), causal=True)
    y_prompt = _merge_call(x_prompt, ma, gb, bg, wts, tt=512)

    assert past % CHUNK == 0 and ds <= CHUNK
    w_blk = jnp.einsum("ab,gij->gaibj", jnp.eye(db, dtype=F32), sgu_w[0][:, :ds, :ds])
    wmix_s, bmix_s = _mix_operands(w_blk.reshape(SGU_GROUPS, SGU_LEN, SGU_LEN),
                                   jnp.tile(sgu_b[0][:, :ds].T, (db, 1)))
    n_tok = db * ds
    (ma_s, gb_s, zb_s, k_s, v_s, ki_s, _, _, qt_s, qit_s, wit_s, _, vn_s) = _proj_call(
        x_sample.reshape(1, n_tok, D_MODEL), wts, wmix_s, bmix_s, tt=n_tok, lt=n_tok, emit_vn=True)
    n_keys = past + ds
    slab = -(-n_keys * db // LT) * LT // db
    assert (slab * db) % LT == 0 and slab >= n_keys
    cat = lambda c, n, w: jnp.pad(
        jnp.concatenate([c.reshape(db, past, w), n.reshape(db, ds, w)], axis=1),
        ((0, 0), (0, slab - n_keys), (0, 0))).reshape(db * slab, w).astype(BF)
    k_all, v_all, ki_all = (cat(cache_k[0], k_s, KV_WIDTH), cat(cache_v[0], v_s, KV_WIDTH),
                            cat(cache_kidx[0], ki_s, IDX_DIM))
    n_t_s = db * slab // LT
    k_hm_s = k_all.reshape(1, db * slab, N_KV_HEADS, HEAD_DIM).transpose(0, 2, 1, 3)
    vt_s = v_all.reshape(n_t_s, LT, N_KV_HEADS, HEAD_DIM).transpose(0, 2, 3, 1)
    vt_s = jnp.concatenate([vt_s, jnp.ones((n_t_s, N_KV_HEADS, BF16_ROWS, LT), BF)], axis=2)
    vt_s = vt_s.reshape(1, n_t_s, N_KV_HEADS * V_ROWS, LT)
    bg_s = _attn_call(qt_s, qit_s, wit_s, zb_s, k_hm_s, vt_s, ki_all[None],
                      topk=min(TOPK_MAX, n_keys // 4), causal=False, stream_len=ds, n_valid=n_keys)
    y_sample = _merge_call(x_sample.reshape(1, n_tok, D_MODEL), ma_s, gb_s, bg_s, wts, tt=n_tok)

    return (y_prompt, y_sample.reshape(db, ds, D_MODEL),
            k.reshape(1, b, s, N_KV_HEADS, HEAD_DIM), v.reshape(1, b, s, N_KV_HEADS, HEAD_DIM),
            ki.reshape(1, b, s, IDX_DIM),
            k_s.reshape(1, db, ds, N_KV_HEADS, HEAD_DIM), v_s.reshape(1, db, ds, N_KV_HEADS, HEAD_DIM),
            ki_s.reshape(1, db, ds, IDX_DIM), vn_s.reshape(1, db, ds, SGU_WIDTH))
```

```python
import functools

import jax
import jax.numpy as jnp
from jax import lax
from jax.experimental import pallas as pl
from jax.experimental.pallas import tpu as pltpu

D_MODEL = 1024
CHUNK = 64
SGU_LEN = 128
SGU_GROUPS = 8
SGU_WIDTH = 512
SGU_GDIM = SGU_WIDTH // SGU_GROUPS
N_HEADS = 8
N_KV_HEADS = 4
HEAD_DIM = 64
GROUP = N_HEADS // N_KV_HEADS
ATT_WIDTH = N_HEADS * HEAD_DIM
KV_WIDTH = N_KV_HEADS * HEAD_DIM
N_IDX_HEADS = 8
IDX_DIM = 64
TOPK_MAX = 256
RMS_EPS = 1e-6
LN_EPS = 1e-5

BF = jnp.bfloat16
F32 = jnp.float32
I32 = jnp.int32

LANES = 128
SUBLANES = 8
BF16_ROWS = 16
TQ = LANES
LT = 512
V_ROWS = HEAD_DIM + BF16_ROWS
VMEM_LIMIT = 52 * 1024 * 1024
MASKED_LOGIT = -1e30
LOWEST = -3.4028235e38
MIN_SOFTMAX_SUM = 1e-30
ALL_TIES = 3e38
UNCHECKED_SEARCH_PASSES = 8
MAX_SEARCH_PASSES = 28
MAX_WALK_STEPS = 8
SEARCH_CLIP = 0.02
LOG2_E = 1.4426950408889634
INT_MIN = -2147483648


def _const_spec(shape):
    nd = len(shape)
    return pl.BlockSpec(shape, lambda *_: (0,) * nd, pipeline_mode=pl.Buffered(1))


def _proj_kernel(x_ref, ng_ref, wuvz_ref, wnat_ref, wg_ref, wt_ref,
                 lng_ref, lnb_ref, wmix_ref, bmix_ref, wpa_ref,
                 ma_ref, gb_ref, zb_ref, k_ref, v_ref, ki_ref, khm_ref, kib_ref,
                 qt_ref, qit_ref, wit_ref, vt_ref, *vn_refs, n_sub):
    x = x_ref[0]
    ms = jnp.mean(x * x, axis=-1, keepdims=True)
    h = (x * lax.rsqrt(ms + RMS_EPS) * ng_ref[...]).astype(BF)

    uvz = jnp.dot(h, wuvz_ref[...], preferred_element_type=F32)
    u = jax.nn.gelu(uvz[:, :SGU_WIDTH])
    v = jax.nn.gelu(uvz[:, SGU_WIDTH:2 * SGU_WIDTH])
    za = uvz[:, 2 * SGU_WIDTH:]
    mu = jnp.mean(v, axis=-1, keepdims=True)
    var = jnp.mean(jnp.square(v - mu), axis=-1, keepdims=True)
    vn = (v - mu) * lax.rsqrt(var + LN_EPS) * lng_ref[...] + lnb_ref[...]
    if vn_refs:
        vn_refs[0][0] = vn
    lane = lax.broadcasted_iota(I32, (SGU_LEN, LANES), 1)
    cols = []
    for c in range(SGU_WIDTH // LANES):
        rows = []
        for s in range(n_sub):
            blk = vn[s * SGU_LEN:(s + 1) * SGU_LEN, c * LANES:(c + 1) * LANES]
            lo = jnp.where(lane < SGU_GDIM, blk, 0.0).astype(BF)
            hi = jnp.where(lane >= SGU_GDIM, blk, 0.0).astype(BF)
            rhs = jnp.concatenate([lo, hi], axis=0)
            rows.append(jnp.dot(wmix_ref[c], rhs, preferred_element_type=F32))
        cols.append(jnp.concatenate(rows, axis=0) if n_sub > 1 else rows[0])
    mixed = jnp.concatenate(cols, axis=1)
    bias = bmix_ref[...]
    if n_sub > 1:
        bias = jnp.concatenate([bias] * n_sub, axis=0)
    a_out = u * (mixed + bias) * jax.nn.silu(za)
    pa = jnp.dot(a_out.astype(BF), wpa_ref[...], preferred_element_type=F32)
    g = jnp.dot(h, wg_ref[...], preferred_element_type=F32)
    ma_ref[0] = (jax.nn.sigmoid(g[:, :D_MODEL]) * pa).astype(BF)
    gb_ref[0] = g[:, D_MODEL:]

    nat = jnp.dot(h, wnat_ref[...], preferred_element_type=F32)
    k = nat[:, :KV_WIDTH]
    k_ref[0] = k
    v_ref[0] = nat[:, KV_WIDTH:2 * KV_WIDTH]
    zb_ref[0] = nat[:, 2 * KV_WIDTH:2 * KV_WIDTH + ATT_WIDTH]
    ki = nat[:, 2 * KV_WIDTH + ATT_WIDTH:]
    ki_ref[0] = ki
    kib_ref[0] = ki.astype(BF)
    for kv in range(N_KV_HEADS):
        khm_ref[0, kv] = k[:, kv * HEAD_DIM:(kv + 1) * HEAD_DIM].astype(BF)

    tm = lax.dot_general(wt_ref[...], h, (((1,), (1,)), ((), ())),
                         preferred_element_type=F32)
    o_qi, o_v, o_wi = ATT_WIDTH, 2 * ATT_WIDTH, 2 * ATT_WIDTH + KV_WIDTH
    for jb in range(n_sub):
        tok = slice(jb * TQ, (jb + 1) * TQ)
        for hh in range(N_HEADS):
            kv, gi = divmod(hh, GROUP)
            qt_ref[0, jb, kv, :, gi * TQ:(gi + 1) * TQ] = (
                tm[hh * HEAD_DIM:(hh + 1) * HEAD_DIM, tok] * (HEAD_DIM ** -0.5 * LOG2_E)).astype(BF)
        for hh in range(N_IDX_HEADS):
            qit_ref[0, jb, :, hh * TQ:(hh + 1) * TQ] = (
                tm[o_qi + hh * IDX_DIM:o_qi + (hh + 1) * IDX_DIM, tok] * (IDX_DIM ** -0.5)).astype(BF)
    for kv in range(N_KV_HEADS):
        vt_ref[0, 0, kv * V_ROWS:kv * V_ROWS + HEAD_DIM, :] = (
            tm[o_v + kv * HEAD_DIM:o_v + (kv + 1) * HEAD_DIM, :].astype(BF))
        vt_ref[0, 0, kv * V_ROWS + HEAD_DIM:(kv + 1) * V_ROWS, :] = jnp.ones(
            (BF16_ROWS, tm.shape[1]), BF)
    wit_ref[0] = tm[o_wi:o_wi + N_IDX_HEADS, :] * (N_IDX_HEADS ** -0.5)


def _proj_call(x, wts, wmix, bmix, *, tt, lt, emit_vn):
    b, s, _ = x.shape
    n_sub = tt // SGU_LEN
    r = lt // tt
    grid = (b, s // tt)
    tok = lambda w, dt=F32: (jax.ShapeDtypeStruct((b, s, w), dt),
                             pl.BlockSpec((1, tt, w), lambda i, j: (i, j, 0)))
    outs = [
        tok(D_MODEL, BF),
        tok(D_MODEL),
        tok(ATT_WIDTH),
        tok(KV_WIDTH), tok(KV_WIDTH), tok(IDX_DIM),
        (jax.ShapeDtypeStruct((b, N_KV_HEADS, s, HEAD_DIM), BF),
         pl.BlockSpec((1, N_KV_HEADS, tt, HEAD_DIM), lambda i, j: (i, 0, j, 0))),
        tok(IDX_DIM, BF),
        (jax.ShapeDtypeStruct((b, s // TQ, N_KV_HEADS, HEAD_DIM, GROUP * TQ), BF),
         pl.BlockSpec((1, n_sub, N_KV_HEADS, HEAD_DIM, GROUP * TQ), lambda i, j: (i, j, 0, 0, 0))),
        (jax.ShapeDtypeStruct((b, s // TQ, IDX_DIM, N_IDX_HEADS * TQ), BF),
         pl.BlockSpec((1, n_sub, IDX_DIM, N_IDX_HEADS * TQ), lambda i, j: (i, j, 0, 0))),
        (jax.ShapeDtypeStruct((b, N_IDX_HEADS, s), F32),
         pl.BlockSpec((1, N_IDX_HEADS, tt), lambda i, j: (i, 0, j))),
        (jax.ShapeDtypeStruct((b, s // lt, N_KV_HEADS * V_ROWS, lt), BF),
         pl.BlockSpec((1, 1, N_KV_HEADS * V_ROWS, tt), lambda i, j: (i, j // r, 0, j % r))),
    ]
    if emit_vn:
        outs.append(tok(SGU_WIDTH))
    consts = [wts["norm_g"], wts["w_uvz"], wts["w_nat"], wts["w_g"], wts["w_t"],
              wts["ln_g"], wts["ln_b"], wmix, bmix, wts["w_pa"]]
    return pl.pallas_call(
        functools.partial(_proj_kernel, n_sub=n_sub),
        grid=grid,
        in_specs=[pl.BlockSpec((1, tt, D_MODEL), lambda i, j: (i, j, 0))]
                 + [_const_spec(c.shape) for c in consts],
        out_specs=[o[1] for o in outs],
        out_shape=[o[0] for o in outs],
        compiler_params=pltpu.CompilerParams(
            dimension_semantics=("arbitrary", "arbitrary"), vmem_limit_bytes=VMEM_LIMIT),
        name="proj_vn" if emit_vn else "proj",
    )(x, *consts)


def _key_to_f32(key_u):
    s = key_u ^ INT_MIN
    bits = s ^ (lax.shift_right_arithmetic(s, 31) & 0x7FFFFFFF)
    return lax.bitcast_convert_type(bits, F32)


def _f32_to_key(x):
    bits = lax.bitcast_convert_type(x, I32)
    return bits ^ (lax.shift_right_arithmetic(bits, 31) & 0x7FFFFFFF) ^ INT_MIN


def _attn_kernel(qt_ref, qit_ref, wit_ref, zb_ref, k_ref, vt_ref, ki_ref, tri_ref,
                 o_ref, sc_ref, bot_ref, mma_ref, mmb_ref, *, n_tiles, topk, causal, stream_len,
                 n_valid):
    lane = lax.broadcasted_iota(I32, (1, TQ), 1)
    if causal:
        q0 = pl.program_id(1) * TQ
        n_t = lax.div(q0 + TQ + LT - 1, LT)
        key_lo = jnp.zeros((1, TQ), I32)
        key_hi = (lax.shift_right_logical(q0 + lane, 6) + 1) * CHUNK
    else:
        n_t = n_tiles
        slab = (n_tiles * LT) // (TQ // stream_len)
        key_lo = lax.div(lane, stream_len) * slab
        key_hi = key_lo + n_valid

    qit = qit_ref[0, 0]
    wit = wit_ref[0]
    wrows = [wit[hh:hh + 1, :] for hh in range(N_IDX_HEADS)]
    row0 = lax.broadcasted_iota(I32, (LT, 1), 0)

    fold = lambda x, op: op(x.reshape(LT // SUBLANES, SUBLANES, TQ), axis=0)

    last = n_t - 1
    n_pairs = lax.div(n_t, 2)
    odd = n_t - 2 * n_pairs == 1
    keep = lambda c: c

    def idx_dot(t, buf):
        start = pl.multiple_of(t * LT, LT)
        buf[...] = jnp.dot(ki_ref[0, pl.ds(start, LT), :], qit, preferred_element_type=F32)

    def score_tile(t, buf, carry):
        mx, mn, n_nonneg, n_pos = carry
        sc = jnp.zeros((LT, TQ), F32)
        for hh in range(N_IDX_HEADS):
            sc = sc + jnp.maximum(buf[:, hh * TQ:(hh + 1) * TQ], 0.0) * wrows[hh]
        pos = row0 + t * LT
        ok = pos < key_hi if causal else (pos >= key_lo) & (pos < key_hi)
        mn = jnp.minimum(mn, fold(sc, jnp.min))
        sc = jnp.where(ok, sc, -jnp.inf)
        sc_ref[t] = sc
        ones = lambda m: fold(jnp.where(m, 1, 0).astype(I32), jnp.sum)
        return (jnp.maximum(mx, fold(sc, jnp.max)), mn,
                n_nonneg + ones(sc >= 0.0), n_pos + ones(sc > 0.0))

    def score_pair(i, carry):
        t0 = 2 * i
        idx_dot(t0 + 1, mmb_ref)
        carry = score_tile(t0, mma_ref, carry)
        idx_dot(jnp.minimum(t0 + 2, last), mma_ref)
        return score_tile(t0 + 1, mmb_ref, carry)

    idx_dot(0, mma_ref)
    stats = lax.fori_loop(
        0, n_pairs, score_pair,
        (jnp.full((SUBLANES, TQ), -jnp.inf, F32), jnp.full((SUBLANES, TQ), jnp.inf, F32),
         jnp.zeros((SUBLANES, TQ), I32), jnp.zeros((SUBLANES, TQ), I32)))
    mx, mn, n_nonneg, n_pos = lax.cond(
        odd, lambda c: score_tile(last, mma_ref, c), keep, stats)
    mx = jnp.max(mx, axis=0, keepdims=True)
    mn = jnp.min(mn, axis=0, keepdims=True)
    n_nonneg = jnp.sum(n_nonneg, axis=0, keepdims=True).astype(F32)
    n_pos = jnp.sum(n_pos, axis=0, keepdims=True).astype(F32)

    def count(pred):
        ones = lambda t: fold(jnp.where(pred(sc_ref[t]), 1, 0).astype(I32), jnp.sum)

        acc = lax.fori_loop(0, n_pairs, lambda i, a: a + ones(2 * i) + ones(2 * i + 1),
                            jnp.zeros((SUBLANES, TQ), I32))
        acc = lax.cond(odd, lambda a: a + ones(last), keep, acc)
        return jnp.sum(acc, axis=0, keepdims=True)

    k_f = float(topk)
    n_adm = (key_hi - key_lo).astype(F32)
    short = n_adm < k_f

    def bisect_bits():
        def bit_pass(i, carry):
            thr_u, n_ge = carry
            cand_u = thr_u | lax.shift_left(jnp.int32(1), 31 - i)
            cand = _key_to_f32(cand_u)
            cnt = count(lambda sc: sc >= cand)
            take = cnt >= topk
            return jnp.where(take, cand_u, thr_u), jnp.where(take, cnt, n_ge)

        thr_u, n_ge = lax.fori_loop(
            0, 32, bit_pass, (jnp.zeros((1, TQ), I32), jnp.zeros((1, TQ), I32)))
        thr = jnp.where(short, LOWEST, _key_to_f32(thr_u))
        n_gt = count(lambda sc: sc > thr)
        need = jnp.where(short, ALL_TIES, (topk - n_gt).astype(F32))
        tie = jnp.logical_and(jnp.logical_not(short), n_ge > topk)
        return thr, need, jnp.max(jnp.where(tie, 1, 0))

    count_f = lambda pred: count(pred).astype(F32)
    zero_thr = jnp.logical_and(n_pos < k_f, n_nonneg >= k_f)
    pos_side = n_pos >= k_f
    lo0 = jnp.where(pos_side, 0.0, mn)
    hi0 = jnp.where(pos_side, mx, 0.0)
    clo0 = jnp.where(pos_side, n_nonneg, n_adm)
    chi0 = jnp.where(pos_side, 0.0, n_nonneg)
    logk = jnp.log(k_f)
    f_of = lambda c: jnp.log(jnp.maximum(c, 0.5)) - logk
    done0 = jnp.where(short | zero_thr | (clo0 == k_f), 1, 0)

    def search_pass(st):
        lo, hi, clo, chi, flo, fhi, side, done = st
        frac = jnp.clip(flo / (flo - fhi), SEARCH_CLIP, 1.0 - SEARCH_CLIP)
        cand = lo + (hi - lo) * frac
        cnt = count_f(lambda sc: sc >= cand)
        ge = cnt >= k_f
        up_lo = jnp.logical_and(done == 0, ge)
        up_hi = jnp.logical_and(done == 0, jnp.logical_not(ge))
        fhi = jnp.where(up_lo & (side == 1), fhi * 0.5, fhi)
        flo = jnp.where(up_hi & (side == -1), flo * 0.5, flo)
        lo = jnp.where(up_lo, cand, lo)
        clo = jnp.where(up_lo, cnt, clo)
        flo = jnp.where(up_lo, f_of(cnt), flo)
        hi = jnp.where(up_hi, cand, hi)
        chi = jnp.where(up_hi, cnt, chi)
        fhi = jnp.where(up_hi, f_of(cnt), fhi)
        side = jnp.where(up_lo, 1, jnp.where(up_hi, -1, side))
        done = jnp.where(clo == k_f, 1, done)
        return lo, hi, clo, chi, flo, fhi, side, done

    def walk_plan(st):
        lo, _, clo, chi = st[:4]
        from_hi = k_f - chi
        from_lo = clo - k_f + 1.0
        down = jnp.logical_or(from_hi <= from_lo, lo == 0.0)
        return down, jnp.where(st[7] == 1, 0.0, jnp.where(down, from_hi, from_lo))

    most_steps = lambda st: jnp.max(walk_plan(st)[1]).astype(I32)

    st = lax.fori_loop(0, UNCHECKED_SEARCH_PASSES, lambda i, s: search_pass(s),
                       (lo0, hi0, clo0, chi0, f_of(clo0), f_of(chi0), jnp.zeros((1, TQ), I32), done0))

    def checked_pass(c):
        s = search_pass(c[2])
        return c[0] + 1, most_steps(s), s

    _, n_steps, st = lax.while_loop(
        lambda c: jnp.logical_and(c[0] < MAX_SEARCH_PASSES, c[1] > MAX_WALK_STEPS), checked_pass,
        (jnp.int32(UNCHECKED_SEARCH_PASSES), most_steps(st), st))

    def walked():
        lo, hi = st[0], st[1]
        down, steps = walk_plan(st)
        sign = jnp.where(down, 1.0, -1.0)
        bound0 = jnp.where(down, hi, _key_to_f32(_f32_to_key(-lo) + 1))

        def step(i, bound):
            def below(t):
                y = sc_ref[t] * sign
                return fold(jnp.where(y < bound, y, -jnp.inf), jnp.max)

            nxt = lax.fori_loop(
                0, n_pairs, lambda j, a: jnp.maximum(a, jnp.maximum(below(2 * j), below(2 * j + 1))),
                jnp.full((SUBLANES, TQ), -jnp.inf, F32))
            nxt = lax.cond(odd, lambda a: jnp.maximum(a, below(last)), keep, nxt)
            return jnp.where(steps > i.astype(F32), jnp.max(nxt, axis=0, keepdims=True), bound)

        found = lax.fori_loop(0, n_steps, step, bound0) * sign
        thr = jnp.where(short, LOWEST,
                        jnp.where(zero_thr, 0.0, jnp.where(st[7] == 1, lo, found)))
        tie = jnp.logical_and(zero_thr, n_nonneg > k_f)
        need = jnp.where(tie, k_f - n_pos, ALL_TIES)
        return thr, need, jnp.max(jnp.where(tie, 1, 0))

    use_bits = n_steps > MAX_WALK_STEPS
    thr, need, n_tie = lax.cond(use_bits, bisect_bits, walked)

    gw = GROUP * TQ

    def qk_dots(t, buf, kvs=range(N_KV_HEADS)):
        start = pl.multiple_of(t * LT, LT)
        for kv in kvs:
            buf[:, kv * gw:(kv + 1) * gw] = jnp.dot(
                k_ref[0, kv, pl.ds(start, LT), :], qt_ref[0, 0, kv], preferred_element_type=F32)

    def attend(with_ties, thr, need):
        def attend_tile(t, buf, carry, nxt=None):
            n_eq, n_sel, ms, accs = carry
            sc = sc_ref[t]
            if with_ties:
                eq = sc == thr
                pref = jnp.dot(tri_ref[...], jnp.where(eq, 1.0, 0.0).astype(BF),
                               preferred_element_type=F32)
                sel = (sc > thr) | (eq & (n_eq + pref - 1.0 < need))
                n_eq = n_eq + pref[LT - 1:LT, :]
            else:
                sel = sc >= thr
            n_sel = n_sel + fold(jnp.where(sel, 1, 0).astype(I32), jnp.sum)
            keep01 = jnp.where(sel, 1.0, 0.0).astype(BF)
            keep01 = jnp.concatenate([keep01] * GROUP, axis=1)
            new_m, new_acc = [None] * N_KV_HEADS, [None] * N_KV_HEADS
            for half in (range(0, N_KV_HEADS // 2), range(N_KV_HEADS // 2, N_KV_HEADS)):
                if nxt is not None:
                    qk_dots(nxt[0], nxt[1], half)
                ps, alphas = {}, {}
                for kv in half:
                    lg = buf[:, kv * gw:(kv + 1) * gw]
                    m_new = jnp.maximum(ms[kv], jnp.max(lg, axis=0, keepdims=True))
                    ps[kv] = jnp.exp2(lg - m_new).astype(BF) * keep01
                    alphas[kv] = jnp.exp2(ms[kv] - m_new)
                    new_m[kv] = m_new
                for kv in half:
                    pv = jnp.dot(vt_ref[0, t, kv * V_ROWS:(kv + 1) * V_ROWS, :], ps[kv],
                                 preferred_element_type=F32)
                    new_acc[kv] = alphas[kv] * accs[kv] + pv
            return n_eq, n_sel, tuple(new_m), tuple(new_acc)

        def attend_pair(i, carry):
            t0 = 2 * i
            carry = attend_tile(t0, mma_ref, carry, (t0 + 1, mmb_ref))
            return attend_tile(t0 + 1, mmb_ref, carry, (jnp.minimum(t0 + 2, last), mma_ref))

        init = (jnp.zeros((1, TQ), F32), jnp.zeros((SUBLANES, TQ), I32),
                tuple(jnp.full((1, gw), MASKED_LOGIT, F32) for _ in range(N_KV_HEADS)),
                tuple(jnp.zeros((V_ROWS, gw), F32) for _ in range(N_KV_HEADS)))
        qk_dots(0, mma_ref)
        carry = lax.fori_loop(0, n_pairs, attend_pair, init)
        _, n_sel, _, accs = lax.cond(odd, lambda c: attend_tile(last, mma_ref, c), keep, carry)
        return accs, jnp.sum(n_sel, axis=0, keepdims=True)

    accs, n_sel = lax.cond(n_tie > 0, lambda: attend(True, thr, need),
                           lambda: attend(False, thr, need))

    def attend_exact(thr, need):
        def tile(t, carry):
            n_eq, ms, accs = carry
            sc = sc_ref[t]
            eq = sc == thr
            pref = jnp.dot(tri_ref[...], jnp.where(eq, 1.0, 0.0).astype(BF),
                           preferred_element_type=F32)
            sel = (sc > thr) | (eq & (n_eq + pref - 1.0 < need))
            n_eq = n_eq + pref[LT - 1:LT, :]
            bias = jnp.concatenate([jnp.where(sel, 0.0, MASKED_LOGIT)] * GROUP, axis=1)
            start = pl.multiple_of(t * LT, LT)
            new_m, new_acc = [], []
            for kv in range(N_KV_HEADS):
                lg = jnp.dot(k_ref[0, kv, pl.ds(start, LT), :], qt_ref[0, 0, kv],
                             preferred_element_type=F32) + bias
                m_new = jnp.maximum(ms[kv], jnp.max(lg, axis=0, keepdims=True))
                p = jnp.exp2(lg - m_new).astype(BF)
                pv = jnp.dot(vt_ref[0, t, kv * V_ROWS:(kv + 1) * V_ROWS, :], p,
                             preferred_element_type=F32)
                new_acc.append(jnp.exp2(ms[kv] - m_new) * accs[kv] + pv)
                new_m.append(m_new)
            return n_eq, tuple(new_m), tuple(new_acc)

        init = (jnp.zeros((1, TQ), F32),
                tuple(jnp.full((1, gw), MASKED_LOGIT, F32) for _ in range(N_KV_HEADS)),
                tuple(jnp.zeros((V_ROWS, gw), F32) for _ in range(N_KV_HEADS)))
        return lax.fori_loop(0, n_t, tile, init)[2]

    n_want = jnp.where(short, n_adm, k_f)
    n_miss = jnp.sum(jnp.where(n_sel.astype(F32) != n_want, 1, 0))
    n_tiny = sum(jnp.sum(jnp.where(a[HEAD_DIM:HEAD_DIM + 1] >= MIN_SOFTMAX_SUM, 0, 1)) for a in accs)

    def redo():
        thr_b, need_b, _ = bisect_bits()
        return attend_exact(thr_b, need_b)

    miss = jnp.logical_and(n_miss > 0, jnp.logical_not(use_bits))
    accs = lax.cond(jnp.logical_or(miss, n_tiny > 0), redo, lambda: accs)

    for kv in range(N_KV_HEADS):
        o_t = accs[kv][:HEAD_DIM] / accs[kv][HEAD_DIM:HEAD_DIM + 1]
        for gi in range(GROUP):
            hh = GROUP * kv + gi
            bot_ref[hh * HEAD_DIM:(hh + 1) * HEAD_DIM, :] = o_t[:, gi * TQ:(gi + 1) * TQ]
    o_ref[0] = (bot_ref[...].T * jax.nn.silu(zb_ref[0])).astype(BF)


def _attn_call(qt, qit, wit, zb, k_hm, vt, ki_b, *, topk, causal, stream_len=0, n_valid=0):
    b, nq = qt.shape[0], qt.shape[1]
    s = nq * TQ
    n_tiles = vt.shape[1]
    l_all = k_hm.shape[2]
    assert l_all == n_tiles * LT
    tri = (lax.broadcasted_iota(I32, (LT, LT), 0) >= lax.broadcasted_iota(I32, (LT, LT), 1)).astype(BF)
    whole = lambda shape: pl.BlockSpec((1,) + shape, lambda i, j: (i,) + (0,) * len(shape),
                                       pipeline_mode=pl.Buffered(1))
    return pl.pallas_call(
        functools.partial(_attn_kernel, n_tiles=n_tiles, topk=topk, causal=causal,
                          stream_len=stream_len, n_valid=n_valid),
        grid=(b, nq),
        in_specs=[
            pl.BlockSpec((1, 1, N_KV_HEADS, HEAD_DIM, GROUP * TQ), lambda i, j: (i, j, 0, 0, 0)),
            pl.BlockSpec((1, 1, IDX_DIM, N_IDX_HEADS * TQ), lambda i, j: (i, j, 0, 0)),
            pl.BlockSpec((1, N_IDX_HEADS, TQ), lambda i, j: (i, 0, j)),
            pl.BlockSpec((1, TQ, ATT_WIDTH), lambda i, j: (i, j, 0)),
            whole((N_KV_HEADS, l_all, HEAD_DIM)),
            whole((n_tiles, N_KV_HEADS * V_ROWS, LT)),
            whole((l_all, IDX_DIM)),
            _const_spec((LT, LT)),
        ],
        out_specs=pl.BlockSpec((1, TQ, ATT_WIDTH), lambda i, j: (i, j, 0)),
        out_shape=jax.ShapeDtypeStruct((b, s, ATT_WIDTH), BF),
        scratch_shapes=[pltpu.VMEM((n_tiles, LT, TQ), F32), pltpu.VMEM((ATT_WIDTH, TQ), F32),
                        pltpu.VMEM((LT, N_IDX_HEADS * TQ), F32), pltpu.VMEM((LT, N_IDX_HEADS * TQ), F32)],
        compiler_params=pltpu.CompilerParams(
            dimension_semantics=("arbitrary", "arbitrary"), vmem_limit_bytes=VMEM_LIMIT),
        name="attn_prompt" if causal else "attn_sample",
    )(qt, qit, wit, zb, k_hm, vt, ki_b, tri)


def _merge_kernel(x_ref, ma_ref, gb_ref, bg_ref, wpb_ref, wout_ref, fg_ref, y_ref):
    pb = jnp.dot(bg_ref[0], wpb_ref[...], preferred_element_type=F32)
    m = ma_ref[0].astype(F32) + jax.nn.sigmoid(gb_ref[0]) * pb
    xo = x_ref[0] + jnp.dot(m.astype(BF), wout_ref[...], preferred_element_type=F32)
    ms = jnp.mean(xo * xo, axis=-1, keepdims=True)
    y_ref[0] = xo * lax.rsqrt(ms + RMS_EPS) * fg_ref[...]


def _merge_call(x, ma, gb, bg, wts, *, tt):
    b, s, _ = x.shape
    tok = lambda w: pl.BlockSpec((1, tt, w), lambda i, j: (i, j, 0))
    consts = [wts["w_pb"], wts["w_out"], wts["final_g"]]
    return pl.pallas_call(
        _merge_kernel,
        grid=(b, s // tt),
        in_specs=[tok(D_MODEL), tok(D_MODEL), tok(D_MODEL), tok(ATT_WIDTH)]
                 + [_const_spec(c.shape) for c in consts],
        out_specs=tok(D_MODEL),
        out_shape=jax.ShapeDtypeStruct((b, s, D_MODEL), F32),
        compiler_params=pltpu.CompilerParams(
            dimension_semantics=("arbitrary", "arbitrary"), vmem_limit_bytes=VMEM_LIMIT),
        name="merge",
    )(x, ma, gb, bg, *consts)


def _prep_weights(norm_g, w_in, ln_g, ln_b, w_pa, w_pb, w_out, final_g):
    o = 0
    cols = {}
    for name, width in (("u", SGU_WIDTH), ("v", SGU_WIDTH), ("za", SGU_WIDTH), ("q", ATT_WIDTH),
                        ("k", KV_WIDTH), ("vv", KV_WIDTH), ("zb", ATT_WIDTH),
                        ("qi", N_IDX_HEADS * IDX_DIM), ("ki", IDX_DIM), ("wi", N_IDX_HEADS),
                        ("ga", D_MODEL), ("gb", D_MODEL)):
        cols[name] = (o, o + width)
        o += width
    wb = w_in.astype(BF)
    sl = lambda a, z: wb[:, cols[a][0]:cols[z][1]]
    return {
        "norm_g": norm_g.reshape(1, D_MODEL), "w_uvz": sl("u", "za"),
        "w_nat": jnp.concatenate([sl("k", "zb"), sl("ki", "ki")], axis=1),
        "w_g": sl("ga", "gb"),
        "w_t": jnp.concatenate([sl("q", "q"), sl("qi", "qi"), sl("vv", "vv"), sl("wi", "wi")], axis=1).T,
        "ln_g": ln_g.reshape(1, SGU_WIDTH), "ln_b": ln_b.reshape(1, SGU_WIDTH),
        "w_pa": w_pa.astype(BF), "w_pb": w_pb.astype(BF), "w_out": w_out.astype(BF),
        "final_g": final_g.reshape(1, D_MODEL),
    }


def _mix_operands(w_mix, b_pos):
    pairs = w_mix.reshape(SGU_GROUPS // 2, 2, SGU_LEN, SGU_LEN)
    wmix = jnp.concatenate([pairs[:, 0], pairs[:, 1]], axis=-1).astype(BF)
    bmix = jnp.repeat(b_pos, SGU_GDIM, axis=1)
    return wmix, bmix


def kernel(x_prompt, x_sample, cache_k, cache_v, cache_kidx, norm_g, w_in, sgu_ln_g, sgu_ln_b,
           sgu_w, sgu_b, w_pa, w_pb, w_out, final_g):
    b, s, _ = x_prompt.shape
    db, ds, _ = x_sample.shape
    past = cache_k.shape[2]
    assert norm_g.shape[0] == 1 and db * ds == TQ and s % LT == 0
    wts = _prep_weights(norm_g[0], w_in[0], sgu_ln_g[0], sgu_ln_b[0], w_pa[0], w_pb[0], w_out[0],
                        final_g)

    pos = jnp.arange(SGU_LEN)
    chunk_ok = (pos[None, :] // CHUNK) <= (pos[:, None] // CHUNK)
    wmix_p, bmix_p = _mix_operands(jnp.where(chunk_ok[None], sgu_w[0], 0.0), sgu_b[0].T)
    (ma, gb, zb, k, v, ki, k_hm, ki_b, qt, qit, wit, vt) = _proj_call(
        x_prompt, wts, wmix_p, bmix_p, tt=4 * TQ, lt=LT, emit_vn=False)
    bg = _attn_call(qt, qit, wit, zb, k_hm, vt, ki_b, topk=min(TOPK_MAX, s // 4), causal=True)
    y_prompt = _merge_call(x_prompt, ma, gb, bg, wts, tt=512)

    assert past % CHUNK == 0 and ds <= CHUNK
    w_blk = jnp.einsum("ab,gij->gaibj", jnp.eye(db, dtype=F32), sgu_w[0][:, :ds, :ds])
    wmix_s, bmix_s = _mix_operands(w_blk.reshape(SGU_GROUPS, SGU_LEN, SGU_LEN),
                                   jnp.tile(sgu_b[0][:, :ds].T, (db, 1)))
    n_tok = db * ds
    (ma_s, gb_s, zb_s, k_s, v_s, ki_s, _, _, qt_s, qit_s, wit_s, _, vn_s) = _proj_call(
        x_sample.reshape(1, n_tok, D_MODEL), wts, wmix_s, bmix_s, tt=n_tok, lt=n_tok, emit_vn=True)
    n_keys = past + ds
    slab = -(-n_keys * db // LT) * LT // db
    assert (slab * db) % LT == 0 and slab >= n_keys
    cat = lambda c, n, w: jnp.pad(
        jnp.concatenate([c.reshape(db, past, w), n.reshape(db, ds, w)], axis=1),
        ((0, 0), (0, slab - n_keys), (0, 0))).reshape(db * slab, w).astype(BF)
    k_all, v_all, ki_all = (cat(cache_k[0], k_s, KV_WIDTH), cat(cache_v[0], v_s, KV_WIDTH),
                            cat(cache_kidx[0], ki_s, IDX_DIM))
    n_t_s = db * slab // LT
    k_hm_s = k_all.reshape(1, db * slab, N_KV_HEADS, HEAD_DIM).transpose(0, 2, 1, 3)
    vt_s = v_all.reshape(n_t_s, LT, N_KV_HEADS, HEAD_DIM).transpose(0, 2, 3, 1)
    vt_s = jnp.concatenate([vt_s, jnp.ones((n_t_s, N_KV_HEADS, BF16_ROWS, LT), BF)], axis=2)
    vt_s = vt_s.reshape(1, n_t_s, N_KV_HEADS * V_ROWS, LT)
    bg_s = _attn_call(qt_s, qit_s, wit_s, zb_s, k_hm_s, vt_s, ki_all[None],
                      topk=min(TOPK_MAX, n_keys // 4), causal=False, stream_len=ds, n_valid=n_keys)
    y_sample = _merge_call(x_sample.reshape(1, n_tok, D_MODEL), ma_s, gb_s, bg_s, wts, tt=n_tok)

    return (y_prompt, y_sample.reshape(db, ds, D_MODEL),
            k.reshape(1, b, s, N_KV_HEADS, HEAD_DIM), v.reshape(1, b, s, N_KV_HEADS, HEAD_DIM),
            ki.reshape(1, b, s, IDX_DIM),
            k_s.reshape(1, db, ds, N_KV_HEADS, HEAD_DIM), v_s.reshape(1, db, ds, N_KV_HEADS, HEAD_DIM),
            ki_s.reshape(1, db, ds, IDX_DIM), vn_s.reshape(1, db, ds, SGU_WIDTH))
```

```python
import functools

import jax
import jax.numpy as jnp
from jax import lax
from jax.experimental import pallas as pl
from jax.experimental.pallas import tpu as pltpu

D_MODEL = 1024
CHUNK = 64
SGU_LEN = 128
SGU_GROUPS = 8
SGU_WIDTH = 512
SGU_GDIM = SGU_WIDTH // SGU_GROUPS
N_HEADS = 8
N_KV_HEADS = 4
HEAD_DIM = 64
GROUP = N_HEADS // N_KV_HEADS
ATT_WIDTH = N_HEADS * HEAD_DIM
KV_WIDTH = N_KV_HEADS * HEAD_DIM
N_IDX_HEADS = 8
IDX_DIM = 64
TOPK_MAX = 256
RMS_EPS = 1e-6
LN_EPS = 1e-5

BF = jnp.bfloat16
F32 = jnp.float32
I32 = jnp.int32

LANES = 128
SUBLANES = 8
BF16_ROWS = 16
TQ = LANES
LT = 512
V_ROWS = HEAD_DIM + BF16_ROWS
VMEM_LIMIT = 52 * 1024 * 1024
MASKED_LOGIT = -1e30
LOWEST = -3.4028235e38
MIN_SOFTMAX_SUM = 1e-30
ALL_TIES = 3e38
UNCHECKED_SEARCH_PASSES = 8
MAX_SEARCH_PASSES = 28
MAX_WALK_STEPS = 8
SEARCH_CLIP = 0.05
LOG2_E = 1.4426950408889634
INT_MIN = -2147483648


def _const_spec(shape):
    nd = len(shape)
    return pl.BlockSpec(shape, lambda *_: (0,) * nd, pipeline_mode=pl.Buffered(1))


def _proj_kernel(x_ref, ng_ref, wuvz_ref, wnat_ref, wg_ref, wt_ref,
                 lng_ref, lnb_ref, wmix_ref, bmix_ref, wpa_ref,
                 ma_ref, gb_ref, zb_ref, k_ref, v_ref, ki_ref, khm_ref, kib_ref,
                 qt_ref, qit_ref, wit_ref, vt_ref, *vn_refs, n_sub):
    x = x_ref[0]
    ms = jnp.mean(x * x, axis=-1, keepdims=True)
    h = (x * lax.rsqrt(ms + RMS_EPS) * ng_ref[...]).astype(BF)

    uvz = jnp.dot(h, wuvz_ref[...], preferred_element_type=F32)
    u = jax.nn.gelu(uvz[:, :SGU_WIDTH])
    v = jax.nn.gelu(uvz[:, SGU_WIDTH:2 * SGU_WIDTH])
    za = uvz[:, 2 * SGU_WIDTH:]
    mu = jnp.mean(v, axis=-1, keepdims=True)
    var = jnp.mean(jnp.square(v - mu), axis=-1, keepdims=True)
    vn = (v - mu) * lax.rsqrt(var + LN_EPS) * lng_ref[...] + lnb_ref[...]
    if vn_refs:
        vn_refs[0][0] = vn
    lane = lax.broadcasted_iota(I32, (SGU_LEN, LANES), 1)
    cols = []
    for c in range(SGU_WIDTH // LANES):
        rows = []
        for s in range(n_sub):
            blk = vn[s * SGU_LEN:(s + 1) * SGU_LEN, c * LANES:(c + 1) * LANES]
            lo = jnp.where(lane < SGU_GDIM, blk, 0.0).astype(BF)
            hi = jnp.where(lane >= SGU_GDIM, blk, 0.0).astype(BF)
            rhs = jnp.concatenate([lo, hi], axis=0)
            rows.append(jnp.dot(wmix_ref[c], rhs, preferred_element_type=F32))
        cols.append(jnp.concatenate(rows, axis=0) if n_sub > 1 else rows[0])
    mixed = jnp.concatenate(cols, axis=1)
    bias = bmix_ref[...]
    if n_sub > 1:
        bias = jnp.concatenate([bias] * n_sub, axis=0)
    a_out = u * (mixed + bias) * jax.nn.silu(za)
    pa = jnp.dot(a_out.astype(BF), wpa_ref[...], preferred_element_type=F32)
    g = jnp.dot(h, wg_ref[...], preferred_element_type=F32)
    ma_ref[0] = (jax.nn.sigmoid(g[:, :D_MODEL]) * pa).astype(BF)
    gb_ref[0] = g[:, D_MODEL:]

    nat = jnp.dot(h, wnat_ref[...], preferred_element_type=F32)
    k = nat[:, :KV_WIDTH]
    k_ref[0] = k
    v_ref[0] = nat[:, KV_WIDTH:2 * KV_WIDTH]
    zb_ref[0] = nat[:, 2 * KV_WIDTH:2 * KV_WIDTH + ATT_WIDTH]
    ki = nat[:, 2 * KV_WIDTH + ATT_WIDTH:]
    ki_ref[0] = ki
    kib_ref[0] = ki.astype(BF)
    for kv in range(N_KV_HEADS):
        khm_ref[0, kv] = k[:, kv * HEAD_DIM:(kv + 1) * HEAD_DIM].astype(BF)

    tm = lax.dot_general(wt_ref[...], h, (((1,), (1,)), ((), ())),
                         preferred_element_type=F32)
    o_qi, o_v, o_wi = ATT_WIDTH, 2 * ATT_WIDTH, 2 * ATT_WIDTH + KV_WIDTH
    for jb in range(n_sub):
        tok = slice(jb * TQ, (jb + 1) * TQ)
        for hh in range(N_HEADS):
            kv, gi = divmod(hh, GROUP)
            qt_ref[0, jb, kv, :, gi * TQ:(gi + 1) * TQ] = (
                tm[hh * HEAD_DIM:(hh + 1) * HEAD_DIM, tok] * (HEAD_DIM ** -0.5 * LOG2_E)).astype(BF)
        for hh in range(N_IDX_HEADS):
            qit_ref[0, jb, :, hh * TQ:(hh + 1) * TQ] = (
                tm[o_qi + hh * IDX_DIM:o_qi + (hh + 1) * IDX_DIM, tok] * (IDX_DIM ** -0.5)).astype(BF)
    for kv in range(N_KV_HEADS):
        vt_ref[0, 0, kv * V_ROWS:kv * V_ROWS + HEAD_DIM, :] = (
            tm[o_v + kv * HEAD_DIM:o_v + (kv + 1) * HEAD_DIM, :].astype(BF))
        vt_ref[0, 0, kv * V_ROWS + HEAD_DIM:(kv + 1) * V_ROWS, :] = jnp.ones(
            (BF16_ROWS, tm.shape[1]), BF)
    wit_ref[0] = tm[o_wi:o_wi + N_IDX_HEADS, :] * (N_IDX_HEADS ** -0.5)


def _proj_call(x, wts, wmix, bmix, *, tt, lt, emit_vn):
    b, s, _ = x.shape
    n_sub = tt // SGU_LEN
    r = lt // tt
    grid = (b, s // tt)
    tok = lambda w, dt=F32: (jax.ShapeDtypeStruct((b, s, w), dt),
                             pl.BlockSpec((1, tt, w), lambda i, j: (i, j, 0)))
    outs = [
        tok(D_MODEL, BF),
        tok(D_MODEL),
        tok(ATT_WIDTH),
        tok(KV_WIDTH), tok(KV_WIDTH), tok(IDX_DIM),
        (jax.ShapeDtypeStruct((b, N_KV_HEADS, s, HEAD_DIM), BF),
         pl.BlockSpec((1, N_KV_HEADS, tt, HEAD_DIM), lambda i, j: (i, 0, j, 0))),
        tok(IDX_DIM, BF),
        (jax.ShapeDtypeStruct((b, s // TQ, N_KV_HEADS, HEAD_DIM, GROUP * TQ), BF),
         pl.BlockSpec((1, n_sub, N_KV_HEADS, HEAD_DIM, GROUP * TQ), lambda i, j: (i, j, 0, 0, 0))),
        (jax.ShapeDtypeStruct((b, s // TQ, IDX_DIM, N_IDX_HEADS * TQ), BF),
         pl.BlockSpec((1, n_sub, IDX_DIM, N_IDX_HEADS * TQ), lambda i, j: (i, j, 0, 0))),
        (jax.ShapeDtypeStruct((b, N_IDX_HEADS, s), F32),
         pl.BlockSpec((1, N_IDX_HEADS, tt), lambda i, j: (i, 0, j))),
        (jax.ShapeDtypeStruct((b, s // lt, N_KV_HEADS * V_ROWS, lt), BF),
         pl.BlockSpec((1, 1, N_KV_HEADS * V_ROWS, tt), lambda i, j: (i, j // r, 0, j % r))),
    ]
    if emit_vn:
        outs.append(tok(SGU_WIDTH))
    consts = [wts["norm_g"], wts["w_uvz"], wts["w_nat"], wts["w_g"], wts["w_t"],
              wts["ln_g"], wts["ln_b"], wmix, bmix, wts["w_pa"]]
    return pl.pallas_call(
        functools.partial(_proj_kernel, n_sub=n_sub),
        grid=grid,
        in_specs=[pl.BlockSpec((1, tt, D_MODEL), lambda i, j: (i, j, 0))]
                 + [_const_spec(c.shape) for c in consts],
        out_specs=[o[1] for o in outs],
        out_shape=[o[0] for o in outs],
        compiler_params=pltpu.CompilerParams(
            dimension_semantics=("arbitrary", "arbitrary"), vmem_limit_bytes=VMEM_LIMIT),
        name="proj_vn" if emit_vn else "proj",
    )(x, *consts)


def _key_to_f32(key_u):
    s = key_u ^ INT_MIN
    bits = s ^ (lax.shift_right_arithmetic(s, 31) & 0x7FFFFFFF)
    return lax.bitcast_convert_type(bits, F32)


def _f32_to_key(x):
    bits = lax.bitcast_convert_type(x, I32)
    return bits ^ (lax.shift_right_arithmetic(bits, 31) & 0x7FFFFFFF) ^ INT_MIN


def _attn_kernel(qt_ref, qit_ref, wit_ref, zb_ref, k_ref, vt_ref, ki_ref, tri_ref,
                 o_ref, sc_ref, bot_ref, mma_ref, mmb_ref, *, n_tiles, topk, causal, stream_len,
                 n_valid):
    lane = lax.broadcasted_iota(I32, (1, TQ), 1)
    if causal:
        q0 = pl.program_id(1) * TQ
        n_t = lax.div(q0 + TQ + LT - 1, LT)
        key_lo = jnp.zeros((1, TQ), I32)
        key_hi = (lax.shift_right_logical(q0 + lane, 6) + 1) * CHUNK
    else:
        n_t = n_tiles
        slab = (n_tiles * LT) // (TQ // stream_len)
        key_lo = lax.div(lane, stream_len) * slab
        key_hi = key_lo + n_valid

    qit = qit_ref[0, 0]
    wit = wit_ref[0]
    wrows = [wit[hh:hh + 1, :] for hh in range(N_IDX_HEADS)]
    row0 = lax.broadcasted_iota(I32, (LT, 1), 0)

    fold = lambda x, op: op(x.reshape(LT // SUBLANES, SUBLANES, TQ), axis=0)

    last = n_t - 1
    n_pairs = lax.div(n_t, 2)
    odd = n_t - 2 * n_pairs == 1
    keep = lambda c: c

    def idx_dot(t, buf):
        start = pl.multiple_of(t * LT, LT)
        buf[...] = jnp.dot(ki_ref[0, pl.ds(start, LT), :], qit, preferred_element_type=F32)

    def score_tile(t, buf, carry):
        mx, mn, n_nonneg, n_pos = carry
        sc = jnp.zeros((LT, TQ), F32)
        for hh in range(N_IDX_HEADS):
            sc = sc + jnp.maximum(buf[:, hh * TQ:(hh + 1) * TQ], 0.0) * wrows[hh]
        pos = row0 + t * LT
        ok = pos < key_hi if causal else (pos >= key_lo) & (pos < key_hi)
        mn = jnp.minimum(mn, fold(sc, jnp.min))
        sc = jnp.where(ok, sc, -jnp.inf)
        sc_ref[t] = sc
        ones = lambda m: fold(jnp.where(m, 1, 0).astype(I32), jnp.sum)
        return (jnp.maximum(mx, fold(sc, jnp.max)), mn,
                n_nonneg + ones(sc >= 0.0), n_pos + ones(sc > 0.0))

    def score_pair(i, carry):
        t0 = 2 * i
        idx_dot(t0 + 1, mmb_ref)
        carry = score_tile(t0, mma_ref, carry)
        idx_dot(jnp.minimum(t0 + 2, last), mma_ref)
        return score_tile(t0 + 1, mmb_ref, carry)

    idx_dot(0, mma_ref)
    stats = lax.fori_loop(
        0, n_pairs, score_pair,
        (jnp.full((SUBLANES, TQ), -jnp.inf, F32), jnp.full((SUBLANES, TQ), jnp.inf, F32),
         jnp.zeros((SUBLANES, TQ), I32), jnp.zeros((SUBLANES, TQ), I32)))
    mx, mn, n_nonneg, n_pos = lax.cond(
        odd, lambda c: score_tile(last, mma_ref, c), keep, stats)
    mx = jnp.max(mx, axis=0, keepdims=True)
    mn = jnp.min(mn, axis=0, keepdims=True)
    n_nonneg = jnp.sum(n_nonneg, axis=0, keepdims=True).astype(F32)
    n_pos = jnp.sum(n_pos, axis=0, keepdims=True).astype(F32)

    def count(pred):
        ones = lambda t: fold(jnp.where(pred(sc_ref[t]), 1, 0).astype(I32), jnp.sum)

        acc = lax.fori_loop(0, n_pairs, lambda i, a: a + ones(2 * i) + ones(2 * i + 1),
                            jnp.zeros((SUBLANES, TQ), I32))
        acc = lax.cond(odd, lambda a: a + ones(last), keep, acc)
        return jnp.sum(acc, axis=0, keepdims=True)

    k_f = float(topk)
    n_adm = (key_hi - key_lo).astype(F32)
    short = n_adm < k_f

    def bisect_bits():
        def bit_pass(i, carry):
            thr_u, n_ge = carry
            cand_u = thr_u | lax.shift_left(jnp.int32(1), 31 - i)
            cand = _key_to_f32(cand_u)
            cnt = count(lambda sc: sc >= cand)
            take = cnt >= topk
            return jnp.where(take, cand_u, thr_u), jnp.where(take, cnt, n_ge)

        thr_u, n_ge = lax.fori_loop(
            0, 32, bit_pass, (jnp.zeros((1, TQ), I32), jnp.zeros((1, TQ), I32)))
        thr = jnp.where(short, LOWEST, _key_to_f32(thr_u))
        n_gt = count(lambda sc: sc > thr)
        need = jnp.where(short, ALL_TIES, (topk - n_gt).astype(F32))
        tie = jnp.logical_and(jnp.logical_not(short), n_ge > topk)
        return thr, need, jnp.max(jnp.where(tie, 1, 0))

    count_f = lambda pred: count(pred).astype(F32)
    zero_thr = jnp.logical_and(n_pos < k_f, n_nonneg >= k_f)
    pos_side = n_pos >= k_f
    lo0 = jnp.where(pos_side, 0.0, mn)
    hi0 = jnp.where(pos_side, mx, 0.0)
    clo0 = jnp.where(pos_side, n_nonneg, n_adm)
    chi0 = jnp.where(pos_side, 0.0, n_nonneg)
    logk = jnp.log(k_f)
    f_of = lambda c: jnp.log(jnp.maximum(c, 0.5)) - logk
    done0 = jnp.where(short | zero_thr | (clo0 == k_f), 1, 0)

    def search_pass(st):
        lo, hi, clo, chi, flo, fhi, side, done = st
        frac = jnp.clip(flo / (flo - fhi), SEARCH_CLIP, 1.0 - SEARCH_CLIP)
        cand = lo + (hi - lo) * frac
        cnt = count_f(lambda sc: sc >= cand)
        ge = cnt >= k_f
        up_lo = jnp.logical_and(done == 0, ge)
        up_hi = jnp.logical_and(done == 0, jnp.logical_not(ge))
        fhi = jnp.where(up_lo & (side == 1), fhi * 0.5, fhi)
        flo = jnp.where(up_hi & (side == -1), flo * 0.5, flo)
        lo = jnp.where(up_lo, cand, lo)
        clo = jnp.where(up_lo, cnt, clo)
        flo = jnp.where(up_lo, f_of(cnt), flo)
        hi = jnp.where(up_hi, cand, hi)
        chi = jnp.where(up_hi, cnt, chi)
        fhi = jnp.where(up_hi, f_of(cnt), fhi)
        side = jnp.where(up_lo, 1, jnp.where(up_hi, -1, side))
        done = jnp.where(clo == k_f, 1, done)
        return lo, hi, clo, chi, flo, fhi, side, done

    def walk_plan(st):
        lo, _, clo, chi = st[:4]
        from_hi = k_f - chi
        from_lo = clo - k_f + 1.0
        down = jnp.logical_or(from_hi <= from_lo, lo == 0.0)
        return down, jnp.where(st[7] == 1, 0.0, jnp.where(down, from_hi, from_lo))

    most_steps = lambda st: jnp.max(walk_plan(st)[1]).astype(I32)

    st = lax.fori_loop(0, UNCHECKED_SEARCH_PASSES, lambda i, s: search_pass(s),
                       (lo0, hi0, clo0, chi0, f_of(clo0), f_of(chi0), jnp.zeros((1, TQ), I32), done0))

    def checked_pass(c):
        s = search_pass(c[2])
        return c[0] + 1, most_steps(s), s

    _, n_steps, st = lax.while_loop(
        lambda c: jnp.logical_and(c[0] < MAX_SEARCH_PASSES, c[1] > MAX_WALK_STEPS), checked_pass,
        (jnp.int32(UNCHECKED_SEARCH_PASSES), most_steps(st), st))

    def walked():
        lo, hi = st[0], st[1]
        down, steps = walk_plan(st)
        sign = jnp.where(down, 1.0, -1.0)
        bound0 = jnp.where(down, hi, _key_to_f32(_f32_to_key(-lo) + 1))

        def step(i, bound):
            def below(t):
                y = sc_ref[t] * sign
                return fold(jnp.where(y < bound, y, -jnp.inf), jnp.max)

            nxt = lax.fori_loop(
                0, n_pairs, lambda j, a: jnp.maximum(a, jnp.maximum(below(2 * j), below(2 * j + 1))),
                jnp.full((SUBLANES, TQ), -jnp.inf, F32))
            nxt = lax.cond(odd, lambda a: jnp.maximum(a, below(last)), keep, nxt)
            return jnp.where(steps > i.astype(F32), jnp.max(nxt, axis=0, keepdims=True), bound)

        found = lax.fori_loop(0, n_steps, step, bound0) * sign
        thr = jnp.where(short, LOWEST,
                        jnp.where(zero_thr, 0.0, jnp.where(st[7] == 1, lo, found)))
        tie = jnp.logical_and(zero_thr, n_nonneg > k_f)
        need = jnp.where(tie, k_f - n_pos, ALL_TIES)
        return thr, need, jnp.max(jnp.where(tie, 1, 0))

    use_bits = n_steps > MAX_WALK_STEPS
    thr, need, n_tie = lax.cond(use_bits, bisect_bits, walked)

    gw = GROUP * TQ

    def qk_dots(t, buf, kvs=range(N_KV_HEADS)):
        start = pl.multiple_of(t * LT, LT)
        for kv in kvs:
            buf[:, kv * gw:(kv + 1) * gw] = jnp.dot(
                k_ref[0, kv, pl.ds(start, LT), :], qt_ref[0, 0, kv], preferred_element_type=F32)

    def attend(with_ties, thr, need):
        def attend_tile(t, buf, carry, nxt=None):
            n_eq, n_sel, ms, accs = carry
            sc = sc_ref[t]
            if with_ties:
                eq = sc == thr
                pref = jnp.dot(tri_ref[...], jnp.where(eq, 1.0, 0.0).astype(BF),
                               preferred_element_type=F32)
                sel = (sc > thr) | (eq & (n_eq + pref - 1.0 < need))
                n_eq = n_eq + pref[LT - 1:LT, :]
            else:
                sel = sc >= thr
            n_sel = n_sel + fold(jnp.where(sel, 1, 0).astype(I32), jnp.sum)
            keep01 = jnp.where(sel, 1.0, 0.0).astype(BF)
            keep01 = jnp.concatenate([keep01] * GROUP, axis=1)
            new_m, new_acc = [None] * N_KV_HEADS, [None] * N_KV_HEADS
            for half in (range(0, N_KV_HEADS // 2), range(N_KV_HEADS // 2, N_KV_HEADS)):
                if nxt is not None:
                    qk_dots(nxt[0], nxt[1], half)
                ps, alphas = {}, {}
                for kv in half:
                    lg = buf[:, kv * gw:(kv + 1) * gw]
                    m_new = jnp.maximum(ms[kv], jnp.max(lg, axis=0, keepdims=True))
                    ps[kv] = jnp.exp2(lg - m_new).astype(BF) * keep01
                    alphas[kv] = jnp.exp2(ms[kv] - m_new)
                    new_m[kv] = m_new
                for kv in half:
                    pv = jnp.dot(vt_ref[0, t, kv * V_ROWS:(kv + 1) * V_ROWS, :], ps[kv],
                                 preferred_element_type=F32)
                    new_acc[kv] = alphas[kv] * accs[kv] + pv
            return n_eq, n_sel, tuple(new_m), tuple(new_acc)

        def attend_pair(i, carry):
            t0 = 2 * i
            carry = attend_tile(t0, mma_ref, carry, (t0 + 1, mmb_ref))
            return attend_tile(t0 + 1, mmb_ref, carry, (jnp.minimum(t0 + 2, last), mma_ref))

        init = (jnp.zeros((1, TQ), F32), jnp.zeros((SUBLANES, TQ), I32),
                tuple(jnp.full((1, gw), MASKED_LOGIT, F32) for _ in range(N_KV_HEADS)),
                tuple(jnp.zeros((V_ROWS, gw), F32) for _ in range(N_KV_HEADS)))
        qk_dots(0, mma_ref)
        carry = lax.fori_loop(0, n_pairs, attend_pair, init)
        _, n_sel, _, accs = lax.cond(odd, lambda c: attend_tile(last, mma_ref, c), keep, carry)
        return accs, jnp.sum(n_sel, axis=0, keepdims=True)

    accs, n_sel = lax.cond(n_tie > 0, lambda: attend(True, thr, need),
                           lambda: attend(False, thr, need))

    def attend_exact(thr, need):
        def tile(t, carry):
            n_eq, ms, accs = carry
            sc = sc_ref[t]
            eq = sc == thr
            pref = jnp.dot(tri_ref[...], jnp.where(eq, 1.0, 0.0).astype(BF),
                           preferred_element_type=F32)
            sel = (sc > thr) | (eq & (n_eq + pref - 1.0 < need))
            n_eq = n_eq + pref[LT - 1:LT, :]
            bias = jnp.concatenate([jnp.where(sel, 0.0, MASKED_LOGIT)] * GROUP, axis=1)
            start = pl.multiple_of(t * LT, LT)
            new_m, new_acc = [], []
            for kv in range(N_KV_HEADS):
                lg = jnp.dot(k_ref[0, kv, pl.ds(start, LT), :], qt_ref[0, 0, kv],
                             preferred_element_type=F32) + bias
                m_new = jnp.maximum(ms[kv], jnp.max(lg, axis=0, keepdims=True))
                p = jnp.exp2(lg - m_new).astype(BF)
                pv = jnp.dot(vt_ref[0, t, kv * V_ROWS:(kv + 1) * V_ROWS, :], p,
                             preferred_element_type=F32)
                new_acc.append(jnp.exp2(ms[kv] - m_new) * accs[kv] + pv)
                new_m.append(m_new)
            return n_eq, tuple(new_m), tuple(new_acc)

        init = (jnp.zeros((1, TQ), F32),
                tuple(jnp.full((1, gw), MASKED_LOGIT, F32) for _ in range(N_KV_HEADS)),
                tuple(jnp.zeros((V_ROWS, gw), F32) for _ in range(N_KV_HEADS)))
        return lax.fori_loop(0, n_t, tile, init)[2]

    n_want = jnp.where(short, n_adm, k_f)
    n_miss = jnp.sum(jnp.where(n_sel.astype(F32) != n_want, 1, 0))
    n_tiny = sum(jnp.sum(jnp.where(a[HEAD_DIM:HEAD_DIM + 1] >= MIN_SOFTMAX_SUM, 0, 1)) for a in accs)

    def redo():
        thr_b, need_b, _ = bisect_bits()
        return attend_exact(thr_b, need_b)

    miss = jnp.logical_and(n_miss > 0, jnp.logical_not(use_bits))
    accs = lax.cond(jnp.logical_or(miss, n_tiny > 0), redo, lambda: accs)

    for kv in range(N_KV_HEADS):
        o_t = accs[kv][:HEAD_DIM] / accs[kv][HEAD_DIM:HEAD_DIM + 1]
        for gi in range(GROUP):
            hh = GROUP * kv + gi
            bot_ref[hh * HEAD_DIM:(hh + 1) * HEAD_DIM, :] = o_t[:, gi * TQ:(gi + 1) * TQ]
    o_ref[0] = (bot_ref[...].T * jax.nn.silu(zb_ref[0])).astype(BF)


def _attn_call(qt, qit, wit, zb, k_hm, vt, ki_b, *, topk, causal, stream_len=0, n_valid=0):
    b, nq = qt.shape[0], qt.shape[1]
    s = nq * TQ
    n_tiles = vt.shape[1]
    l_all = k_hm.shape[2]
    assert l_all == n_tiles * LT
    tri = (lax.broadcasted_iota(I32, (LT, LT), 0) >= lax.broadcasted_iota(I32, (LT, LT), 1)).astype(BF)
    whole = lambda shape: pl.BlockSpec((1,) + shape, lambda i, j: (i,) + (0,) * len(shape),
                                       pipeline_mode=pl.Buffered(1))
    return pl.pallas_call(
        functools.partial(_attn_kernel, n_tiles=n_tiles, topk=topk, causal=causal,
                          stream_len=stream_len, n_valid=n_valid),
        grid=(b, nq),
        in_specs=[
            pl.BlockSpec((1, 1, N_KV_HEADS, HEAD_DIM, GROUP * TQ), lambda i, j: (i, j, 0, 0, 0)),
            pl.BlockSpec((1, 1, IDX_DIM, N_IDX_HEADS * TQ), lambda i, j: (i, j, 0, 0)),
            pl.BlockSpec((1, N_IDX_HEADS, TQ), lambda i, j: (i, 0, j)),
            pl.BlockSpec((1, TQ, ATT_WIDTH), lambda i, j: (i, j, 0)),
            whole((N_KV_HEADS, l_all, HEAD_DIM)),
            whole((n_tiles, N_KV_HEADS * V_ROWS, LT)),
            whole((l_all, IDX_DIM)),
            _const_spec((LT, LT)),
        ],
        out_specs=pl.BlockSpec((1, TQ, ATT_WIDTH), lambda i, j: (i, j, 0)),
        out_shape=jax.ShapeDtypeStruct((b, s, ATT_WIDTH), BF),
        scratch_shapes=[pltpu.VMEM((n_tiles, LT, TQ), F32), pltpu.VMEM((ATT_WIDTH, TQ), F32),
                        pltpu.VMEM((LT, N_IDX_HEADS * TQ), F32), pltpu.VMEM((LT, N_IDX_HEADS * TQ), F32)],
        compiler_params=pltpu.CompilerParams(
            dimension_semantics=("arbitrary", "arbitrary"), vmem_limit_bytes=VMEM_LIMIT),
        name="attn_prompt" if causal else "attn_sample",
    )(qt, qit, wit, zb, k_hm, vt, ki_b, tri)


def _merge_kernel(x_ref, ma_ref, gb_ref, bg_ref, wpb_ref, wout_ref, fg_ref, y_ref):
    pb = jnp.dot(bg_ref[0], wpb_ref[...], preferred_element_type=F32)
    m = ma_ref[0].astype(F32) + jax.nn.sigmoid(gb_ref[0]) * pb
    xo = x_ref[0] + jnp.dot(m.astype(BF), wout_ref[...], preferred_element_type=F32)
    ms = jnp.mean(xo * xo, axis=-1, keepdims=True)
    y_ref[0] = xo * lax.rsqrt(ms + RMS_EPS) * fg_ref[...]


def _merge_call(x, ma, gb, bg, wts, *, tt):
    b, s, _ = x.shape
    tok = lambda w: pl.BlockSpec((1, tt, w), lambda i, j: (i, j, 0))
    consts = [wts["w_pb"], wts["w_out"], wts["final_g"]]
    return pl.pallas_call(
        _merge_kernel,
        grid=(b, s // tt),
        in_specs=[tok(D_MODEL), tok(D_MODEL), tok(D_MODEL), tok(ATT_WIDTH)]
                 + [_const_spec(c.shape) for c in consts],
        out_specs=tok(D_MODEL),
        out_shape=jax.ShapeDtypeStruct((b, s, D_MODEL), F32),
        compiler_params=pltpu.CompilerParams(
            dimension_semantics=("arbitrary", "arbitrary"), vmem_limit_bytes=VMEM_LIMIT),
        name="merge",
    )(x, ma, gb, bg, *consts)


def _prep_weights(norm_g, w_in, ln_g, ln_b, w_pa, w_pb, w_out, final_g):
    o = 0
    cols = {}
    for name, width in (("u", SGU_WIDTH), ("v", SGU_WIDTH), ("za", SGU_WIDTH), ("q", ATT_WIDTH),
                        ("k", KV_WIDTH), ("vv", KV_WIDTH), ("zb", ATT_WIDTH),
                        ("qi", N_IDX_HEADS * IDX_DIM), ("ki", IDX_DIM), ("wi", N_IDX_HEADS),
                        ("ga", D_MODEL), ("gb", D_MODEL)):
        cols[name] = (o, o + width)
        o += width
    wb = w_in.astype(BF)
    sl = lambda a, z: wb[:, cols[a][0]:cols[z][1]]
    return {
        "norm_g": norm_g.reshape(1, D_MODEL), "w_uvz": sl("u", "za"),
        "w_nat": jnp.concatenate([sl("k", "zb"), sl("ki", "ki")], axis=1),
        "w_g": sl("ga", "gb"),
        "w_t": jnp.concatenate([sl("q", "q"), sl("qi", "qi"), sl("vv", "vv"), sl("wi", "wi")], axis=1).T,
        "ln_g": ln_g.reshape(1, SGU_WIDTH), "ln_b": ln_b.reshape(1, SGU_WIDTH),
        "w_pa": w_pa.astype(BF), "w_pb": w_pb.astype(BF), "w_out": w_out.astype(BF),
        "final_g": final_g.reshape(1, D_MODEL),
    }


def _mix_operands(w_mix, b_pos):
    pairs = w_mix.reshape(SGU_GROUPS // 2, 2, SGU_LEN, SGU_LEN)
    wmix = jnp.concatenate([pairs[:, 0], pairs[:, 1]], axis=-1).astype(BF)
    bmix = jnp.repeat(b_pos, SGU_GDIM, axis=1)
    return wmix, bmix


def kernel(x_prompt, x_sample, cache_k, cache_v, cache_kidx, norm_g, w_in, sgu_ln_g, sgu_ln_b,
           sgu_w, sgu_b, w_pa, w_pb, w_out, final_g):
    b, s, _ = x_prompt.shape
    db, ds, _ = x_sample.shape
    past = cache_k.shape[2]
    assert norm_g.shape[0] == 1 and db * ds == TQ and s % LT == 0
    wts = _prep_weights(norm_g[0], w_in[0], sgu_ln_g[0], sgu_ln_b[0], w_pa[0], w_pb[0], w_out[0],
                        final_g)

    pos = jnp.arange(SGU_LEN)
    chunk_ok = (pos[None, :] // CHUNK) <= (pos[:, None] // CHUNK)
    wmix_p, bmix_p = _mix_operands(jnp.where(chunk_ok[None], sgu_w[0], 0.0), sgu_b[0].T)
    (ma, gb, zb, k, v, ki, k_hm, ki_b, qt, qit, wit, vt) = _proj_call(
        x_prompt, wts, wmix_p, bmix_p, tt=4 * TQ, lt=LT, emit_vn=False)
    bg = _attn_call(qt, qit, wit, zb, k_hm, vt, ki_b, topk=min(TOPK_MAX, s // 4), causal=True)
    y_prompt = _merge_call(x_prompt, ma, gb, bg, wts, tt=1024)

    assert past % CHUNK == 0 and ds <= CHUNK
    w_blk = jnp.einsum("ab,gij->gaibj", jnp.eye(db, dtype=F32), sgu_w[0][:, :ds, :ds])
    wmix_s, bmix_s = _mix_operands(w_blk.reshape(SGU_GROUPS, SGU_LEN, SGU_LEN),
                                   jnp.tile(sgu_b[0][:, :ds].T, (db, 1)))
    n_tok = db * ds
    (ma_s, gb_s, zb_s, k_s, v_s, ki_s, _, _, qt_s, qit_s, wit_s, _, vn_s) = _proj_call(
        x_sample.reshape(1, n_tok, D_MODEL), wts, wmix_s, bmix_s, tt=n_tok, lt=n_tok, emit_vn=True)
    n_keys = past + ds
    slab = -(-n_keys * db // LT) * LT // db
    assert (slab * db) % LT == 0 and slab >= n_keys
    cat = lambda c, n, w: jnp.pad(
        jnp.concatenate([c.reshape(db, past, w), n.reshape(db, ds, w)], axis=1),
        ((0, 0), (0, slab - n_keys), (0, 0))).reshape(db * slab, w).astype(BF)
    k_all, v_all, ki_all = (cat(cache_k[0], k_s, KV_WIDTH), cat(cache_v[0], v_s, KV_WIDTH),
                            cat(cache_kidx[0], ki_s, IDX_DIM))
    n_t_s = db * slab // LT
    k_hm_s = k_all.reshape(1, db * slab, N_KV_HEADS, HEAD_DIM).transpose(0, 2, 1, 3)
    vt_s = v_all.reshape(n_t_s, LT, N_KV_HEADS, HEAD_DIM).transpose(0, 2, 3, 1)
    vt_s = jnp.concatenate([vt_s, jnp.ones((n_t_s, N_KV_HEADS, BF16_ROWS, LT), BF)], axis=2)
    vt_s = vt_s.reshape(1, n_t_s, N_KV_HEADS * V_ROWS, LT)
    bg_s = _attn_call(qt_s, qit_s, wit_s, zb_s, k_hm_s, vt_s, ki_all[None],
                      topk=min(TOPK_MAX, n_keys // 4), causal=False, stream_len=ds, n_valid=n_keys)
    y_sample = _merge_call(x_sample.reshape(1, n_tok, D_MODEL), ma_s, gb_s, bg_s, wts, tt=n_tok)

    return (y_prompt, y_sample.reshape(db, ds, D_MODEL),
            k.reshape(1, b, s, N_KV_HEADS, HEAD_DIM), v.reshape(1, b, s, N_KV_HEADS, HEAD_DIM),
            ki.reshape(1, b, s, IDX_DIM),
            k_s.reshape(1, db, ds, N_KV_HEADS, HEAD_DIM), v_s.reshape(1, db, ds, N_KV_HEADS, HEAD_DIM),
            ki_s.reshape(1, db, ds, IDX_DIM), vn_s.reshape(1, db, ds, SGU_WIDTH))
```

```python
import functools

import jax
import jax.numpy as jnp
from jax import lax
from jax.experimental import pallas as pl
from jax.experimental.pallas import tpu as pltpu

D_MODEL = 1024
CHUNK = 64
SGU_LEN = 128
SGU_GROUPS = 8
SGU_WIDTH = 512
SGU_GDIM = SGU_WIDTH // SGU_GROUPS
N_HEADS = 8
N_KV_HEADS = 4
HEAD_DIM = 64
GROUP = N_HEADS // N_KV_HEADS
ATT_WIDTH = N_HEADS * HEAD_DIM
KV_WIDTH = N_KV_HEADS * HEAD_DIM
N_IDX_HEADS = 8
IDX_DIM = 64
TOPK_MAX = 256
RMS_EPS = 1e-6
LN_EPS = 1e-5

BF = jnp.bfloat16
F32 = jnp.float32
I32 = jnp.int32

LANES = 128
SUBLANES = 8
BF16_ROWS = 16
TQ = LANES
LT = 512
V_ROWS = HEAD_DIM + BF16_ROWS
VMEM_LIMIT = 52 * 1024 * 1024
MASKED_LOGIT = -1e30
LOWEST = -3.4028235e38
MIN_SOFTMAX_SUM = 1e-30
ALL_TIES = 3e38
UNCHECKED_SEARCH_PASSES = 8
MAX_SEARCH_PASSES = 28
MAX_WALK_STEPS = 8
SEARCH_CLIP = 0.05
LOG2_E = 1.4426950408889634
INT_MIN = -2147483648


def _const_spec(shape):
    nd = len(shape)
    return pl.BlockSpec(shape, lambda *_: (0,) * nd, pipeline_mode=pl.Buffered(1))


def _proj_kernel(x_ref, ng_ref, wuvz_ref, wnat_ref, wg_ref, wt_ref,
                 lng_ref, lnb_ref, wmix_ref, bmix_ref, wpa_ref,
                 ma_ref, gb_ref, zb_ref, k_ref, v_ref, ki_ref, khm_ref, kib_ref,
                 qt_ref, qit_ref, wit_ref, vt_ref, *vn_refs, n_sub):
    x = x_ref[0]
    ms = jnp.mean(x * x, axis=-1, keepdims=True)
    h = (x * lax.rsqrt(ms + RMS_EPS) * ng_ref[...]).astype(BF)

    uvz = jnp.dot(h, wuvz_ref[...], preferred_element_type=F32)
    u = jax.nn.gelu(uvz[:, :SGU_WIDTH])
    v = jax.nn.gelu(uvz[:, SGU_WIDTH:2 * SGU_WIDTH])
    za = uvz[:, 2 * SGU_WIDTH:]
    mu = jnp.mean(v, axis=-1, keepdims=True)
    var = jnp.mean(jnp.square(v - mu), axis=-1, keepdims=True)
    vn = (v - mu) * lax.rsqrt(var + LN_EPS) * lng_ref[...] + lnb_ref[...]
    if vn_refs:
        vn_refs[0][0] = vn
    lane = lax.broadcasted_iota(I32, (SGU_LEN, LANES), 1)
    cols = []
    for c in range(SGU_WIDTH // LANES):
        rows = []
        for s in range(n_sub):
            blk = vn[s * SGU_LEN:(s + 1) * SGU_LEN, c * LANES:(c + 1) * LANES]
            lo = jnp.where(lane < SGU_GDIM, blk, 0.0).astype(BF)
            hi = jnp.where(lane >= SGU_GDIM, blk, 0.0).astype(BF)
            rhs = jnp.concatenate([lo, hi], axis=0)
            rows.append(jnp.dot(wmix_ref[c], rhs, preferred_element_type=F32))
        cols.append(jnp.concatenate(rows, axis=0) if n_sub > 1 else rows[0])
    mixed = jnp.concatenate(cols, axis=1)
    bias = bmix_ref[...]
    if n_sub > 1:
        bias = jnp.concatenate([bias] * n_sub, axis=0)
    a_out = u * (mixed + bias) * jax.nn.silu(za)
    pa = jnp.dot(a_out.astype(BF), wpa_ref[...], preferred_element_type=F32)
    g = jnp.dot(h, wg_ref[...], preferred_element_type=F32)
    ma_ref[0] = (jax.nn.sigmoid(g[:, :D_MODEL]) * pa).astype(BF)
    gb_ref[0] = g[:, D_MODEL:]

    nat = jnp.dot(h, wnat_ref[...], preferred_element_type=F32)
    k = nat[:, :KV_WIDTH]
    k_ref[0] = k
    v_ref[0] = nat[:, KV_WIDTH:2 * KV_WIDTH]
    zb_ref[0] = nat[:, 2 * KV_WIDTH:2 * KV_WIDTH + ATT_WIDTH]
    ki = nat[:, 2 * KV_WIDTH + ATT_WIDTH:]
    ki_ref[0] = ki
    kib_ref[0] = ki.astype(BF)
    for kv in range(N_KV_HEADS):
        khm_ref[0, kv] = k[:, kv * HEAD_DIM:(kv + 1) * HEAD_DIM].astype(BF)

    tm = lax.dot_general(wt_ref[...], h, (((1,), (1,)), ((), ())),
                         preferred_element_type=F32)
    o_qi, o_v, o_wi = ATT_WIDTH, 2 * ATT_WIDTH, 2 * ATT_WIDTH + KV_WIDTH
    for jb in range(n_sub):
        tok = slice(jb * TQ, (jb + 1) * TQ)
        for hh in range(N_HEADS):
            kv, gi = divmod(hh, GROUP)
            qt_ref[0, jb, kv, :, gi * TQ:(gi + 1) * TQ] = (
                tm[hh * HEAD_DIM:(hh + 1) * HEAD_DIM, tok] * (HEAD_DIM ** -0.5 * LOG2_E)).astype(BF)
        for hh in range(N_IDX_HEADS):
            qit_ref[0, jb, :, hh * TQ:(hh + 1) * TQ] = (
                tm[o_qi + hh * IDX_DIM:o_qi + (hh + 1) * IDX_DIM, tok] * (IDX_DIM ** -0.5)).astype(BF)
    for kv in range(N_KV_HEADS):
        vt_ref[0, 0, kv * V_ROWS:kv * V_ROWS + HEAD_DIM, :] = (
            tm[o_v + kv * HEAD_DIM:o_v + (kv + 1) * HEAD_DIM, :].astype(BF))
        vt_ref[0, 0, kv * V_ROWS + HEAD_DIM:(kv + 1) * V_ROWS, :] = jnp.ones(
            (BF16_ROWS, tm.shape[1]), BF)
    wit_ref[0] = tm[o_wi:o_wi + N_IDX_HEADS, :] * (N_IDX_HEADS ** -0.5)


def _proj_call(x, wts, wmix, bmix, *, tt, lt, emit_vn):
    b, s, _ = x.shape
    n_sub = tt // SGU_LEN
    r = lt // tt
    grid = (b, s // tt)
    tok = lambda w, dt=F32: (jax.ShapeDtypeStruct((b, s, w), dt),
                             pl.BlockSpec((1, tt, w), lambda i, j: (i, j, 0)))
    outs = [
        tok(D_MODEL, BF),
        tok(D_MODEL),
        tok(ATT_WIDTH),
        tok(KV_WIDTH), tok(KV_WIDTH), tok(IDX_DIM),
        (jax.ShapeDtypeStruct((b, N_KV_HEADS, s, HEAD_DIM), BF),
         pl.BlockSpec((1, N_KV_HEADS, tt, HEAD_DIM), lambda i, j: (i, 0, j, 0))),
        tok(IDX_DIM, BF),
        (jax.ShapeDtypeStruct((b, s // TQ, N_KV_HEADS, HEAD_DIM, GROUP * TQ), BF),
         pl.BlockSpec((1, n_sub, N_KV_HEADS, HEAD_DIM, GROUP * TQ), lambda i, j: (i, j, 0, 0, 0))),
        (jax.ShapeDtypeStruct((b, s // TQ, IDX_DIM, N_IDX_HEADS * TQ), BF),
         pl.BlockSpec((1, n_sub, IDX_DIM, N_IDX_HEADS * TQ), lambda i, j: (i, j, 0, 0))),
        (jax.ShapeDtypeStruct((b, N_IDX_HEADS, s), F32),
         pl.BlockSpec((1, N_IDX_HEADS, tt), lambda i, j: (i, 0, j))),
        (jax.ShapeDtypeStruct((b, s // lt, N_KV_HEADS * V_ROWS, lt), BF),
         pl.BlockSpec((1, 1, N_KV_HEADS * V_ROWS, tt), lambda i, j: (i, j // r, 0, j % r))),
    ]
    if emit_vn:
        outs.append(tok(SGU_WIDTH))
    consts = [wts["norm_g"], wts["w_uvz"], wts["w_nat"], wts["w_g"], wts["w_t"],
              wts["ln_g"], wts["ln_b"], wmix, bmix, wts["w_pa"]]
    return pl.pallas_call(
        functools.partial(_proj_kernel, n_sub=n_sub),
        grid=grid,
        in_specs=[pl.BlockSpec((1, tt, D_MODEL), lambda i, j: (i, j, 0))]
                 + [_const_spec(c.shape) for c in consts],
        out_specs=[o[1] for o in outs],
        out_shape=[o[0] for o in outs],
        compiler_params=pltpu.CompilerParams(
            dimension_semantics=("arbitrary", "arbitrary"), vmem_limit_bytes=VMEM_LIMIT),
        name="proj_vn" if emit_vn else "proj",
    )(x, *consts)


def _key_to_f32(key_u):
    s = key_u ^ INT_MIN
    bits = s ^ (lax.shift_right_arithmetic(s, 31) & 0x7FFFFFFF)
    return lax.bitcast_convert_type(bits, F32)


def _f32_to_key(x):
    bits = lax.bitcast_convert_type(x, I32)
    return bits ^ (lax.shift_right_arithmetic(bits, 31) & 0x7FFFFFFF) ^ INT_MIN


def _attn_kernel(qt_ref, qit_ref, wit_ref, zb_ref, k_ref, vt_ref, ki_ref, tri_ref,
                 o_ref, sc_ref, bot_ref, mma_ref, mmb_ref, *, n_tiles, topk, causal, stream_len,
                 n_valid):
    lane = lax.broadcasted_iota(I32, (1, TQ), 1)
    if causal:
        q0 = pl.program_id(1) * TQ
        n_t = lax.div(q0 + TQ + LT - 1, LT)
        key_lo = jnp.zeros((1, TQ), I32)
        key_hi = (lax.shift_right_logical(q0 + lane, 6) + 1) * CHUNK
    else:
        n_t = n_tiles
        slab = (n_tiles * LT) // (TQ // stream_len)
        key_lo = lax.div(lane, stream_len) * slab
        key_hi = key_lo + n_valid

    qit = qit_ref[0, 0]
    wit = wit_ref[0]
    wrows = [wit[hh:hh + 1, :] for hh in range(N_IDX_HEADS)]
    row0 = lax.broadcasted_iota(I32, (LT, 1), 0)

    fold = lambda x, op: op(x.reshape(LT // SUBLANES, SUBLANES, TQ), axis=0)

    last = n_t - 1
    n_pairs = lax.div(n_t, 2)
    odd = n_t - 2 * n_pairs == 1
    keep = lambda c: c

    def idx_dot(t, buf):
        start = pl.multiple_of(t * LT, LT)
        buf[...] = jnp.dot(ki_ref[0, pl.ds(start, LT), :], qit, preferred_element_type=F32)

    def score_tile(t, buf, carry):
        mx, mn, n_nonneg, n_pos = carry
        sc = jnp.zeros((LT, TQ), F32)
        for hh in range(N_IDX_HEADS):
            sc = sc + jnp.maximum(buf[:, hh * TQ:(hh + 1) * TQ], 0.0) * wrows[hh]
        pos = row0 + t * LT
        ok = pos < key_hi if causal else (pos >= key_lo) & (pos < key_hi)
        mn = jnp.minimum(mn, fold(sc, jnp.min))
        sc = jnp.where(ok, sc, -jnp.inf)
        sc_ref[t] = sc
        ones = lambda m: fold(jnp.where(m, 1, 0).astype(I32), jnp.sum)
        return (jnp.maximum(mx, fold(sc, jnp.max)), mn,
                n_nonneg + ones(sc >= 0.0), n_pos + ones(sc > 0.0))

    def score_pair(i, carry):
        t0 = 2 * i
        idx_dot(t0 + 1, mmb_ref)
        carry = score_tile(t0, mma_ref, carry)
        idx_dot(jnp.minimum(t0 + 2, last), mma_ref)
        return score_tile(t0 + 1, mmb_ref, carry)

    idx_dot(0, mma_ref)
    stats = lax.fori_loop(
        0, n_pairs, score_pair,
        (jnp.full((SUBLANES, TQ), -jnp.inf, F32), jnp.full((SUBLANES, TQ), jnp.inf, F32),
         jnp.zeros((SUBLANES, TQ), I32), jnp.zeros((SUBLANES, TQ), I32)))
    mx, mn, n_nonneg, n_pos = lax.cond(
        odd, lambda c: score_tile(last, mma_ref, c), keep, stats)
    mx = jnp.max(mx, axis=0, keepdims=True)
    mn = jnp.min(mn, axis=0, keepdims=True)
    n_nonneg = jnp.sum(n_nonneg, axis=0, keepdims=True).astype(F32)
    n_pos = jnp.sum(n_pos, axis=0, keepdims=True).astype(F32)

    def count(pred):
        ones = lambda t: fold(jnp.where(pred(sc_ref[t]), 1, 0).astype(I32), jnp.sum)

        acc = lax.fori_loop(0, n_pairs, lambda i, a: a + ones(2 * i) + ones(2 * i + 1),
                            jnp.zeros((SUBLANES, TQ), I32))
        acc = lax.cond(odd, lambda a: a + ones(last), keep, acc)
        return jnp.sum(acc, axis=0, keepdims=True)

    k_f = float(topk)
    n_adm = (key_hi - key_lo).astype(F32)
    short = n_adm < k_f

    def bisect_bits():
        def bit_pass(i, carry):
            thr_u, n_ge = carry
            cand_u = thr_u | lax.shift_left(jnp.int32(1), 31 - i)
            cand = _key_to_f32(cand_u)
            cnt = count(lambda sc: sc >= cand)
            take = cnt >= topk
            return jnp.where(take, cand_u, thr_u), jnp.where(take, cnt, n_ge)

        thr_u, n_ge = lax.fori_loop(
            0, 32, bit_pass, (jnp.zeros((1, TQ), I32), jnp.zeros((1, TQ), I32)))
        thr = jnp.where(short, LOWEST, _key_to_f32(thr_u))
        n_gt = count(lambda sc: sc > thr)
        need = jnp.where(short, ALL_TIES, (topk - n_gt).astype(F32))
        tie = jnp.logical_and(jnp.logical_not(short), n_ge > topk)
        return thr, need, jnp.max(jnp.where(tie, 1, 0))

    count_f = lambda pred: count(pred).astype(F32)
    zero_thr = jnp.logical_and(n_pos < k_f, n_nonneg >= k_f)
    pos_side = n_pos >= k_f
    lo0 = jnp.where(pos_side, 0.0, mn)
    hi0 = jnp.where(pos_side, mx, 0.0)
    clo0 = jnp.where(pos_side, n_nonneg, n_adm)
    chi0 = jnp.where(pos_side, 0.0, n_nonneg)
    logk = jnp.log(k_f)
    f_of = lambda c: jnp.log(jnp.maximum(c, 0.5)) - logk
    done0 = jnp.where(short | zero_thr | (clo0 == k_f), 1, 0)

    def search_pass(st):
        lo, hi, clo, chi, flo, fhi, side, done = st
        frac = jnp.clip(flo / (flo - fhi), SEARCH_CLIP, 1.0 - SEARCH_CLIP)
        cand = lo + (hi - lo) * frac
        cnt = count_f(lambda sc: sc >= cand)
        ge = cnt >= k_f
        up_lo = jnp.logical_and(done == 0, ge)
        up_hi = jnp.logical_and(done == 0, jnp.logical_not(ge))
        fhi = jnp.where(up_lo & (side == 1), fhi * 0.5, fhi)
        flo = jnp.where(up_hi & (side == -1), flo * 0.5, flo)
        lo = jnp.where(up_lo, cand, lo)
        clo = jnp.where(up_lo, cnt, clo)
        flo = jnp.where(up_lo, f_of(cnt), flo)
        hi = jnp.where(up_hi, cand, hi)
        chi = jnp.where(up_hi, cnt, chi)
        fhi = jnp.where(up_hi, f_of(cnt), fhi)
        side = jnp.where(up_lo, 1, jnp.where(up_hi, -1, side))
        done = jnp.where(clo == k_f, 1, done)
        return lo, hi, clo, chi, flo, fhi, side, done

    def walk_plan(st):
        lo, _, clo, chi = st[:4]
        from_hi = k_f - chi
        from_lo = clo - k_f + 1.0
        down = jnp.logical_or(from_hi <= from_lo, lo == 0.0)
        return down, jnp.where(st[7] == 1, 0.0, jnp.where(down, from_hi, from_lo))

    most_steps = lambda st: jnp.max(walk_plan(st)[1]).astype(I32)

    st = lax.fori_loop(0, UNCHECKED_SEARCH_PASSES, lambda i, s: search_pass(s),
                       (lo0, hi0, clo0, chi0, f_of(clo0), f_of(chi0), jnp.zeros((1, TQ), I32), done0))

    def checked_pass(c):
        s = search_pass(c[2])
        return c[0] + 1, most_steps(s), s

    _, n_steps, st = lax.while_loop(
        lambda c: jnp.logical_and(c[0] < MAX_SEARCH_PASSES, c[1] > MAX_WALK_STEPS), checked_pass,
        (jnp.int32(UNCHECKED_SEARCH_PASSES), most_steps(st), st))

    def walked():
        lo, hi = st[0], st[1]
        down, steps = walk_plan(st)
        sign = jnp.where(down, 1.0, -1.0)
        bound0 = jnp.where(down, hi, _key_to_f32(_f32_to_key(-lo) + 1))

        def step(i, bound):
            def below(t):
                y = sc_ref[t] * sign
                return fold(jnp.where(y < bound, y, -jnp.inf), jnp.max)

            nxt = lax.fori_loop(
                0, n_pairs, lambda j, a: jnp.maximum(a, jnp.maximum(below(2 * j), below(2 * j + 1))),
                jnp.full((SUBLANES, TQ), -jnp.inf, F32))
            nxt = lax.cond(odd, lambda a: jnp.maximum(a, below(last)), keep, nxt)
            return jnp.where(steps > i.astype(F32), jnp.max(nxt, axis=0, keepdims=True), bound)

        found = lax.fori_loop(0, n_steps, step, bound0) * sign
        thr = jnp.where(short, LOWEST,
                        jnp.where(zero_thr, 0.0, jnp.where(st[7] == 1, lo, found)))
        tie = jnp.logical_and(zero_thr, n_nonneg > k_f)
        need = jnp.where(tie, k_f - n_pos, ALL_TIES)
        return thr, need, jnp.max(jnp.where(tie, 1, 0))

    use_bits = n_steps > MAX_WALK_STEPS
    thr, need, n_tie = lax.cond(use_bits, bisect_bits, walked)

    gw = GROUP * TQ

    def qk_dots(t, buf, kvs=range(N_KV_HEADS)):
        start = pl.multiple_of(t * LT, LT)
        for kv in kvs:
            buf[:, kv * gw:(kv + 1) * gw] = jnp.dot(
                k_ref[0, kv, pl.ds(start, LT), :], qt_ref[0, 0, kv], preferred_element_type=F32)

    def attend(with_ties, thr, need):
        def attend_tile(t, buf, carry, nxt=None):
            n_eq, n_sel, ms, accs = carry
            sc = sc_ref[t]
            if with_ties:
                eq = sc == thr
                pref = jnp.dot(tri_ref[...], jnp.where(eq, 1.0, 0.0).astype(BF),
                               preferred_element_type=F32)
                sel = (sc > thr) | (eq & (n_eq + pref - 1.0 < need))
                n_eq = n_eq + pref[LT - 1:LT, :]
            else:
                sel = sc >= thr
            n_sel = n_sel + fold(jnp.where(sel, 1, 0).astype(I32), jnp.sum)
            keep01 = jnp.where(sel, 1.0, 0.0).astype(BF)
            keep01 = jnp.concatenate([keep01] * GROUP, axis=1)
            new_m, new_acc = [None] * N_KV_HEADS, [None] * N_KV_HEADS
            for half in (range(0, N_KV_HEADS // 2), range(N_KV_HEADS // 2, N_KV_HEADS)):
                if nxt is not None:
                    qk_dots(nxt[0], nxt[1], half)
                ps, alphas = {}, {}
                for kv in half:
                    lg = buf[:, kv * gw:(kv + 1) * gw]
                    m_new = jnp.maximum(ms[kv], jnp.max(lg, axis=0, keepdims=True))
                    ps[kv] = jnp.exp2(lg - m_new).astype(BF) * keep01
                    alphas[kv] = jnp.exp2(ms[kv] - m_new)
                    new_m[kv] = m_new
                for kv in half:
                    pv = jnp.dot(vt_ref[0, t, kv * V_ROWS:(kv + 1) * V_ROWS, :], ps[kv],
                                 preferred_element_type=F32)
                    new_acc[kv] = alphas[kv] * accs[kv] + pv
            return n_eq, n_sel, tuple(new_m), tuple(new_acc)

        def attend_pair(i, carry):
            t0 = 2 * i
            carry = attend_tile(t0, mma_ref, carry, (t0 + 1, mmb_ref))
            return attend_tile(t0 + 1, mmb_ref, carry, (jnp.minimum(t0 + 2, last), mma_ref))

        init = (jnp.zeros((1, TQ), F32), jnp.zeros((SUBLANES, TQ), I32),
                tuple(jnp.full((1, gw), MASKED_LOGIT, F32) for _ in range(N_KV_HEADS)),
                tuple(jnp.zeros((V_ROWS, gw), F32) for _ in range(N_KV_HEADS)))
        qk_dots(0, mma_ref)
        carry = lax.fori_loop(0, n_pairs, attend_pair, init)
        _, n_sel, _, accs = lax.cond(odd, lambda c: attend_tile(last, mma_ref, c), keep, carry)
        return accs, jnp.sum(n_sel, axis=0, keepdims=True)

    accs, n_sel = lax.cond(n_tie > 0, lambda: attend(True, thr, need),
                           lambda: attend(False, thr, need))

    def attend_exact(thr, need):
        def tile(t, carry):
            n_eq, ms, accs = carry
            sc = sc_ref[t]
            eq = sc == thr
            pref = jnp.dot(tri_ref[...], jnp.where(eq, 1.0, 0.0).astype(BF),
                           preferred_element_type=F32)
            sel = (sc > thr) | (eq & (n_eq + pref - 1.0 < need))
            n_eq = n_eq + pref[LT - 1:LT, :]
            bias = jnp.concatenate([jnp.where(sel, 0.0, MASKED_LOGIT)] * GROUP, axis=1)
            start = pl.multiple_of(t * LT, LT)
            new_m, new_acc = [], []
            for kv in range(N_KV_HEADS):
                lg = jnp.dot(k_ref[0, kv, pl.ds(start, LT), :], qt_ref[0, 0, kv],
                             preferred_element_type=F32) + bias
                m_new = jnp.maximum(ms[kv], jnp.max(lg, axis=0, keepdims=True))
                p = jnp.exp2(lg - m_new).astype(BF)
                pv = jnp.dot(vt_ref[0, t, kv * V_ROWS:(kv + 1) * V_ROWS, :], p,
                             preferred_element_type=F32)
                new_acc.append(jnp.exp2(ms[kv] - m_new) * accs[kv] + pv)
                new_m.append(m_new)
            return n_eq, tuple(new_m), tuple(new_acc)

        init = (jnp.zeros((1, TQ), F32),
                tuple(jnp.full((1, gw), MASKED_LOGIT, F32) for _ in range(N_KV_HEADS)),
                tuple(jnp.zeros((V_ROWS, gw), F32) for _ in range(N_KV_HEADS)))
        return lax.fori_loop(0, n_t, tile, init)[2]

    n_want = jnp.where(short, n_adm, k_f)
    n_miss = jnp.sum(jnp.where(n_sel.astype(F32) != n_want, 1, 0))
    n_tiny = sum(jnp.sum(jnp.where(a[HEAD_DIM:HEAD_DIM + 1] >= MIN_SOFTMAX_SUM, 0, 1)) for a in accs)

    def redo():
        thr_b, need_b, _ = bisect_bits()
        return attend_exact(thr_b, need_b)

    miss = jnp.logical_and(n_miss > 0, jnp.logical_not(use_bits))
    accs = lax.cond(jnp.logical_or(miss, n_tiny > 0), redo, lambda: accs)

    for kv in range(N_KV_HEADS):
        o_t = accs[kv][:HEAD_DIM] / accs[kv][HEAD_DIM:HEAD_DIM + 1]
        for gi in range(GROUP):
            hh = GROUP * kv + gi
            bot_ref[hh * HEAD_DIM:(hh + 1) * HEAD_DIM, :] = o_t[:, gi * TQ:(gi + 1) * TQ]
    o_ref[0] = (bot_ref[...].T * jax.nn.silu(zb_ref[0])).astype(BF)


def _attn_call(qt, qit, wit, zb, k_hm, vt, ki_b, *, topk, causal, stream_len=0, n_valid=0):
    b, nq = qt.shape[0], qt.shape[1]
    s = nq * TQ
    n_tiles = vt.shape[1]
    l_all = k_hm.shape[2]
    assert l_all == n_tiles * LT
    tri = (lax.broadcasted_iota(I32, (LT, LT), 0) >= lax.broadcasted_iota(I32, (LT, LT), 1)).astype(BF)
    whole = lambda shape: pl.BlockSpec((1,) + shape, lambda i, j: (i,) + (0,) * len(shape),
                                       pipeline_mode=pl.Buffered(1))
    return pl.pallas_call(
        functools.partial(_attn_kernel, n_tiles=n_tiles, topk=topk, causal=causal,
                          stream_len=stream_len, n_valid=n_valid),
        grid=(b, nq),
        in_specs=[
            pl.BlockSpec((1, 1, N_KV_HEADS, HEAD_DIM, GROUP * TQ), lambda i, j: (i, j, 0, 0, 0)),
            pl.BlockSpec((1, 1, IDX_DIM, N_IDX_HEADS * TQ), lambda i, j: (i, j, 0, 0)),
            pl.BlockSpec((1, N_IDX_HEADS, TQ), lambda i, j: (i, 0, j)),
            pl.BlockSpec((1, TQ, ATT_WIDTH), lambda i, j: (i, j, 0)),
            whole((N_KV_HEADS, l_all, HEAD_DIM)),
            whole((n_tiles, N_KV_HEADS * V_ROWS, LT)),
            whole((l_all, IDX_DIM)),
            _const_spec((LT, LT)),
        ],
        out_specs=pl.BlockSpec((1, TQ, ATT_WIDTH), lambda i, j: (i, j, 0)),
        out_shape=jax.ShapeDtypeStruct((b, s, ATT_WIDTH), BF),
        scratch_shapes=[pltpu.VMEM((n_tiles, LT, TQ), F32), pltpu.VMEM((ATT_WIDTH, TQ), F32),
                        pltpu.VMEM((LT, N_IDX_HEADS * TQ), F32), pltpu.VMEM((LT, N_IDX_HEADS * TQ), F32)],
        compiler_params=pltpu.CompilerParams(
            dimension_semantics=("arbitrary", "arbitrary"), vmem_limit_bytes=VMEM_LIMIT),
        name="attn_prompt" if causal else "attn_sample",
    )(qt, qit, wit, zb, k_hm, vt, ki_b, tri)


def _merge_kernel(x_ref, ma_ref, gb_ref, bg_ref, wpb_ref, wout_ref, fg_ref, y_ref):
    pb = jnp.dot(bg_ref[0], wpb_ref[...], preferred_element_type=F32)
    m = ma_ref[0].astype(F32) + jax.nn.sigmoid(gb_ref[0]) * pb
    xo = x_ref[0] + jnp.dot(m.astype(BF), wout_ref[...], preferred_element_type=F32)
    ms = jnp.mean(xo * xo, axis=-1, keepdims=True)
    y_ref[0] = xo * lax.rsqrt(ms + RMS_EPS) * fg_ref[...]


def _merge_call(x, ma, gb, bg, wts, *, tt):
    b, s, _ = x.shape
    tok = lambda w: pl.BlockSpec((1, tt, w), lambda i, j: (i, j, 0))
    consts = [wts["w_pb"], wts["w_out"], wts["final_g"]]
    if s // tt < 4:
        return pl.pallas_call(
            _merge_kernel,
            grid=(b, s // tt),
            in_specs=[tok(D_MODEL), tok(D_MODEL), tok(D_MODEL), tok(ATT_WIDTH)]
                     + [_const_spec(c.shape) for c in consts],
            out_specs=tok(D_MODEL),
            out_shape=jax.ShapeDtypeStruct((b, s, D_MODEL), F32),
            compiler_params=pltpu.CompilerParams(
                dimension_semantics=("arbitrary", "arbitrary"), vmem_limit_bytes=VMEM_LIMIT),
            name="merge",
        )(x, ma, gb, bg, *consts)

    deep = lambda w: pl.BlockSpec((1, tt, w), lambda i, j: (i, j, 0), pipeline_mode=pl.Buffered(3))

    def outer(x_hbm, ma_hbm, gb_hbm, bg_hbm, wpb_ref, wout_ref, fg_ref, y_hbm):
        def step(x_v, ma_v, gb_v, bg_v, y_v):
            _merge_kernel(x_v, ma_v, gb_v, bg_v, wpb_ref, wout_ref, fg_ref, y_v)
        pltpu.emit_pipeline(
            step, grid=(b, s // tt),
            in_specs=[deep(D_MODEL), deep(D_MODEL), deep(D_MODEL), deep(ATT_WIDTH)],
            out_specs=[tok(D_MODEL)],
        )(x_hbm, ma_hbm, gb_hbm, bg_hbm, y_hbm)

    anyspace = pl.BlockSpec(memory_space=pl.ANY)
    vmem = pl.BlockSpec(memory_space=pltpu.VMEM)
    return pl.pallas_call(
        outer,
        in_specs=[anyspace] * 4 + [vmem] * 3,
        out_specs=anyspace,
        out_shape=jax.ShapeDtypeStruct((b, s, D_MODEL), F32),
        compiler_params=pltpu.CompilerParams(vmem_limit_bytes=VMEM_LIMIT),
        name="merge_deep",
    )(x, ma, gb, bg, *consts)


def _prep_weights(norm_g, w_in, ln_g, ln_b, w_pa, w_pb, w_out, final_g):
    o = 0
    cols = {}
    for name, width in (("u", SGU_WIDTH), ("v", SGU_WIDTH), ("za", SGU_WIDTH), ("q", ATT_WIDTH),
                        ("k", KV_WIDTH), ("vv", KV_WIDTH), ("zb", ATT_WIDTH),
                        ("qi", N_IDX_HEADS * IDX_DIM), ("ki", IDX_DIM), ("wi", N_IDX_HEADS),
                        ("ga", D_MODEL), ("gb", D_MODEL)):
        cols[name] = (o, o + width)
        o += width
    wb = w_in.astype(BF)
    sl = lambda a, z: wb[:, cols[a][0]:cols[z][1]]
    return {
        "norm_g": norm_g.reshape(1, D_MODEL), "w_uvz": sl("u", "za"),
        "w_nat": jnp.concatenate([sl("k", "zb"), sl("ki", "ki")], axis=1),
        "w_g": sl("ga", "gb"),
        "w_t": jnp.concatenate([sl("q", "q"), sl("qi", "qi"), sl("vv", "vv"), sl("wi", "wi")], axis=1).T,
        "ln_g": ln_g.reshape(1, SGU_WIDTH), "ln_b": ln_b.reshape(1, SGU_WIDTH),
        "w_pa": w_pa.astype(BF), "w_pb": w_pb.astype(BF), "w_out": w_out.astype(BF),
        "final_g": final_g.reshape(1, D_MODEL),
    }


def _mix_operands(w_mix, b_pos):
    pairs = w_mix.reshape(SGU_GROUPS // 2, 2, SGU_LEN, SGU_LEN)
    wmix = jnp.concatenate([pairs[:, 0], pairs[:, 1]], axis=-1).astype(BF)
    bmix = jnp.repeat(b_pos, SGU_GDIM, axis=1)
    return wmix, bmix


def kernel(x_prompt, x_sample, cache_k, cache_v, cache_kidx, norm_g, w_in, sgu_ln_g, sgu_ln_b,
           sgu_w, sgu_b, w_pa, w_pb, w_out, final_g):
    b, s, _ = x_prompt.shape
    db, ds, _ = x_sample.shape
    past = cache_k.shape[2]
    assert norm_g.shape[0] == 1 and db * ds == TQ and s % LT == 0
    wts = _prep_weights(norm_g[0], w_in[0], sgu_ln_g[0], sgu_ln_b[0], w_pa[0], w_pb[0], w_out[0],
                        final_g)

    pos = jnp.arange(SGU_LEN)
    chunk_ok = (pos[None, :] // CHUNK) <= (pos[:, None] // CHUNK)
    wmix_p, bmix_p = _mix_operands(jnp.where(chunk_ok[None], sgu_w[0], 0.0), sgu_b[0].T)
    (ma, gb, zb, k, v, ki, k_hm, ki_b, qt, qit, wit, vt) = _proj_call(
        x_prompt, wts, wmix_p, bmix_p, tt=4 * TQ, lt=LT, emit_vn=False)
    bg = _attn_call(qt, qit, wit, zb, k_hm, vt, ki_b, topk=min(TOPK_MAX, s // 4), causal=True)
    y_prompt = _merge_call(x_prompt, ma, gb, bg, wts, tt=1024)

    assert past % CHUNK == 0 and ds <= CHUNK
    w_blk = jnp.einsum("ab,gij->gaibj", jnp.eye(db, dtype=F32), sgu_w[0][:, :ds, :ds])
    wmix_s, bmix_s = _mix_operands(w_blk.reshape(SGU_GROUPS, SGU_LEN, SGU_LEN),
                                   jnp.tile(sgu_b[0][:, :ds].T, (db, 1)))
    n_tok = db * ds
    (ma_s, gb_s, zb_s, k_s, v_s, ki_s, _, _, qt_s, qit_s, wit_s, _, vn_s) = _proj_call(
        x_sample.reshape(1, n_tok, D_MODEL), wts, wmix_s, bmix_s, tt=n_tok, lt=n_tok, emit_vn=True)
    n_keys = past + ds
    slab = -(-n_keys * db // LT) * LT // db
    assert (slab * db) % LT == 0 and slab >= n_keys
    cat = lambda c, n, w: jnp.pad(
        jnp.concatenate([c.reshape(db, past, w), n.reshape(db, ds, w)], axis=1),
        ((0, 0), (0, slab - n_keys), (0, 0))).reshape(db * slab, w).astype(BF)
    k_all, v_all, ki_all = (cat(cache_k[0], k_s, KV_WIDTH), cat(cache_v[0], v_s, KV_WIDTH),
                            cat(cache_kidx[0], ki_s, IDX_DIM))
    n_t_s = db * slab // LT
    k_hm_s = k_all.reshape(1, db * slab, N_KV_HEADS, HEAD_DIM).transpose(0, 2, 1, 3)
    vt_s = v_all.reshape(n_t_s, LT, N_KV_HEADS, HEAD_DIM).transpose(0, 2, 3, 1)
    vt_s = jnp.concatenate([vt_s, jnp.ones((n_t_s, N_KV_HEADS, BF16_ROWS, LT), BF)], axis=2)
    vt_s = vt_s.reshape(1, n_t_s, N_KV_HEADS * V_ROWS, LT)
    bg_s = _attn_call(qt_s, qit_s, wit_s, zb_s, k_hm_s, vt_s, ki_all[None],
                      topk=min(TOPK_MAX, n_keys // 4), causal=False, stream_len=ds, n_valid=n_keys)
    y_sample = _merge_call(x_sample.reshape(1, n_tok, D_MODEL), ma_s, gb_s, bg_s, wts, tt=n_tok)

    return (y_prompt, y_sample.reshape(db, ds, D_MODEL),
            k.reshape(1, b, s, N_KV_HEADS, HEAD_DIM), v.reshape(1, b, s, N_KV_HEADS, HEAD_DIM),
            ki.reshape(1, b, s, IDX_DIM),
            k_s.reshape(1, db, ds, N_KV_HEADS, HEAD_DIM), v_s.reshape(1, db, ds, N_KV_HEADS, HEAD_DIM),
            ki_s.reshape(1, db, ds, IDX_DIM), vn_s.reshape(1, db, ds, SGU_WIDTH))
```
